```python
import jax, jax.numpy as jnp
from jax import lax
import numpy as np

D_MODEL = 1024
BATCH = 1
SEQ = 16384
DEPTH = 1
DEC_BATCH = 2
DEC_SEQ = 8192
PAST_LEN = 128

MIX_WIDTH = D_MODEL
A_HEADS = 4
A_HEAD_DIM = MIX_WIDTH // 8
A_WIDTH = A_HEADS * A_HEAD_DIM
B_GROUPS = 4
B_GROUP_DIM = MIX_WIDTH // 8
B_WIDTH = B_GROUPS * B_GROUP_DIM
POOL_WINDOWS = (2, 4, 8, 16)
D_FF = ((8 * D_MODEL // 3 + 255) // 256) * 256
EPS = 1e-6

kernel_name = 'hybrid_fourier_pool_encoder'


def rmsnorm(x, g):
    xf = x.astype(jnp.float32)
    r = lax.rsqrt(jnp.mean(xf * xf, axis=-1, keepdims=True) + EPS)
    return (xf * r * g.astype(jnp.float32)).astype(x.dtype)


def fourier_mix(za, w_a):
    f = jnp.fft.fftn(za.astype(jnp.float32), axes=(1, 3), norm='ortho').real
    f = f.astype(za.dtype)
    return jnp.einsum('bshc,hcd->bshd', f, w_a)


def centred_mean(u, k):
    s = u.shape[1]
    cs = jnp.concatenate([jnp.zeros_like(u[:, :1]), jnp.cumsum(u, axis=1)], axis=1)
    t = jnp.arange(s)
    lo = jnp.clip(t - k // 2, 0, s)
    hi = jnp.clip(t + k // 2, 0, s)
    sums = jnp.take(cs, hi, axis=1) - jnp.take(cs, lo, axis=1)
    cnt = (hi - lo).astype(jnp.float32)
    return sums / cnt[None, :, None]


def pool_mix(zb, w_b, scale_b):
    zf = zb.astype(jnp.float32)
    pooled = jnp.stack([centred_mean(zf[:, :, g], POOL_WINDOWS[g]) for g in range(B_GROUPS)], axis=2)
    d = (pooled - zf).astype(zb.dtype)
    y = jnp.einsum('bsgc,gcd->bsgd', d, w_b)
    return y * scale_b.reshape(B_GROUPS, B_GROUP_DIM)


def encoder_layer(x, norm1_g, w_in, w_fourier, w_pool, pool_scale, w_out, norm2_g, w_gate, w_up, w_down):
    b, s, _ = x.shape
    h = rmsnorm(x, norm1_g)
    z = h @ w_in
    za = z[..., :A_WIDTH].reshape(b, s, A_HEADS, A_HEAD_DIM)
    zb = z[..., A_WIDTH:].reshape(b, s, B_GROUPS, B_GROUP_DIM)
    ya = fourier_mix(za, w_fourier).reshape(b, s, A_WIDTH)
    yb = pool_mix(zb, w_pool, pool_scale).reshape(b, s, B_WIDTH)
    x = x + jnp.concatenate([ya, yb], axis=-1) @ w_out
    h = rmsnorm(x, norm2_g)
    x = x + (jax.nn.silu(h @ w_gate) * (h @ w_up)) @ w_down
    return x


def trunk(x, norm1_g, w_in, w_fourier, w_pool, pool_scale, w_out, norm2_g, w_gate, w_up, w_down, normf_g):
    for l in range(DEPTH):
        x = encoder_layer(x, norm1_g[l], w_in[l], w_fourier[l], w_pool[l], pool_scale[l], w_out[l],
                          norm2_g[l], w_gate[l], w_up[l], w_down[l])
    return rmsnorm(x, normf_g)


def setup_inputs(seed: int = 0) -> dict:
    key = jax.random.key(seed)
    ks = jax.random.split(key, 16)
    f32 = jnp.float32
    def nrm(k, shape, fan_in):
        return jax.random.normal(k, shape, f32) * (fan_in ** -0.5)
    return {
        'x_prompt': jax.random.normal(ks[0], (BATCH, SEQ, D_MODEL), f32),
        'x_sample': jax.random.normal(ks[1], (DEC_BATCH, DEC_SEQ, D_MODEL), f32),
        'norm1_g': 1.0 + 0.02 * jax.random.normal(ks[2], (DEPTH, D_MODEL), f32),
        'w_in': nrm(ks[3], (DEPTH, D_MODEL, MIX_WIDTH), D_MODEL),
        'w_fourier': nrm(ks[4], (DEPTH, A_HEADS, A_HEAD_DIM, A_HEAD_DIM), A_HEAD_DIM),
        'w_pool': nrm(ks[5], (DEPTH, B_GROUPS, B_GROUP_DIM, B_GROUP_DIM), B_GROUP_DIM),
        'pool_scale': 1.0 + 0.02 * jax.random.normal(ks[6], (DEPTH, B_WIDTH), f32),
        'w_out': nrm(ks[7], (DEPTH, MIX_WIDTH, D_MODEL), MIX_WIDTH),
        'norm2_g': 1.0 + 0.02 * jax.random.normal(ks[8], (DEPTH, D_MODEL), f32),
        'w_gate': nrm(ks[9], (DEPTH, D_MODEL, D_FF), D_MODEL),
        'w_up': nrm(ks[10], (DEPTH, D_MODEL, D_FF), D_MODEL),
        'w_down': nrm(ks[11], (DEPTH, D_FF, D_MODEL), D_FF),
        'normf_g': 1.0 + 0.02 * jax.random.normal(ks[12], (D_MODEL,), f32),
    }


def reference(x_prompt, x_sample, norm1_g, w_in, w_fourier, w_pool, pool_scale, w_out, norm2_g, w_gate, w_up, w_down, normf_g):
    y_prompt = trunk(x_prompt, norm1_g, w_in, w_fourier, w_pool, pool_scale, w_out, norm2_g, w_gate, w_up, w_down, normf_g)
    y_sample = trunk(x_sample, norm1_g, w_in, w_fourier, w_pool, pool_scale, w_out, norm2_g, w_gate, w_up, w_down, normf_g)
    return (y_prompt, y_sample)
```

```python
import functools

import jax
import jax.numpy as jnp
import numpy as np
from jax.experimental import pallas as pl
from jax.experimental.pallas import tpu as pltpu

F32 = jnp.float32
BF16 = jnp.bfloat16

D_MODEL = 1024
N_HEADS = 4
HEAD_DIM = 128
A_WIDTH = N_HEADS * HEAD_DIM
B_WIDTH = N_HEADS * HEAD_DIM
POOL_WINDOWS = (2, 4, 8, 16)
D_FF = 2816
EPS = 1e-6

HALO = 8
TOKEN_TILE = 512
FF_CHUNK = 256
DFT_SPLIT = 128
S2_BLOCK = 8
K1_COLS = 2048
VMEM_LIMIT_V7X = 56 * 1024 * 1024


def _const_spec(shape):
    zeros = (0,) * len(shape)
    return pl.BlockSpec(shape, lambda *_: zeros, pipeline_mode=pl.Buffered(1))


def _rmsnorm(x, g):
    r = jax.lax.rsqrt(jnp.mean(x * x, axis=-1, keepdims=True) + EPS)
    return x * r * g


def _dot(a, b):
    return jnp.dot(a, b, preferred_element_type=F32)


def _split_bf16(a):
    hi = a.astype(BF16)
    lo = (a - hi.astype(F32)).astype(BF16)
    return hi, lo


def _dot_f32(a, b):
    ah, al = _split_bf16(a)
    bh, bl = _split_bf16(b)
    return _dot(ah, bh) + (_dot(ah, bl) + _dot(al, bh))


def _prep_kernel(cs_ref, wf_ref, w2_ref):
    for h in range(N_HEADS):
        w = wf_ref[h]
        w2_ref[h, :, :HEAD_DIM] = _dot_f32(cs_ref[0], w).astype(BF16)
        w2_ref[h, :, HEAD_DIM:] = (-_dot_f32(cs_ref[1], w)).astype(BF16)


def _prep(w_fourier):
    n = np.arange(HEAD_DIM)
    ang = 2.0 * np.pi * np.outer(n, n) / HEAD_DIM
    cs = np.stack([np.cos(ang), np.sin(ang)]) / np.sqrt(HEAD_DIM)
    return pl.pallas_call(
        _prep_kernel,
        out_shape=jax.ShapeDtypeStruct((N_HEADS, HEAD_DIM, 2 * HEAD_DIM), BF16),
        name="prep_w2",
    )(jnp.asarray(cs, F32), w_fourier)


def _k1_kernel(x_ref, g1_ref, win_ref, w2_ref, u_ref):
    h = _rmsnorm(x_ref[0], g1_ref[...]).astype(BF16)
    za = _dot(h, win_ref[...]).astype(BF16)
    for hd in range(N_HEADS):
        ab = _dot(za[:, hd * HEAD_DIM:(hd + 1) * HEAD_DIM], w2_ref[hd])
        u_ref[0, 0, hd] = ab[:, :HEAD_DIM].astype(BF16)
        u_ref[0, 1, hd] = ab[:, HEAD_DIM:].astype(BF16)


def _k1(x, g1, w_in_a, w2):
    b, s, _ = x.shape
    tm = TOKEN_TILE
    return pl.pallas_call(
        _k1_kernel,
        grid=(b, s // tm),
        in_specs=[
            pl.BlockSpec((1, tm, D_MODEL), lambda bi, i: (bi, i, 0)),
            _const_spec((1, D_MODEL)),
            _const_spec((D_MODEL, A_WIDTH)),
            _const_spec((N_HEADS, HEAD_DIM, 2 * HEAD_DIM)),
        ],
        out_specs=pl.BlockSpec((1, 2, N_HEADS, tm, HEAD_DIM), lambda bi, i: (bi, 0, 0, i, 0)),
        out_shape=jax.ShapeDtypeStruct((b, 2, N_HEADS, s, HEAD_DIM), BF16),
        compiler_params=pltpu.CompilerParams(
            dimension_semantics=("parallel", "parallel"), vmem_limit_bytes=VMEM_LIMIT_V7X),
        name="k1_in_proj",
    )(x, g1, w_in_a, w2)


def _k2a_kernel(u_ref, f1_ref, tr_ref, ts_ref, g_ref, *, n1, bs2):
    tr = tr_ref[...]
    ts = ts_ref[...]
    for hd in range(N_HEADS):
        x = jnp.concatenate([u_ref[0, 0, hd], u_ref[0, 1, hd]], axis=0)
        g = _dot(f1_ref[...], x)
        gr = g[:n1]
        gi = g[n1:]
        pr = (gr * tr + gi * ts).astype(BF16)
        pi = (gi * tr - gr * ts).astype(BF16)
        for j in range(bs2):
            g_ref[0, hd, 0, j] = pr[:, j * HEAD_DIM:(j + 1) * HEAD_DIM]
            g_ref[0, hd, 1, j] = pi[:, j * HEAD_DIM:(j + 1) * HEAD_DIM]


def _k2b_kernel(g_ref, f3_ref, y_ref, *, n2):
    x = g_ref[0, 0].reshape(2 * n2, g_ref.shape[-1])
    y_ref[0, 0] = _dot(f3_ref[...], x).astype(BF16)


def _dft_tables(s, n1):
    n2 = s // n1
    k1 = np.arange(n1)
    a1 = 2.0 * np.pi * np.outer(k1, k1) / n1
    c1, s1 = np.cos(a1), np.sin(a1)
    f1 = np.block([[c1, s1], [-s1, c1]]) / np.sqrt(n1)
    at = 2.0 * np.pi * np.outer(k1, np.arange(n2)) / s
    k2 = np.arange(n2)
    a2 = 2.0 * np.pi * np.outer(k2, k2) / n2
    f3 = np.concatenate([np.cos(a2), np.sin(a2)], axis=1) / np.sqrt(n2)
    return (jnp.asarray(f1, F32).astype(BF16), jnp.asarray(np.cos(at), F32), jnp.asarray(np.sin(at), F32),
            jnp.asarray(f3, F32).astype(BF16))


def _seq_dft(u):
    b, _, _, s, _ = u.shape
    n1 = DFT_SPLIT
    n2 = s // n1
    bs2 = S2_BLOCK
    f1, tr, ts, f3 = _dft_tables(s, n1)
    tr = jnp.broadcast_to(tr[:, :, None], (n1, n2, HEAD_DIM)).reshape(n1, n2 * HEAD_DIM)
    ts = jnp.broadcast_to(ts[:, :, None], (n1, n2, HEAD_DIM)).reshape(n1, n2 * HEAD_DIM)
    cols = bs2 * HEAD_DIM
    g = pl.pallas_call(
        functools.partial(_k2a_kernel, n1=n1, bs2=bs2),
        grid=(b, n2 // bs2),
        in_specs=[
            pl.BlockSpec((1, 2, N_HEADS, n1, cols), lambda bi, j: (bi, 0, 0, 0, j)),
            _const_spec((2 * n1, 2 * n1)),
            pl.BlockSpec((n1, cols), lambda bi, j: (0, j)),
            pl.BlockSpec((n1, cols), lambda bi, j: (0, j)),
        ],
        out_specs=pl.BlockSpec((1, N_HEADS, 2, bs2, n1, HEAD_DIM), lambda bi, j: (bi, 0, 0, j, 0, 0)),
        out_shape=jax.ShapeDtypeStruct((b, N_HEADS, 2, n2, n1, HEAD_DIM), BF16),
        compiler_params=pltpu.CompilerParams(
            dimension_semantics=("parallel", "parallel"), vmem_limit_bytes=VMEM_LIMIT_V7X),
        name="k2a_dft_stage1",
    )(u.reshape(b, 2, N_HEADS, n1, n2 * HEAD_DIM), f1, tr, ts)

    kc = K1_COLS
    y = pl.pallas_call(
        functools.partial(_k2b_kernel, n2=n2),
        grid=(b, N_HEADS, n1 * HEAD_DIM // kc),
        in_specs=[
            pl.BlockSpec((1, 1, 2, n2, kc), lambda bi, h, j: (bi, h, 0, 0, j)),
            _const_spec((n2, 2 * n2)),
        ],
        out_specs=pl.BlockSpec((1, 1, n2, kc), lambda bi, h, j: (bi, h, 0, j)),
        out_shape=jax.ShapeDtypeStruct((b, N_HEADS, n2, n1 * HEAD_DIM), BF16),
        compiler_params=pltpu.CompilerParams(
            dimension_semantics=("parallel", "parallel", "parallel"), vmem_limit_bytes=VMEM_LIMIT_V7X),
        name="k2b_dft_stage2",
    )(g.reshape(b, N_HEADS, 2, n2, n1 * HEAD_DIM), f3)
    return y.reshape(b, N_HEADS, s, HEAD_DIM)


def _k3_kernel(x_ref, xp_ref, xn_ref, ya_ref, g1_ref, winb_ref, wp_ref, ps_ref, wout_ref, g2_ref,
               wg_ref, wu_ref, wd_ref, gf_ref, o_ref, zs_ref, act_ref, *, seq_len):
    tm = x_ref.shape[1]
    i = pl.program_id(1)
    last = pl.num_programs(1) - 1
    x = x_ref[0]

    g1 = g1_ref[...]
    winb = winb_ref[...]
    zs_ref[HALO:HALO + tm] = _dot(_rmsnorm(x, g1).astype(BF16), winb)
    zp = _dot(_rmsnorm(xp_ref[0], g1).astype(BF16), winb)
    zn = _dot(_rmsnorm(xn_ref[0], g1).astype(BF16), winb)
    zs_ref[:HALO] = jnp.where(i > 0, zp, 0.0)
    zs_ref[HALO + tm:] = jnp.where(i < last, zn, 0.0)

    t = i * tm + jax.lax.broadcasted_iota(jnp.int32, (tm, 1), 0)
    yb = []
    for gi, k in enumerate(POOL_WINDOWS):
        lanes = slice(gi * HEAD_DIM, (gi + 1) * HEAD_DIM)
        sums = zs_ref[HALO - k // 2:HALO - k // 2 + tm, lanes]
        for j in range(1, k):
            sums = sums + zs_ref[HALO - k // 2 + j:HALO - k // 2 + j + tm, lanes]
        cnt = jnp.minimum(t + k // 2, seq_len) - jnp.maximum(t - k // 2, 0)
        d = sums / cnt.astype(F32) - zs_ref[HALO:HALO + tm, lanes]
        yb.append((_dot(d.astype(BF16), wp_ref[gi]) * ps_ref[:, lanes]).astype(BF16))

    mix = jnp.concatenate([ya_ref[0, hd] for hd in range(N_HEADS)] + yb, axis=-1)
    x1 = x + _dot(mix, wout_ref[...])
    h2 = _rmsnorm(x1, g2_ref[...]).astype(BF16)
    for c in range(D_FF // FF_CHUNK):
        cols = slice(c * FF_CHUNK, (c + 1) * FF_CHUNK)
        gate = _dot(h2, wg_ref[:, cols])
        up = _dot(h2, wu_ref[:, cols])
        act_ref[:, cols] = (gate * jax.nn.sigmoid(gate) * up).astype(BF16)
    x2 = x1 + _dot(act_ref[...], wd_ref[...])
    o_ref[0] = _rmsnorm(x2, gf_ref[...])


def _k3(x, ya, g1, w_in_b, w_pool, pool_scale, w_out, g2, w_gate, w_up, w_down, gf):
    b, s, _ = x.shape
    tm = TOKEN_TILE
    hb = tm // HALO
    n_hb = s // HALO
    return pl.pallas_call(
        functools.partial(_k3_kernel, seq_len=s),
        grid=(b, s // tm),
        in_specs=[
            pl.BlockSpec((1, tm, D_MODEL), lambda bi, i: (bi, i, 0)),
            pl.BlockSpec((1, HALO, D_MODEL), lambda bi, i: (bi, jnp.maximum(i * hb - 1, 0), 0)),
            pl.BlockSpec((1, HALO, D_MODEL), lambda bi, i: (bi, jnp.minimum((i + 1) * hb, n_hb - 1), 0)),
            pl.BlockSpec((1, N_HEADS, tm, HEAD_DIM), lambda bi, i: (bi, 0, i, 0)),
            _const_spec((1, D_MODEL)),
            _const_spec((D_MODEL, B_WIDTH)),
            _const_spec((N_HEADS, HEAD_DIM, HEAD_DIM)),
            _const_spec((1, B_WIDTH)),
            _const_spec((D_MODEL, D_MODEL)),
            _const_spec((1, D_MODEL)),
            _const_spec((D_MODEL, D_FF)),
            _const_spec((D_MODEL, D_FF)),
            _const_spec((D_FF, D_MODEL)),
            _const_spec((1, D_MODEL)),
        ],
        out_specs=pl.BlockSpec((1, tm, D_MODEL), lambda bi, i: (bi, i, 0)),
        out_shape=jax.ShapeDtypeStruct((b, s, D_MODEL), F32),
        scratch_shapes=[
            pltpu.VMEM((tm + 2 * HALO, B_WIDTH), F32),
            pltpu.VMEM((tm, D_FF), BF16),
        ],
        compiler_params=pltpu.CompilerParams(
            dimension_semantics=("parallel", "parallel"), vmem_limit_bytes=VMEM_LIMIT_V7X),
        name="k3_pool_out_ffn",
    )(x, x, x, ya, g1, w_in_b, w_pool, pool_scale, w_out, g2, w_gate, w_up, w_down, gf)


def kernel(x_prompt, x_sample, norm1_g, w_in, w_fourier, w_pool, pool_scale, w_out, norm2_g,
           w_gate, w_up, w_down, normf_g):
    assert norm1_g.shape[0] == 1, "single-layer block"
    g1 = norm1_g[0][None]
    g2 = norm2_g[0][None]
    gf = normf_g[None]
    w_in_a = w_in[0, :, :A_WIDTH].astype(BF16)
    w_in_b = w_in[0, :, A_WIDTH:].astype(BF16)
    w_p = w_pool[0].astype(BF16)
    p_s = pool_scale[0][None]
    w_o = w_out[0].astype(BF16)
    w_g = w_gate[0].astype(BF16)
    w_u = w_up[0].astype(BF16)
    w_d = w_down[0].astype(BF16)
    w2 = _prep(w_fourier[0])

    def trunk(x):
        u = _k1(x, g1, w_in_a, w2)
        ya = _seq_dft(u)
        return _k3(x, ya, g1, w_in_b, w_p, p_s, w_o, g2, w_g, w_u, w_d, gf)

    return trunk(x_prompt), trunk(x_sample)
```

```python
import functools

import jax
import jax.numpy as jnp
import numpy as np
from jax.experimental import pallas as pl
from jax.experimental.pallas import tpu as pltpu

F32 = jnp.float32
BF16 = jnp.bfloat16

D_MODEL = 1024
N_HEADS = 4
HEAD_DIM = 128
A_WIDTH = N_HEADS * HEAD_DIM
B_WIDTH = N_HEADS * HEAD_DIM
POOL_WINDOWS = (2, 4, 8, 16)
D_FF = 2816
EPS = 1e-6

HALO = 8
TOKEN_TILE = 512
FF_CHUNK = 256
DFT_SPLIT = 128
S2_BLOCK = 8
K1_BLOCK = 8
VMEM_LIMIT_V7X = 56 * 1024 * 1024


def _const_spec(shape):
    zeros = (0,) * len(shape)
    return pl.BlockSpec(shape, lambda *_: zeros, pipeline_mode=pl.Buffered(1))


def _rmsnorm(x, g):
    r = jax.lax.rsqrt(jnp.mean(x * x, axis=-1, keepdims=True) + EPS)
    return x * r * g


def _dot(a, b):
    return jnp.dot(a, b, preferred_element_type=F32)


def _split_bf16(a):
    hi = a.astype(BF16)
    lo = (a - hi.astype(F32)).astype(BF16)
    return hi, lo


def _dot_f32(a, b):
    ah, al = _split_bf16(a)
    bh, bl = _split_bf16(b)
    return _dot(ah, bh) + (_dot(ah, bl) + _dot(al, bh))


def _prep_kernel(cs_ref, wf_ref, w2_ref):
    for h in range(N_HEADS):
        w = wf_ref[h]
        w2_ref[h, :, :HEAD_DIM] = _dot_f32(cs_ref[0], w).astype(BF16)
        w2_ref[h, :, HEAD_DIM:] = (-_dot_f32(cs_ref[1], w)).astype(BF16)


def _prep(w_fourier):
    n = np.arange(HEAD_DIM)
    ang = 2.0 * np.pi * np.outer(n, n) / HEAD_DIM
    cs = np.stack([np.cos(ang), np.sin(ang)]) / np.sqrt(HEAD_DIM)
    return pl.pallas_call(
        _prep_kernel,
        out_shape=jax.ShapeDtypeStruct((N_HEADS, HEAD_DIM, 2 * HEAD_DIM), BF16),
        name="prep_w2",
    )(jnp.asarray(cs, F32), w_fourier)


def _k1_kernel(x_ref, g1_ref, win_ref, w2_ref, f1_ref, tr_ref, ts_ref, g_ref, ab_ref, *, n1):
    rows = n1 * S2_BLOCK
    x = x_ref[0].reshape(rows, D_MODEL)
    h = _rmsnorm(x, g1_ref[...]).astype(BF16)
    za = _dot(h, win_ref[...]).astype(BF16)
    for hd in range(N_HEADS):
        ab = _dot(za[:, hd * HEAD_DIM:(hd + 1) * HEAD_DIM], w2_ref[hd])
        ab_ref[hd, 0] = ab[:, :HEAD_DIM]
        ab_ref[hd, 1] = ab[:, HEAD_DIM:]
    tr = tr_ref[...]
    ts = ts_ref[...]
    for hd in range(N_HEADS):
        xx = jnp.concatenate(
            [jnp.concatenate([ab_ref.at[hd, ri][pl.ds(j, n1, stride=S2_BLOCK), :]
                              for j in range(S2_BLOCK)], axis=1) for ri in range(2)],
            axis=0).astype(BF16)
        g = _dot(f1_ref[...], xx)
        gr = g[:n1]
        gi = g[n1:]
        pr = gr * tr + gi * ts
        pi = gi * tr - gr * ts
        out_r = g_ref.at[0, hd, 0].reshape(rows, HEAD_DIM)
        out_i = g_ref.at[0, hd, 1].reshape(rows, HEAD_DIM)
        for j in range(S2_BLOCK):
            out_r[pl.ds(j, n1, stride=S2_BLOCK), :] = pr[:, j * HEAD_DIM:(j + 1) * HEAD_DIM]
            out_i[pl.ds(j, n1, stride=S2_BLOCK), :] = pi[:, j * HEAD_DIM:(j + 1) * HEAD_DIM]


def _dft_tables(s, n1):
    n2 = s // n1
    k1 = np.arange(n1)
    a1 = 2.0 * np.pi * np.outer(k1, k1) / n1
    c1, s1 = np.cos(a1), np.sin(a1)
    f1 = np.block([[c1, s1], [-s1, c1]]) / np.sqrt(n1)
    at = 2.0 * np.pi * np.outer(k1, np.arange(n2)) / s
    k2 = np.arange(n2)
    a2 = 2.0 * np.pi * np.outer(k2, k2) / n2
    f3 = np.concatenate([np.cos(a2), np.sin(a2)], axis=1) / np.sqrt(n2)
    tr = jnp.broadcast_to(jnp.asarray(np.cos(at), F32)[:, :, None], (n1, n2, HEAD_DIM))
    ts = jnp.broadcast_to(jnp.asarray(np.sin(at), F32)[:, :, None], (n1, n2, HEAD_DIM))
    return (jnp.asarray(f1, F32).astype(BF16), tr.reshape(n1, n2 * HEAD_DIM),
            ts.reshape(n1, n2 * HEAD_DIM), jnp.asarray(f3, F32).astype(BF16))


def _k1(x, g1, w_in_a, w2, f1, tr, ts):
    b, s, _ = x.shape
    n1 = DFT_SPLIT
    n2 = s // n1
    cols = S2_BLOCK * HEAD_DIM
    return pl.pallas_call(
        functools.partial(_k1_kernel, n1=n1),
        grid=(b, n2 // S2_BLOCK),
        in_specs=[
            pl.BlockSpec((1, n1, S2_BLOCK, D_MODEL), lambda bi, j: (bi, 0, j, 0)),
            _const_spec((1, D_MODEL)),
            _const_spec((D_MODEL, A_WIDTH)),
            _const_spec((N_HEADS, HEAD_DIM, 2 * HEAD_DIM)),
            _const_spec((2 * n1, 2 * n1)),
            pl.BlockSpec((n1, cols), lambda bi, j: (0, j)),
            pl.BlockSpec((n1, cols), lambda bi, j: (0, j)),
        ],
        out_specs=pl.BlockSpec((1, N_HEADS, 2, n1, S2_BLOCK, HEAD_DIM), lambda bi, j: (bi, 0, 0, 0, j, 0)),
        out_shape=jax.ShapeDtypeStruct((b, N_HEADS, 2, n1, n2, HEAD_DIM), F32),
        scratch_shapes=[pltpu.VMEM((N_HEADS, 2, n1 * S2_BLOCK, HEAD_DIM), F32)],
        compiler_params=pltpu.CompilerParams(
            dimension_semantics=("parallel", "parallel"), vmem_limit_bytes=VMEM_LIMIT_V7X),
        name="k1_in_proj_dft1",
    )(x.reshape(b, n1, n2, D_MODEL), g1, w_in_a, w2, f1, tr, ts)


def _k2_kernel(g_ref, f3_ref, y_ref, *, n2):
    for k in range(K1_BLOCK):
        x = jnp.concatenate(
            [jnp.concatenate([g_ref[0, hd, ri, k] for hd in range(N_HEADS)], axis=1) for ri in range(2)],
            axis=0).astype(BF16)
        r = _dot(f3_ref[...], x)
        for hd in range(N_HEADS):
            out = y_ref.at[0, hd].reshape(n2 * K1_BLOCK, HEAD_DIM)
            out[pl.ds(k, n2, stride=K1_BLOCK), :] = r[:, hd * HEAD_DIM:(hd + 1) * HEAD_DIM]


def _k2(g, f3):
    b, _, _, n1, n2, _ = g.shape
    y = pl.pallas_call(
        functools.partial(_k2_kernel, n2=n2),
        grid=(b, n1 // K1_BLOCK),
        in_specs=[
            pl.BlockSpec((1, N_HEADS, 2, K1_BLOCK, n2, HEAD_DIM), lambda bi, j: (bi, 0, 0, j, 0, 0)),
            _const_spec((n2, 2 * n2)),
        ],
        out_specs=pl.BlockSpec((1, N_HEADS, n2, K1_BLOCK, HEAD_DIM), lambda bi, j: (bi, 0, 0, j, 0)),
        out_shape=jax.ShapeDtypeStruct((b, N_HEADS, n2, n1, HEAD_DIM), F32),
        compiler_params=pltpu.CompilerParams(
            dimension_semantics=("parallel", "parallel"), vmem_limit_bytes=VMEM_LIMIT_V7X),
        name="k2_dft2",
    )(g, f3)
    return y.reshape(b, N_HEADS, n1 * n2, HEAD_DIM)


def _k3_kernel(x_ref, xp_ref, xn_ref, ya_ref, g1_ref, winb_ref, wp_ref, ps_ref, wout_ref, g2_ref,
               wg_ref, wu_ref, wd_ref, gf_ref, o_ref, zs_ref, act_ref, *, seq_len):
    tm = x_ref.shape[1]
    i = pl.program_id(1)
    last = pl.num_programs(1) - 1
    x = x_ref[0]

    g1 = g1_ref[...]
    winb = winb_ref[...]
    zs_ref[HALO:HALO + tm] = _dot(_rmsnorm(x, g1).astype(BF16), winb)
    zp = _dot(_rmsnorm(xp_ref[0], g1).astype(BF16), winb)
    zn = _dot(_rmsnorm(xn_ref[0], g1).astype(BF16), winb)
    zs_ref[:HALO] = jnp.where(i > 0, zp, 0.0)
    zs_ref[HALO + tm:] = jnp.where(i < last, zn, 0.0)

    t = i * tm + jax.lax.broadcasted_iota(jnp.int32, (tm, 1), 0)
    yb = []
    for gi, k in enumerate(POOL_WINDOWS):
        lanes = slice(gi * HEAD_DIM, (gi + 1) * HEAD_DIM)
        sums = zs_ref[HALO - k // 2:HALO - k // 2 + tm, lanes]
        for j in range(1, k):
            sums = sums + zs_ref[HALO - k // 2 + j:HALO - k // 2 + j + tm, lanes]
        cnt = jnp.minimum(t + k // 2, seq_len) - jnp.maximum(t - k // 2, 0)
        d = sums / cnt.astype(F32) - zs_ref[HALO:HALO + tm, lanes]
        yb.append((_dot(d.astype(BF16), wp_ref[gi]) * ps_ref[:, lanes]).astype(BF16))

    ya = [ya_ref[0, hd].astype(BF16) for hd in range(N_HEADS)]
    mix = jnp.concatenate(ya + yb, axis=-1)
    x1 = x + _dot(mix, wout_ref[...])
    h2 = _rmsnorm(x1, g2_ref[...]).astype(BF16)
    for c in range(D_FF // FF_CHUNK):
        cols = slice(c * FF_CHUNK, (c + 1) * FF_CHUNK)
        gate = _dot(h2, wg_ref[:, cols])
        up = _dot(h2, wu_ref[:, cols])
        act_ref[:, cols] = (gate * jax.nn.sigmoid(gate) * up).astype(BF16)
    x2 = x1 + _dot(act_ref[...], wd_ref[...])
    o_ref[0] = _rmsnorm(x2, gf_ref[...])


def _k3(x, ya, g1, w_in_b, w_pool, pool_scale, w_out, g2, w_gate, w_up, w_down, gf):
    b, s, _ = x.shape
    tm = TOKEN_TILE
    hb = tm // HALO
    n_hb = s // HALO
    return pl.pallas_call(
        functools.partial(_k3_kernel, seq_len=s),
        grid=(b, s // tm),
        in_specs=[
            pl.BlockSpec((1, tm, D_MODEL), lambda bi, i: (bi, i, 0)),
            pl.BlockSpec((1, HALO, D_MODEL), lambda bi, i: (bi, jnp.maximum(i * hb - 1, 0), 0)),
            pl.BlockSpec((1, HALO, D_MODEL), lambda bi, i: (bi, jnp.minimum((i + 1) * hb, n_hb - 1), 0)),
            pl.BlockSpec((1, N_HEADS, tm, HEAD_DIM), lambda bi, i: (bi, 0, i, 0)),
            _const_spec((1, D_MODEL)),
            _const_spec((D_MODEL, B_WIDTH)),
            _const_spec((N_HEADS, HEAD_DIM, HEAD_DIM)),
            _const_spec((1, B_WIDTH)),
            _const_spec((D_MODEL, D_MODEL)),
            _const_spec((1, D_MODEL)),
            _const_spec((D_MODEL, D_FF)),
            _const_spec((D_MODEL, D_FF)),
            _const_spec((D_FF, D_MODEL)),
            _const_spec((1, D_MODEL)),
        ],
        out_specs=pl.BlockSpec((1, tm, D_MODEL), lambda bi, i: (bi, i, 0)),
        out_shape=jax.ShapeDtypeStruct((b, s, D_MODEL), F32),
        scratch_shapes=[
            pltpu.VMEM((tm + 2 * HALO, B_WIDTH), F32),
            pltpu.VMEM((tm, D_FF), BF16),
        ],
        compiler_params=pltpu.CompilerParams(
            dimension_semantics=("parallel", "parallel"), vmem_limit_bytes=VMEM_LIMIT_V7X),
        name="k3_pool_out_ffn",
    )(x, x, x, ya, g1, w_in_b, w_pool, pool_scale, w_out, g2, w_gate, w_up, w_down, gf)


def kernel(x_prompt, x_sample, norm1_g, w_in, w_fourier, w_pool, pool_scale, w_out, norm2_g,
           w_gate, w_up, w_down, normf_g):
    assert norm1_g.shape[0] == 1, "single-layer block"
    g1 = norm1_g[0][None]
    g2 = norm2_g[0][None]
    gf = normf_g[None]
    w_in_a = w_in[0, :, :A_WIDTH].astype(BF16)
    w_in_b = w_in[0, :, A_WIDTH:].astype(BF16)
    w_p = w_pool[0].astype(BF16)
    p_s = pool_scale[0][None]
    w_o = w_out[0].astype(BF16)
    w_g = w_gate[0].astype(BF16)
    w_u = w_up[0].astype(BF16)
    w_d = w_down[0].astype(BF16)
    w2 = _prep(w_fourier[0])

    def trunk(x):
        f1, tr, ts, f3 = _dft_tables(x.shape[1], DFT_SPLIT)
        g = _k1(x, g1, w_in_a, w2, f1, tr, ts)
        ya = _k2(g, f3)
        return _k3(x, ya, g1, w_in_b, w_p, p_s, w_o, g2, w_g, w_u, w_d, gf)

    return trunk(x_prompt), trunk(x_sample)
```

```python
import functools

import jax
import jax.numpy as jnp
import numpy as np
from jax.experimental import pallas as pl
from jax.experimental.pallas import tpu as pltpu

F32 = jnp.float32
BF16 = jnp.bfloat16

D_MODEL = 1024
N_HEADS = 4
HEAD_DIM = 128
A_WIDTH = N_HEADS * HEAD_DIM
B_WIDTH = N_HEADS * HEAD_DIM
POOL_WINDOWS = (2, 4, 8, 16)
D_FF = 2816
EPS = 1e-6

HALO = 16
TOKEN_TILE = 512
FF_CHUNK = 256
DFT_SPLIT = 128
S2_BLOCK = 8
K1_BLOCK = 8
VMEM_LIMIT_V7X = 56 * 1024 * 1024


def _const_spec(shape):
    zeros = (0,) * len(shape)
    return pl.BlockSpec(shape, lambda *_: zeros, pipeline_mode=pl.Buffered(1))


def _rmsnorm(x, g):
    r = jax.lax.rsqrt(jnp.mean(x * x, axis=-1, keepdims=True) + EPS)
    return x * r * g


def _dot(a, b):
    return jnp.dot(a, b, preferred_element_type=F32)


def _split_bf16(a):
    hi = a.astype(BF16)
    lo = (a - hi.astype(F32)).astype(BF16)
    return hi, lo


def _dot_f32(a, b):
    ah, al = _split_bf16(a)
    bh, bl = _split_bf16(b)
    return _dot(ah, bh) + (_dot(ah, bl) + _dot(al, bh))


def _prep_kernel(cs_ref, wf_ref, w2_ref):
    for h in range(N_HEADS):
        w = wf_ref[h]
        w2_ref[h, :, :HEAD_DIM] = _dot_f32(cs_ref[0], w).astype(BF16)
        w2_ref[h, :, HEAD_DIM:] = (-_dot_f32(cs_ref[1], w)).astype(BF16)


def _prep(w_fourier):
    n = np.arange(HEAD_DIM)
    ang = 2.0 * np.pi * np.outer(n, n) / HEAD_DIM
    cs = np.stack([np.cos(ang), np.sin(ang)]) / np.sqrt(HEAD_DIM)
    return pl.pallas_call(
        _prep_kernel,
        out_shape=jax.ShapeDtypeStruct((N_HEADS, HEAD_DIM, 2 * HEAD_DIM), BF16),
        name="prep_w2",
    )(jnp.asarray(cs, F32), w_fourier)


def _k1_kernel(x_ref, g1_ref, win_ref, w2_ref, f1_ref, g_ref, ab_ref, *, n1):
    rows = n1 * S2_BLOCK
    x = x_ref[0].reshape(rows, D_MODEL)
    h = _rmsnorm(x, g1_ref[...]).astype(BF16)
    za = _dot(h, win_ref[...]).astype(BF16)
    for hd in range(N_HEADS):
        ab = _dot(za[:, hd * HEAD_DIM:(hd + 1) * HEAD_DIM], w2_ref[hd])
        ab_ref[hd, 0] = ab[:, :HEAD_DIM]
        ab_ref[hd, 1] = ab[:, HEAD_DIM:]
    for hd in range(N_HEADS):
        xx = jnp.concatenate(
            [jnp.concatenate([ab_ref.at[hd, ri][pl.ds(j, n1, stride=S2_BLOCK), :]
                              for j in range(S2_BLOCK)], axis=1) for ri in range(2)],
            axis=0).astype(BF16)
        g = _dot(f1_ref[...], xx)
        out_r = g_ref.at[0, hd, 0].reshape(rows, HEAD_DIM)
        out_i = g_ref.at[0, hd, 1].reshape(rows, HEAD_DIM)
        for j in range(S2_BLOCK):
            out_r[pl.ds(j, n1, stride=S2_BLOCK), :] = g[:n1, j * HEAD_DIM:(j + 1) * HEAD_DIM]
            out_i[pl.ds(j, n1, stride=S2_BLOCK), :] = g[n1:, j * HEAD_DIM:(j + 1) * HEAD_DIM]


def _dft_tables(s, n1):
    n2 = s // n1
    k1 = np.arange(n1)
    a1 = 2.0 * np.pi * np.outer(k1, k1) / n1
    c1, s1 = np.cos(a1), np.sin(a1)
    f1 = np.block([[c1, s1], [-s1, c1]]) / np.sqrt(n1)
    at = 2.0 * np.pi * np.outer(k1, np.arange(n2)) / s
    k2 = np.arange(n2)
    a2 = 2.0 * np.pi * np.outer(k2, k2) / n2
    f3 = np.concatenate([np.cos(a2), np.sin(a2)], axis=1) / np.sqrt(n2)
    tr = jnp.broadcast_to(jnp.asarray(np.cos(at), F32)[:, :, None], (n1, n2, HEAD_DIM))
    ts = jnp.broadcast_to(jnp.asarray(np.sin(at), F32)[:, :, None], (n1, n2, HEAD_DIM))
    return jnp.asarray(f1, F32).astype(BF16), tr, ts, jnp.asarray(f3, F32).astype(BF16)


def _k1(x, g1, w_in_a, w2, f1):
    b, s, _ = x.shape
    n1 = DFT_SPLIT
    n2 = s // n1
    return pl.pallas_call(
        functools.partial(_k1_kernel, n1=n1),
        grid=(b, n2 // S2_BLOCK),
        in_specs=[
            pl.BlockSpec((1, n1, S2_BLOCK, D_MODEL), lambda bi, j: (bi, 0, j, 0)),
            _const_spec((1, D_MODEL)),
            _const_spec((D_MODEL, A_WIDTH)),
            _const_spec((N_HEADS, HEAD_DIM, 2 * HEAD_DIM)),
            _const_spec((2 * n1, 2 * n1)),
        ],
        out_specs=pl.BlockSpec((1, N_HEADS, 2, n1, S2_BLOCK, HEAD_DIM), lambda bi, j: (bi, 0, 0, 0, j, 0)),
        out_shape=jax.ShapeDtypeStruct((b, N_HEADS, 2, n1, n2, HEAD_DIM), F32),
        scratch_shapes=[pltpu.VMEM((N_HEADS, 2, n1 * S2_BLOCK, HEAD_DIM), F32)],
        compiler_params=pltpu.CompilerParams(
            dimension_semantics=("parallel", "parallel"), vmem_limit_bytes=VMEM_LIMIT_V7X),
        name="k1_in_proj_dft1",
    )(x.reshape(b, n1, n2, D_MODEL), g1, w_in_a, w2, f1)


def _k2_kernel(g_ref, tr_ref, ts_ref, f3_ref, y_ref, *, n2):
    for k in range(K1_BLOCK):
        tr = tr_ref[k]
        ts = ts_ref[k]
        pr = []
        pi = []
        for hd in range(N_HEADS):
            gr = g_ref[0, hd, 0, k]
            gi = g_ref[0, hd, 1, k]
            pr.append(gr * tr + gi * ts)
            pi.append(gi * tr - gr * ts)
        x = jnp.concatenate([jnp.concatenate(pr, axis=1), jnp.concatenate(pi, axis=1)],
                            axis=0).astype(BF16)
        r = _dot(f3_ref[...], x)
        for hd in range(N_HEADS):
            out = y_ref.at[0, hd].reshape(n2 * K1_BLOCK, HEAD_DIM)
            out[pl.ds(k, n2, stride=K1_BLOCK), :] = r[:, hd * HEAD_DIM:(hd + 1) * HEAD_DIM]


def _k2(g, tr, ts, f3):
    b, _, _, n1, n2, _ = g.shape
    y = pl.pallas_call(
        functools.partial(_k2_kernel, n2=n2),
        grid=(b, n1 // K1_BLOCK),
        in_specs=[
            pl.BlockSpec((1, N_HEADS, 2, K1_BLOCK, n2, HEAD_DIM), lambda bi, j: (bi, 0, 0, j, 0, 0)),
            pl.BlockSpec((K1_BLOCK, n2, HEAD_DIM), lambda bi, j: (j, 0, 0)),
            pl.BlockSpec((K1_BLOCK, n2, HEAD_DIM), lambda bi, j: (j, 0, 0)),
            _const_spec((n2, 2 * n2)),
        ],
        out_specs=pl.BlockSpec((1, N_HEADS, n2, K1_BLOCK, HEAD_DIM), lambda bi, j: (bi, 0, 0, j, 0)),
        out_shape=jax.ShapeDtypeStruct((b, N_HEADS, n2, n1, HEAD_DIM), F32),
        compiler_params=pltpu.CompilerParams(
            dimension_semantics=("parallel", "parallel"), vmem_limit_bytes=VMEM_LIMIT_V7X),
        name="k2_dft2",
    )(g, tr, ts, f3)
    return y.reshape(b, N_HEADS, n1 * n2, HEAD_DIM)


def _k3_kernel(x_ref, xp_ref, xn_ref, ya_ref, g1_ref, winb_ref, wp_ref, ps_ref, wout_ref, g2_ref,
               wg_ref, wu_ref, wd_ref, gf_ref, o_ref, hs_ref, zs_ref, act_ref, *, seq_len):
    tm = x_ref.shape[1]
    i = pl.program_id(1)
    last = pl.num_programs(1) - 1
    x = x_ref[0]

    ya = jnp.concatenate([ya_ref[0, hd].astype(BF16) for hd in range(N_HEADS)], axis=-1)
    x1 = x + _dot(ya, wout_ref[:A_WIDTH, :])

    g1 = g1_ref[...]
    hs_ref[:HALO] = _rmsnorm(xp_ref[0], g1).astype(BF16)
    hs_ref[HALO:HALO + tm] = _rmsnorm(x, g1).astype(BF16)
    hs_ref[HALO + tm:] = _rmsnorm(xn_ref[0], g1).astype(BF16)
    zb = _dot(hs_ref[...], winb_ref[...])
    row = jax.lax.broadcasted_iota(jnp.int32, (tm + 2 * HALO, 1), 0)
    inside = ((row >= HALO) | (i > 0)) & ((row < HALO + tm) | (i < last))
    zb = jnp.where(inside, zb, 0.0)
    for gi in range(N_HEADS):
        zs_ref[gi] = zb[:, gi * HEAD_DIM:(gi + 1) * HEAD_DIM]

    t = i * tm + jax.lax.broadcasted_iota(jnp.int32, (tm, 1), 0)
    yb = []
    for gi, k in enumerate(POOL_WINDOWS):
        lanes = slice(gi * HEAD_DIM, (gi + 1) * HEAD_DIM)
        sums = zs_ref[gi, pl.ds(HALO - k // 2, tm), :]
        for j in range(1, k):
            sums = sums + zs_ref[gi, pl.ds(HALO - k // 2 + j, tm), :]
        cnt = jnp.minimum(t + k // 2, seq_len) - jnp.maximum(t - k // 2, 0)
        d = sums / cnt.astype(F32) - zs_ref[gi, pl.ds(HALO, tm), :]
        yb.append((_dot(d.astype(BF16), wp_ref[gi]) * ps_ref[:, lanes]).astype(BF16))

    x1 = x1 + _dot(jnp.concatenate(yb, axis=-1), wout_ref[A_WIDTH:, :])
    h2 = _rmsnorm(x1, g2_ref[...]).astype(BF16)
    for c in range(D_FF // FF_CHUNK):
        cols = slice(c * FF_CHUNK, (c + 1) * FF_CHUNK)
        gate = _dot(h2, wg_ref[:, cols])
        up = _dot(h2, wu_ref[:, cols])
        act_ref[:, cols] = (gate * jax.nn.sigmoid(gate) * up).astype(BF16)
    x2 = x1 + _dot(act_ref[...], wd_ref[...])
    o_ref[0] = _rmsnorm(x2, gf_ref[...])


def _k3(x, ya, g1, w_in_b, w_pool, pool_scale, w_out, g2, w_gate, w_up, w_down, gf):
    b, s, _ = x.shape
    tm = TOKEN_TILE
    hb = tm // HALO
    n_hb = s // HALO
    return pl.pallas_call(
        functools.partial(_k3_kernel, seq_len=s),
        grid=(b, s // tm),
        in_specs=[
            pl.BlockSpec((1, tm, D_MODEL), lambda bi, i: (bi, i, 0)),
            pl.BlockSpec((1, HALO, D_MODEL), lambda bi, i: (bi, jnp.maximum(i * hb - 1, 0), 0)),
            pl.BlockSpec((1, HALO, D_MODEL), lambda bi, i: (bi, jnp.minimum((i + 1) * hb, n_hb - 1), 0)),
            pl.BlockSpec((1, N_HEADS, tm, HEAD_DIM), lambda bi, i: (bi, 0, i, 0)),
            _const_spec((1, D_MODEL)),
            _const_spec((D_MODEL, B_WIDTH)),
            _const_spec((N_HEADS, HEAD_DIM, HEAD_DIM)),
            _const_spec((1, B_WIDTH)),
            _const_spec((D_MODEL, D_MODEL)),
            _const_spec((1, D_MODEL)),
            _const_spec((D_MODEL, D_FF)),
            _const_spec((D_MODEL, D_FF)),
            _const_spec((D_FF, D_MODEL)),
            _const_spec((1, D_MODEL)),
        ],
        out_specs=pl.BlockSpec((1, tm, D_MODEL), lambda bi, i: (bi, i, 0)),
        out_shape=jax.ShapeDtypeStruct((b, s, D_MODEL), F32),
        scratch_shapes=[
            pltpu.VMEM((tm + 2 * HALO, D_MODEL), BF16),
            pltpu.VMEM((N_HEADS, tm + 2 * HALO, HEAD_DIM), F32),
            pltpu.VMEM((tm, D_FF), BF16),
        ],
        compiler_params=pltpu.CompilerParams(
            dimension_semantics=("parallel", "parallel"), vmem_limit_bytes=VMEM_LIMIT_V7X),
        name="k3_pool_out_ffn",
    )(x, x, x, ya, g1, w_in_b, w_pool, pool_scale, w_out, g2, w_gate, w_up, w_down, gf)


def kernel(x_prompt, x_sample, norm1_g, w_in, w_fourier, w_pool, pool_scale, w_out, norm2_g,
           w_gate, w_up, w_down, normf_g):
    assert norm1_g.shape[0] == 1, "single-layer block"
    g1 = norm1_g[0][None]
    g2 = norm2_g[0][None]
    gf = normf_g[None]
    w_in_a = w_in[0, :, :A_WIDTH].astype(BF16)
    w_in_b = w_in[0, :, A_WIDTH:].astype(BF16)
    w_p = w_pool[0].astype(BF16)
    p_s = pool_scale[0][None]
    w_o = w_out[0].astype(BF16)
    w_g = w_gate[0].astype(BF16)
    w_u = w_up[0].astype(BF16)
    w_d = w_down[0].astype(BF16)
    w2 = _prep(w_fourier[0])

    def trunk(x):
        f1, tr, ts, f3 = _dft_tables(x.shape[1], DFT_SPLIT)
        g = _k1(x, g1, w_in_a, w2, f1)
        ya = _k2(g, tr, ts, f3)
        return _k3(x, ya, g1, w_in_b, w_p, p_s, w_o, g2, w_g, w_u, w_d, gf)

    return trunk(x_prompt), trunk(x_sample)
```

```python
import functools

import jax
import jax.numpy as jnp
import numpy as np
from jax.experimental import pallas as pl
from jax.experimental.pallas import tpu as pltpu

F32 = jnp.float32
BF16 = jnp.bfloat16

D_MODEL = 1024
N_HEADS = 4
HEAD_DIM = 128
A_WIDTH = N_HEADS * HEAD_DIM
B_WIDTH = N_HEADS * HEAD_DIM
POOL_WINDOWS = (2, 4, 8, 16)
D_FF = 2816
EPS = 1e-6

HALO = 16
TOKEN_TILE = 512
FF_CHUNK = 256
DFT_SPLIT = 128
S2_BLOCK = 8
K1_BLOCK = 8
VMEM_LIMIT_V7X = 56 * 1024 * 1024


def _const_spec(shape):
    zeros = (0,) * len(shape)
    return pl.BlockSpec(shape, lambda *_: zeros, pipeline_mode=pl.Buffered(1))


def _rmsnorm(x, g):
    r = jax.lax.rsqrt(jnp.mean(x * x, axis=-1, keepdims=True) + EPS)
    return x * r * g


def _dot(a, b):
    return jnp.dot(a, b, preferred_element_type=F32)


def _split_bf16(a):
    hi = a.astype(BF16)
    lo = (a - hi.astype(F32)).astype(BF16)
    return hi, lo


def _dot_f32(a, b):
    ah, al = _split_bf16(a)
    bh, bl = _split_bf16(b)
    return _dot(ah, bh) + (_dot(ah, bl) + _dot(al, bh))


def _prep_kernel(cs_ref, wf_ref, w2_ref):
    for h in range(N_HEADS):
        w = wf_ref[h]
        w2_ref[h, :, :HEAD_DIM] = _dot_f32(cs_ref[0], w).astype(BF16)
        w2_ref[h, :, HEAD_DIM:] = (-_dot_f32(cs_ref[1], w)).astype(BF16)


def _prep(w_fourier):
    n = np.arange(HEAD_DIM)
    ang = 2.0 * np.pi * np.outer(n, n) / HEAD_DIM
    cs = np.stack([np.cos(ang), np.sin(ang)]) / np.sqrt(HEAD_DIM)
    return pl.pallas_call(
        _prep_kernel,
        out_shape=jax.ShapeDtypeStruct((N_HEADS, HEAD_DIM, 2 * HEAD_DIM), BF16),
        name="prep_w2",
    )(jnp.asarray(cs, F32), w_fourier)


def _k1_kernel(x_ref, g1_ref, win_ref, w2_ref, f1_ref, g_ref, ab_ref, *, n1):
    rows = n1 * S2_BLOCK
    x = x_ref[0].reshape(rows, D_MODEL)
    h = _rmsnorm(x, g1_ref[...]).astype(BF16)
    za = _dot(h, win_ref[...]).astype(BF16)
    for hd in range(N_HEADS):
        ab = _dot(za[:, hd * HEAD_DIM:(hd + 1) * HEAD_DIM], w2_ref[hd])
        ab_ref[hd, 0] = ab[:, :HEAD_DIM]
        ab_ref[hd, 1] = ab[:, HEAD_DIM:]
    for hd in range(N_HEADS):
        xx = jnp.concatenate(
            [jnp.concatenate([ab_ref.at[hd, ri][pl.ds(j, n1, stride=S2_BLOCK), :]
                              for j in range(S2_BLOCK)], axis=1) for ri in range(2)],
            axis=0).astype(BF16)
        g = _dot(f1_ref[...], xx)
        out_r = g_ref.at[0, hd, 0, 0].reshape(rows, HEAD_DIM)
        out_i = g_ref.at[0, hd, 1, 0].reshape(rows, HEAD_DIM)
        for j in range(S2_BLOCK):
            out_r[pl.ds(j, n1, stride=S2_BLOCK), :] = g[:n1, j * HEAD_DIM:(j + 1) * HEAD_DIM]
            out_i[pl.ds(j, n1, stride=S2_BLOCK), :] = g[n1:, j * HEAD_DIM:(j + 1) * HEAD_DIM]


def _dft_tables(s, n1):
    n2 = s // n1
    k1 = np.arange(n1)
    a1 = 2.0 * np.pi * np.outer(k1, k1) / n1
    c1, s1 = np.cos(a1), np.sin(a1)
    f1 = np.block([[c1, s1], [-s1, c1]]) / np.sqrt(n1)
    at = 2.0 * np.pi * np.outer(k1, np.arange(n2)) / s
    k2 = np.arange(n2)
    a2 = 2.0 * np.pi * np.outer(k2, k2) / n2
    f3 = np.concatenate([np.cos(a2), np.sin(a2)], axis=1) / np.sqrt(n2)
    lanes = np.ones((1, 1, HEAD_DIM))
    tr = jnp.asarray(np.cos(at)[:, :, None] * lanes, F32)
    ts = jnp.asarray(np.sin(at)[:, :, None] * lanes, F32)
    return jnp.asarray(f1, F32).astype(BF16), tr, ts, jnp.asarray(f3, F32).astype(BF16)


def _k1(x, g1, w_in_a, w2, f1):
    b, s, _ = x.shape
    n1 = DFT_SPLIT
    n2 = s // n1
    return pl.pallas_call(
        functools.partial(_k1_kernel, n1=n1),
        grid=(b, n2 // S2_BLOCK),
        in_specs=[
            pl.BlockSpec((1, n1, S2_BLOCK, D_MODEL), lambda bi, j: (bi, 0, j, 0)),
            _const_spec((1, D_MODEL)),
            _const_spec((D_MODEL, A_WIDTH)),
            _const_spec((N_HEADS, HEAD_DIM, 2 * HEAD_DIM)),
            _const_spec((2 * n1, 2 * n1)),
        ],
        out_specs=pl.BlockSpec((1, N_HEADS, 2, 1, n1, S2_BLOCK, HEAD_DIM),
                               lambda bi, j: (bi, 0, 0, j, 0, 0, 0)),
        out_shape=jax.ShapeDtypeStruct((b, N_HEADS, 2, n2 // S2_BLOCK, n1, S2_BLOCK, HEAD_DIM), F32),
        scratch_shapes=[pltpu.VMEM((N_HEADS, 2, n1 * S2_BLOCK, HEAD_DIM), F32)],
        compiler_params=pltpu.CompilerParams(
            dimension_semantics=("parallel", "parallel"), vmem_limit_bytes=VMEM_LIMIT_V7X),
        name="k1_in_proj_dft1",
    )(x.reshape(b, n1, n2, D_MODEL), g1, w_in_a, w2, f1)


def _k2_kernel(g_ref, tr_ref, ts_ref, f3_ref, y_ref, *, n2):
    for k in range(K1_BLOCK):
        tr = tr_ref[k]
        ts = ts_ref[k]
        pr = []
        pi = []
        for hd in range(N_HEADS):
            gr = g_ref[0, hd, 0, :, k].reshape(n2, HEAD_DIM)
            gi = g_ref[0, hd, 1, :, k].reshape(n2, HEAD_DIM)
            pr.append(gr * tr + gi * ts)
            pi.append(gi * tr - gr * ts)
        x = jnp.concatenate([jnp.concatenate(pr, axis=1), jnp.concatenate(pi, axis=1)],
                            axis=0).astype(BF16)
        r = _dot(f3_ref[...], x)
        for hd in range(N_HEADS):
            out = y_ref.at[0, hd, 0].reshape(n2 * K1_BLOCK, HEAD_DIM)
            out[pl.ds(k, n2, stride=K1_BLOCK), :] = r[:, hd * HEAD_DIM:(hd + 1) * HEAD_DIM]


def _k2(g, tr, ts, f3):
    b, _, _, nj, n1, _, _ = g.shape
    n2 = nj * S2_BLOCK
    return pl.pallas_call(
        functools.partial(_k2_kernel, n2=n2),
        grid=(b, n1 // K1_BLOCK),
        in_specs=[
            pl.BlockSpec((1, N_HEADS, 2, nj, K1_BLOCK, S2_BLOCK, HEAD_DIM),
                         lambda bi, j: (bi, 0, 0, 0, j, 0, 0)),
            pl.BlockSpec((K1_BLOCK, n2, HEAD_DIM), lambda bi, j: (j, 0, 0)),
            pl.BlockSpec((K1_BLOCK, n2, HEAD_DIM), lambda bi, j: (j, 0, 0)),
            _const_spec((n2, 2 * n2)),
        ],
        out_specs=pl.BlockSpec((1, N_HEADS, 1, n2, K1_BLOCK, HEAD_DIM), lambda bi, j: (bi, 0, j, 0, 0, 0)),
        out_shape=jax.ShapeDtypeStruct((b, N_HEADS, n1 // K1_BLOCK, n2, K1_BLOCK, HEAD_DIM), F32),
        compiler_params=pltpu.CompilerParams(
            dimension_semantics=("parallel", "parallel"), vmem_limit_bytes=VMEM_LIMIT_V7X),
        name="k2_dft2",
    )(g, tr, ts, f3)


def _k3_kernel(x_ref, xp_ref, xn_ref, ya_ref, g1_ref, winb_ref, wp_ref, ps_ref, wout_ref, g2_ref,
               wg_ref, wu_ref, wd_ref, gf_ref, o_ref, hs_ref, zs_ref, act_ref, *, seq_len):
    tm = x_ref.shape[1]
    i = pl.program_id(1)
    last = pl.num_programs(1) - 1
    x = x_ref[0]

    ya = jnp.concatenate(
        [jnp.swapaxes(ya_ref[0, hd], 0, 1).reshape(tm, HEAD_DIM).astype(BF16) for hd in range(N_HEADS)],
        axis=-1)
    x1 = x + _dot(ya, wout_ref[:A_WIDTH, :])

    g1 = g1_ref[...]
    hs_ref[:HALO] = _rmsnorm(xp_ref[0], g1).astype(BF16)
    hs_ref[HALO:HALO + tm] = _rmsnorm(x, g1).astype(BF16)
    hs_ref[HALO + tm:] = _rmsnorm(xn_ref[0], g1).astype(BF16)
    zb = _dot(hs_ref[...], winb_ref[...])
    row = jax.lax.broadcasted_iota(jnp.int32, (tm + 2 * HALO, 1), 0)
    inside = ((row >= HALO) | (i > 0)) & ((row < HALO + tm) | (i < last))
    zb = jnp.where(inside, zb, 0.0)
    for gi in range(N_HEADS):
        zs_ref[gi] = zb[:, gi * HEAD_DIM:(gi + 1) * HEAD_DIM]

    t = i * tm + jax.lax.broadcasted_iota(jnp.int32, (tm, 1), 0)
    yb = []
    for gi, k in enumerate(POOL_WINDOWS):
        lanes = slice(gi * HEAD_DIM, (gi + 1) * HEAD_DIM)
        sums = zs_ref[gi, pl.ds(HALO - k // 2, tm), :]
        for j in range(1, k):
            sums = sums + zs_ref[gi, pl.ds(HALO - k // 2 + j, tm), :]
        cnt = jnp.minimum(t + k // 2, seq_len) - jnp.maximum(t - k // 2, 0)
        d = sums / cnt.astype(F32) - zs_ref[gi, pl.ds(HALO, tm), :]
        yb.append((_dot(d.astype(BF16), wp_ref[gi]) * ps_ref[:, lanes]).astype(BF16))

    x1 = x1 + _dot(jnp.concatenate(yb, axis=-1), wout_ref[A_WIDTH:, :])
    h2 = _rmsnorm(x1, g2_ref[...]).astype(BF16)
    for c in range(D_FF // FF_CHUNK):
        cols = slice(c * FF_CHUNK, (c + 1) * FF_CHUNK)
        gate = _dot(h2, wg_ref[:, cols])
        up = _dot(h2, wu_ref[:, cols])
        act_ref[:, cols] = (gate * jax.nn.sigmoid(gate) * up).astype(BF16)
    x2 = x1 + _dot(act_ref[...], wd_ref[...])
    o_ref[0] = _rmsnorm(x2, gf_ref[...])


def _k3(x, ya, g1, w_in_b, w_pool, pool_scale, w_out, g2, w_gate, w_up, w_down, gf):
    b, s, _ = x.shape
    tm = TOKEN_TILE
    hb = tm // HALO
    n_hb = s // HALO
    return pl.pallas_call(
        functools.partial(_k3_kernel, seq_len=s),
        grid=(b, s // tm),
        in_specs=[
            pl.BlockSpec((1, tm, D_MODEL), lambda bi, i: (bi, i, 0)),
            pl.BlockSpec((1, HALO, D_MODEL), lambda bi, i: (bi, jnp.maximum(i * hb - 1, 0), 0)),
            pl.BlockSpec((1, HALO, D_MODEL), lambda bi, i: (bi, jnp.minimum((i + 1) * hb, n_hb - 1), 0)),
            pl.BlockSpec((1, N_HEADS, DFT_SPLIT // K1_BLOCK, tm // DFT_SPLIT, K1_BLOCK, HEAD_DIM),
                         lambda bi, i: (bi, 0, 0, i, 0, 0)),
            _const_spec((1, D_MODEL)),
            _const_spec((D_MODEL, B_WIDTH)),
            _const_spec((N_HEADS, HEAD_DIM, HEAD_DIM)),
            _const_spec((1, B_WIDTH)),
            _const_spec((D_MODEL, D_MODEL)),
            _const_spec((1, D_MODEL)),
            _const_spec((D_MODEL, D_FF)),
            _const_spec((D_MODEL, D_FF)),
            _const_spec((D_FF, D_MODEL)),
            _const_spec((1, D_MODEL)),
        ],
        out_specs=pl.BlockSpec((1, tm, D_MODEL), lambda bi, i: (bi, i, 0)),
        out_shape=jax.ShapeDtypeStruct((b, s, D_MODEL), F32),
        scratch_shapes=[
            pltpu.VMEM((tm + 2 * HALO, D_MODEL), BF16),
            pltpu.VMEM((N_HEADS, tm + 2 * HALO, HEAD_DIM), F32),
            pltpu.VMEM((tm, D_FF), BF16),
        ],
        compiler_params=pltpu.CompilerParams(
            dimension_semantics=("parallel", "parallel"), vmem_limit_bytes=VMEM_LIMIT_V7X),
        name="k3_pool_out_ffn",
    )(x, x, x, ya, g1, w_in_b, w_pool, pool_scale, w_out, g2, w_gate, w_up, w_down, gf)


def kernel(x_prompt, x_sample, norm1_g, w_in, w_fourier, w_pool, pool_scale, w_out, norm2_g,
           w_gate, w_up, w_down, normf_g):
    assert norm1_g.shape[0] == 1, "single-layer block"
    g1 = norm1_g[0][None]
    g2 = norm2_g[0][None]
    gf = normf_g[None]
    w_in_a = w_in[0, :, :A_WIDTH].astype(BF16)
    w_in_b = w_in[0, :, A_WIDTH:].astype(BF16)
    w_p = w_pool[0].astype(BF16)
    p_s = pool_scale[0][None]
    w_o = w_out[0].astype(BF16)
    w_g = w_gate[0].astype(BF16)
    w_u = w_up[0].astype(BF16)
    w_d = w_down[0].astype(BF16)
    w2 = _prep(w_fourier[0])

    def trunk(x):
        f1, tr, ts, f3 = _dft_tables(x.shape[1], DFT_SPLIT)
        g = _k1(x, g1, w_in_a, w2, f1)
        ya = _k2(g, tr, ts, f3)
        return _k3(x, ya, g1, w_in_b, w_p, p_s, w_o, g2, w_g, w_u, w_d, gf)

    return trunk(x_prompt), trunk(x_sample)
```

```python
import functools

import jax
import jax.numpy as jnp
import numpy as np
from jax.experimental import pallas as pl
from jax.experimental.pallas import tpu as pltpu

F32 = jnp.float32
BF16 = jnp.bfloat16

D_MODEL = 1024
N_HEADS = 4
HEAD_DIM = 128
A_WIDTH = N_HEADS * HEAD_DIM
B_WIDTH = N_HEADS * HEAD_DIM
POOL_WINDOWS = (2, 4, 8, 16)
D_FF = 2816
EPS = 1e-6

HALO = 16
TOKEN_TILE = 512
FF_CHUNK = 256
DFT_SPLIT = 128
S2_BLOCK = 8
K1_BLOCK = 8
VMEM_LIMIT_V7X = 56 * 1024 * 1024


def _const_spec(shape):
    zeros = (0,) * len(shape)
    return pl.BlockSpec(shape, lambda *_: zeros, pipeline_mode=pl.Buffered(1))


def _rmsnorm(x, g):
    r = jax.lax.rsqrt(jnp.mean(x * x, axis=-1, keepdims=True) + EPS)
    return x * r * g


def _dot(a, b):
    return jnp.dot(a, b, preferred_element_type=F32)


def _split_bf16(a):
    hi = a.astype(BF16)
    lo = (a - hi.astype(F32)).astype(BF16)
    return hi, lo


def _dot_f32(a, b):
    ah, al = _split_bf16(a)
    bh, bl = _split_bf16(b)
    return _dot(ah, bh) + (_dot(ah, bl) + _dot(al, bh))


def _prep_kernel(cs_ref, wf_ref, wp_ref, ps_ref, wout_ref, w2_ref, wpo_ref):
    for h in range(N_HEADS):
        w = wf_ref[h]
        w2_ref[h, :, :HEAD_DIM] = _dot_f32(cs_ref[0], w).astype(BF16)
        w2_ref[h, :, HEAD_DIM:] = (-_dot_f32(cs_ref[1], w)).astype(BF16)
    for g in range(N_HEADS):
        lanes = slice(g * HEAD_DIM, (g + 1) * HEAD_DIM)
        rows = slice(A_WIDTH + g * HEAD_DIM, A_WIDTH + (g + 1) * HEAD_DIM)
        wpo_ref[lanes, :] = _dot_f32(wp_ref[g] * ps_ref[:, lanes], wout_ref[rows, :]).astype(BF16)


def _prep(w_fourier, w_pool, pool_scale, w_out):
    n = np.arange(HEAD_DIM)
    ang = 2.0 * np.pi * np.outer(n, n) / HEAD_DIM
    cs = np.stack([np.cos(ang), np.sin(ang)]) / np.sqrt(HEAD_DIM)
    return pl.pallas_call(
        _prep_kernel,
        out_shape=(jax.ShapeDtypeStruct((N_HEADS, HEAD_DIM, 2 * HEAD_DIM), BF16),
                   jax.ShapeDtypeStruct((B_WIDTH, D_MODEL), BF16)),
        name="prep_weights",
    )(jnp.asarray(cs, F32), w_fourier, w_pool, pool_scale, w_out)


def _k1_kernel(x_ref, g1_ref, win_ref, w2_ref, f1_ref, g_ref, ab_ref, *, n1):
    rows = n1 * S2_BLOCK
    x = x_ref[0].reshape(rows, D_MODEL)
    h = _rmsnorm(x, g1_ref[...]).astype(BF16)
    za = _dot(h, win_ref[...]).astype(BF16)
    for hd in range(N_HEADS):
        ab = _dot(za[:, hd * HEAD_DIM:(hd + 1) * HEAD_DIM], w2_ref[hd])
        ab_ref[hd, 0] = ab[:, :HEAD_DIM]
        ab_ref[hd, 1] = ab[:, HEAD_DIM:]
    for hd in range(N_HEADS):
        xx = jnp.concatenate(
            [jnp.concatenate([ab_ref.at[hd, ri][pl.ds(j, n1, stride=S2_BLOCK), :]
                              for j in range(S2_BLOCK)], axis=1) for ri in range(2)],
            axis=0).astype(BF16)
        g = _dot(f1_ref[...], xx)
        out_r = g_ref.at[0, hd, 0, 0].reshape(rows, HEAD_DIM)
        out_i = g_ref.at[0, hd, 1, 0].reshape(rows, HEAD_DIM)
        for j in range(S2_BLOCK):
            out_r[pl.ds(j, n1, stride=S2_BLOCK), :] = g[:n1, j * HEAD_DIM:(j + 1) * HEAD_DIM]
            out_i[pl.ds(j, n1, stride=S2_BLOCK), :] = g[n1:, j * HEAD_DIM:(j + 1) * HEAD_DIM]


def _dft_tables(s, n1):
    n2 = s // n1
    k1 = np.arange(n1)
    a1 = 2.0 * np.pi * np.outer(k1, k1) / n1
    c1, s1 = np.cos(a1), np.sin(a1)
    f1 = np.block([[c1, s1], [-s1, c1]]) / np.sqrt(n1)
    at = 2.0 * np.pi * np.outer(k1, np.arange(n2)) / s
    k2 = np.arange(n2)
    a2 = 2.0 * np.pi * np.outer(k2, k2) / n2
    f3 = np.concatenate([np.cos(a2), np.sin(a2)], axis=1) / np.sqrt(n2)
    lanes = np.ones((1, 1, HEAD_DIM))
    tr = jnp.asarray(np.cos(at)[:, :, None] * lanes, F32)
    ts = jnp.asarray(np.sin(at)[:, :, None] * lanes, F32)
    return jnp.asarray(f1, F32).astype(BF16), tr, ts, jnp.asarray(f3, F32).astype(BF16)


def _k1(x, g1, w_in_a, w2, f1):
    b, s, _ = x.shape
    n1 = DFT_SPLIT
    n2 = s // n1
    return pl.pallas_call(
        functools.partial(_k1_kernel, n1=n1),
        grid=(b, n2 // S2_BLOCK),
        in_specs=[
            pl.BlockSpec((1, n1, S2_BLOCK, D_MODEL), lambda bi, j: (bi, 0, j, 0)),
            _const_spec((1, D_MODEL)),
            _const_spec((D_MODEL, A_WIDTH)),
            _const_spec((N_HEADS, HEAD_DIM, 2 * HEAD_DIM)),
            _const_spec((2 * n1, 2 * n1)),
        ],
        out_specs=pl.BlockSpec((1, N_HEADS, 2, 1, n1, S2_BLOCK, HEAD_DIM),
                               lambda bi, j: (bi, 0, 0, j, 0, 0, 0)),
        out_shape=jax.ShapeDtypeStruct((b, N_HEADS, 2, n2 // S2_BLOCK, n1, S2_BLOCK, HEAD_DIM), F32),
        scratch_shapes=[pltpu.VMEM((N_HEADS, 2, n1 * S2_BLOCK, HEAD_DIM), F32)],
        compiler_params=pltpu.CompilerParams(
            dimension_semantics=("parallel", "parallel"), vmem_limit_bytes=VMEM_LIMIT_V7X),
        name="k1_in_proj_dft1",
    )(x.reshape(b, n1, n2, D_MODEL), g1, w_in_a, w2, f1)


def _k2_kernel(g_ref, tr_ref, ts_ref, f3_ref, y_ref, *, n2):
    for k in range(K1_BLOCK):
        tr = tr_ref[k]
        ts = ts_ref[k]
        pr = []
        pi = []
        for hd in range(N_HEADS):
            gr = g_ref[0, hd, 0, :, k].reshape(n2, HEAD_DIM)
            gi = g_ref[0, hd, 1, :, k].reshape(n2, HEAD_DIM)
            pr.append(gr * tr + gi * ts)
            pi.append(gi * tr - gr * ts)
        x = jnp.concatenate([jnp.concatenate(pr, axis=1), jnp.concatenate(pi, axis=1)],
                            axis=0).astype(BF16)
        r = _dot(f3_ref[...], x)
        for hd in range(N_HEADS):
            out = y_ref.at[0, hd, 0].reshape(n2 * K1_BLOCK, HEAD_DIM)
            out[pl.ds(k, n2, stride=K1_BLOCK), :] = r[:, hd * HEAD_DIM:(hd + 1) * HEAD_DIM]


def _k2(g, tr, ts, f3):
    b, _, _, nj, n1, _, _ = g.shape
    n2 = nj * S2_BLOCK
    return pl.pallas_call(
        functools.partial(_k2_kernel, n2=n2),
        grid=(b, n1 // K1_BLOCK),
        in_specs=[
            pl.BlockSpec((1, N_HEADS, 2, nj, K1_BLOCK, S2_BLOCK, HEAD_DIM),
                         lambda bi, j: (bi, 0, 0, 0, j, 0, 0)),
            pl.BlockSpec((K1_BLOCK, n2, HEAD_DIM), lambda bi, j: (j, 0, 0)),
            pl.BlockSpec((K1_BLOCK, n2, HEAD_DIM), lambda bi, j: (j, 0, 0)),
            _const_spec((n2, 2 * n2)),
        ],
        out_specs=pl.BlockSpec((1, N_HEADS, 1, n2, K1_BLOCK, HEAD_DIM), lambda bi, j: (bi, 0, j, 0, 0, 0)),
        out_shape=jax.ShapeDtypeStruct((b, N_HEADS, n1 // K1_BLOCK, n2, K1_BLOCK, HEAD_DIM), F32),
        compiler_params=pltpu.CompilerParams(
            dimension_semantics=("parallel", "parallel"), vmem_limit_bytes=VMEM_LIMIT_V7X),
        name="k2_dft2",
    )(g, tr, ts, f3)


def _k3_kernel(x_ref, xp_ref, xn_ref, ya_ref, g1_ref, winb_ref, wouta_ref, wpo_ref, g2_ref,
               wg_ref, wu_ref, wd_ref, gf_ref, o_ref, hs_ref, zs_ref, act_ref, *, seq_len):
    tm = x_ref.shape[1]
    i = pl.program_id(1)
    last = pl.num_programs(1) - 1
    x = x_ref[0]

    ya = jnp.concatenate(
        [jnp.swapaxes(ya_ref[0, hd], 0, 1).reshape(tm, HEAD_DIM).astype(BF16) for hd in range(N_HEADS)],
        axis=-1)
    x1 = x + _dot(ya, wouta_ref[...])

    g1 = g1_ref[...]
    hs_ref[:HALO] = _rmsnorm(xp_ref[0], g1).astype(BF16)
    hs_ref[HALO:HALO + tm] = _rmsnorm(x, g1).astype(BF16)
    hs_ref[HALO + tm:] = _rmsnorm(xn_ref[0], g1).astype(BF16)
    zb = _dot(hs_ref[...], winb_ref[...])
    row = jax.lax.broadcasted_iota(jnp.int32, (tm + 2 * HALO, 1), 0)
    inside = ((row >= HALO) | (i > 0)) & ((row < HALO + tm) | (i < last))
    zb = jnp.where(inside, zb, 0.0)
    for gi in range(N_HEADS):
        zs_ref[gi] = zb[:, gi * HEAD_DIM:(gi + 1) * HEAD_DIM]

    t = i * tm + jax.lax.broadcasted_iota(jnp.int32, (tm, 1), 0)
    d = []
    for gi, k in enumerate(POOL_WINDOWS):
        sums = zs_ref[gi, pl.ds(HALO - k // 2, tm), :]
        for j in range(1, k):
            sums = sums + zs_ref[gi, pl.ds(HALO - k // 2 + j, tm), :]
        cnt = jnp.minimum(t + k // 2, seq_len) - jnp.maximum(t - k // 2, 0)
        d.append((sums / cnt.astype(F32) - zs_ref[gi, pl.ds(HALO, tm), :]).astype(BF16))

    x1 = x1 + _dot(jnp.concatenate(d, axis=-1), wpo_ref[...])
    h2 = _rmsnorm(x1, g2_ref[...]).astype(BF16)
    for c in range(D_FF // FF_CHUNK):
        cols = slice(c * FF_CHUNK, (c + 1) * FF_CHUNK)
        gate = _dot(h2, wg_ref[:, cols])
        up = _dot(h2, wu_ref[:, cols])
        act_ref[:, cols] = (gate * jax.nn.sigmoid(gate) * up).astype(BF16)
    x2 = x1 + _dot(act_ref[...], wd_ref[...])
    o_ref[0] = _rmsnorm(x2, gf_ref[...])


def _k3(x, ya, g1, w_in_b, w_out_a, w_po, g2, w_gate, w_up, w_down, gf):
    b, s, _ = x.shape
    tm = TOKEN_TILE
    hb = tm // HALO
    n_hb = s // HALO
    return pl.pallas_call(
        functools.partial(_k3_kernel, seq_len=s),
        grid=(b, s // tm),
        in_specs=[
            pl.BlockSpec((1, tm, D_MODEL), lambda bi, i: (bi, i, 0)),
            pl.BlockSpec((1, HALO, D_MODEL), lambda bi, i: (bi, jnp.maximum(i * hb - 1, 0), 0)),
            pl.BlockSpec((1, HALO, D_MODEL), lambda bi, i: (bi, jnp.minimum((i + 1) * hb, n_hb - 1), 0)),
            pl.BlockSpec((1, N_HEADS, DFT_SPLIT // K1_BLOCK, tm // DFT_SPLIT, K1_BLOCK, HEAD_DIM),
                         lambda bi, i: (bi, 0, 0, i, 0, 0)),
            _const_spec((1, D_MODEL)),
            _const_spec((D_MODEL, B_WIDTH)),
            _const_spec((A_WIDTH, D_MODEL)),
            _const_spec((B_WIDTH, D_MODEL)),
            _const_spec((1, D_MODEL)),
            _const_spec((D_MODEL, D_FF)),
            _const_spec((D_MODEL, D_FF)),
            _const_spec((D_FF, D_MODEL)),
            _const_spec((1, D_MODEL)),
        ],
        out_specs=pl.BlockSpec((1, tm, D_MODEL), lambda bi, i: (bi, i, 0)),
        out_shape=jax.ShapeDtypeStruct((b, s, D_MODEL), F32),
        scratch_shapes=[
            pltpu.VMEM((tm + 2 * HALO, D_MODEL), BF16),
            pltpu.VMEM((N_HEADS, tm + 2 * HALO, HEAD_DIM), F32),
            pltpu.VMEM((tm, D_FF), BF16),
        ],
        compiler_params=pltpu.CompilerParams(
            dimension_semantics=("parallel", "parallel"), vmem_limit_bytes=VMEM_LIMIT_V7X),
        name="k3_pool_out_ffn",
    )(x, x, x, ya, g1, w_in_b, w_out_a, w_po, g2, w_gate, w_up, w_down, gf)


def kernel(x_prompt, x_sample, norm1_g, w_in, w_fourier, w_pool, pool_scale, w_out, norm2_g,
           w_gate, w_up, w_down, normf_g):
    assert norm1_g.shape[0] == 1, "single-layer block"
    g1 = norm1_g[0][None]
    g2 = norm2_g[0][None]
    gf = normf_g[None]
    w_in_a = w_in[0, :, :A_WIDTH].astype(BF16)
    w_in_b = w_in[0, :, A_WIDTH:].astype(BF16)
    w_o_a = w_out[0, :A_WIDTH].astype(BF16)
    w_g = w_gate[0].astype(BF16)
    w_u = w_up[0].astype(BF16)
    w_d = w_down[0].astype(BF16)
    w2, w_po = _prep(w_fourier[0], w_pool[0], pool_scale[0][None], w_out[0])

    def trunk(x):
        f1, tr, ts, f3 = _dft_tables(x.shape[1], DFT_SPLIT)
        g = _k1(x, g1, w_in_a, w2, f1)
        ya = _k2(g, tr, ts, f3)
        return _k3(x, ya, g1, w_in_b, w_o_a, w_po, g2, w_g, w_u, w_d, gf)

    return trunk(x_prompt), trunk(x_sample)
```

```python
import functools

import jax
import jax.numpy as jnp
import numpy as np
from jax.experimental import pallas as pl
from jax.experimental.pallas import tpu as pltpu

F32 = jnp.float32
BF16 = jnp.bfloat16

D_MODEL = 1024
N_HEADS = 4
HEAD_DIM = 128
A_WIDTH = N_HEADS * HEAD_DIM
B_WIDTH = N_HEADS * HEAD_DIM
POOL_WINDOWS = (2, 4, 8, 16)
D_FF = 2816
EPS = 1e-6

HALO = 16
TOKEN_TILE = 512
FF_CHUNK = 256
DFT_SPLIT = 128
S2_BLOCK = 8
K1_BLOCK = 8
VMEM_LIMIT_V7X = 56 * 1024 * 1024


def _const_spec(shape):
    zeros = (0,) * len(shape)
    return pl.BlockSpec(shape, lambda *_: zeros, pipeline_mode=pl.Buffered(1))


def _rmsnorm(x, g):
    r = jax.lax.rsqrt(jnp.mean(x * x, axis=-1, keepdims=True) + EPS)
    return x * r * g


def _dot(a, b):
    return jnp.dot(a, b, preferred_element_type=F32)


def _split_bf16(a):
    hi = a.astype(BF16)
    lo = (a - hi.astype(F32)).astype(BF16)
    return hi, lo


def _dot_f32(a, b):
    ah, al = _split_bf16(a)
    bh, bl = _split_bf16(b)
    return _dot(ah, bh) + (_dot(ah, bl) + _dot(al, bh))


def _prep_kernel(cs_ref, wf_ref, wp_ref, ps_ref, wout_ref, w2_ref, wpo_ref):
    for h in range(N_HEADS):
        w = wf_ref[h]
        w2_ref[h, :, :HEAD_DIM] = _dot_f32(cs_ref[0], w).astype(BF16)
        w2_ref[h, :, HEAD_DIM:] = (-_dot_f32(cs_ref[1], w)).astype(BF16)
    for g in range(N_HEADS):
        lanes = slice(g * HEAD_DIM, (g + 1) * HEAD_DIM)
        rows = slice(A_WIDTH + g * HEAD_DIM, A_WIDTH + (g + 1) * HEAD_DIM)
        wpo_ref[lanes, :] = _dot_f32(wp_ref[g] * ps_ref[:, lanes], wout_ref[rows, :]).astype(BF16)


def _prep(w_fourier, w_pool, pool_scale, w_out):
    n = np.arange(HEAD_DIM)
    ang = 2.0 * np.pi * np.outer(n, n) / HEAD_DIM
    cs = np.stack([np.cos(ang), np.sin(ang)]) / np.sqrt(HEAD_DIM)
    return pl.pallas_call(
        _prep_kernel,
        out_shape=(jax.ShapeDtypeStruct((N_HEADS, HEAD_DIM, 2 * HEAD_DIM), BF16),
                   jax.ShapeDtypeStruct((B_WIDTH, D_MODEL), BF16)),
        name="prep_weights",
    )(jnp.asarray(cs, F32), w_fourier, w_pool, pool_scale, w_out)


def _k1_kernel(x_ref, g1_ref, win_ref, w2_ref, f1_ref, g_ref, ab_ref, *, n1):
    rows = n1 * S2_BLOCK
    x = x_ref[0].reshape(rows, D_MODEL)
    h = _rmsnorm(x, g1_ref[...]).astype(BF16)
    za = _dot(h, win_ref[...]).astype(BF16)
    for hd in range(N_HEADS):
        ab = _dot(za[:, hd * HEAD_DIM:(hd + 1) * HEAD_DIM], w2_ref[hd])
        ab_ref[hd, 0] = ab[:, :HEAD_DIM]
        ab_ref[hd, 1] = ab[:, HEAD_DIM:]
    for hd in range(N_HEADS):
        xx = jnp.concatenate(
            [jnp.concatenate([ab_ref.at[hd, ri][pl.ds(j, n1, stride=S2_BLOCK), :]
                              for j in range(S2_BLOCK)], axis=1) for ri in range(2)],
            axis=0).astype(BF16)
        g = _dot(f1_ref[...], xx)
        out_r = g_ref.at[0, hd, 0, 0].reshape(rows, HEAD_DIM)
        out_i = g_ref.at[0, hd, 1, 0].reshape(rows, HEAD_DIM)
        for j in range(S2_BLOCK):
            out_r[pl.ds(j, n1, stride=S2_BLOCK), :] = g[:n1, j * HEAD_DIM:(j + 1) * HEAD_DIM]
            out_i[pl.ds(j, n1, stride=S2_BLOCK), :] = g[n1:, j * HEAD_DIM:(j + 1) * HEAD_DIM]


def _dft_tables(s, n1):
    n2 = s // n1
    k1 = np.arange(n1)
    a1 = 2.0 * np.pi * np.outer(k1, k1) / n1
    c1, s1 = np.cos(a1), np.sin(a1)
    f1 = np.block([[c1, s1], [-s1, c1]]) / np.sqrt(n1)
    at = 2.0 * np.pi * np.outer(k1, np.arange(n2)) / s
    k2 = np.arange(n2)
    a2 = 2.0 * np.pi * np.outer(k2, k2) / n2
    f3 = np.concatenate([np.cos(a2), np.sin(a2)], axis=1) / np.sqrt(n2)
    lanes = np.ones((1, 1, HEAD_DIM))
    tr = jnp.asarray(np.cos(at)[:, :, None] * lanes, F32)
    ts = jnp.asarray(np.sin(at)[:, :, None] * lanes, F32)
    return jnp.asarray(f1, F32).astype(BF16), tr, ts, jnp.asarray(f3, F32).astype(BF16)


def _k1(x, g1, w_in_a, w2, f1):
    b, s, _ = x.shape
    n1 = DFT_SPLIT
    n2 = s // n1
    return pl.pallas_call(
        functools.partial(_k1_kernel, n1=n1),
        grid=(b, n2 // S2_BLOCK),
        in_specs=[
            pl.BlockSpec((1, n1, S2_BLOCK, D_MODEL), lambda bi, j: (bi, 0, j, 0)),
            _const_spec((1, D_MODEL)),
            _const_spec((D_MODEL, A_WIDTH)),
            _const_spec((N_HEADS, HEAD_DIM, 2 * HEAD_DIM)),
            _const_spec((2 * n1, 2 * n1)),
        ],
        out_specs=pl.BlockSpec((1, N_HEADS, 2, 1, n1, S2_BLOCK, HEAD_DIM),
                               lambda bi, j: (bi, 0, 0, j, 0, 0, 0)),
        out_shape=jax.ShapeDtypeStruct((b, N_HEADS, 2, n2 // S2_BLOCK, n1, S2_BLOCK, HEAD_DIM), F32),
        scratch_shapes=[pltpu.VMEM((N_HEADS, 2, n1 * S2_BLOCK, HEAD_DIM), F32)],
        compiler_params=pltpu.CompilerParams(
            dimension_semantics=("parallel", "parallel"), vmem_limit_bytes=VMEM_LIMIT_V7X),
        name="k1_in_proj_dft1",
    )(x.reshape(b, n1, n2, D_MODEL), g1, w_in_a, w2, f1)


def _k2_kernel(g_ref, tr_ref, ts_ref, f3_ref, y_ref, *, n2):
    for k in range(K1_BLOCK):
        tr = tr_ref[k]
        ts = ts_ref[k]
        pr = []
        pi = []
        for hd in range(N_HEADS):
            gr = g_ref[0, hd, 0, :, k].reshape(n2, HEAD_DIM)
            gi = g_ref[0, hd, 1, :, k].reshape(n2, HEAD_DIM)
            pr.append(gr * tr + gi * ts)
            pi.append(gi * tr - gr * ts)
        x = jnp.concatenate([jnp.concatenate(pr, axis=1), jnp.concatenate(pi, axis=1)],
                            axis=0).astype(BF16)
        r = _dot(f3_ref[...], x)
        for hd in range(N_HEADS):
            out = y_ref.at[0, hd, 0].reshape(n2 * K1_BLOCK, HEAD_DIM)
            out[pl.ds(k, n2, stride=K1_BLOCK), :] = r[:, hd * HEAD_DIM:(hd + 1) * HEAD_DIM]


def _k2(g, tr, ts, f3):
    b, _, _, nj, n1, _, _ = g.shape
    n2 = nj * S2_BLOCK
    return pl.pallas_call(
        functools.partial(_k2_kernel, n2=n2),
        grid=(b, n1 // K1_BLOCK),
        in_specs=[
            pl.BlockSpec((1, N_HEADS, 2, nj, K1_BLOCK, S2_BLOCK, HEAD_DIM),
                         lambda bi, j: (bi, 0, 0, 0, j, 0, 0)),
            pl.BlockSpec((K1_BLOCK, n2, HEAD_DIM), lambda bi, j: (j, 0, 0)),
            pl.BlockSpec((K1_BLOCK, n2, HEAD_DIM), lambda bi, j: (j, 0, 0)),
            _const_spec((n2, 2 * n2)),
        ],
        out_specs=pl.BlockSpec((1, N_HEADS, 1, n2, K1_BLOCK, HEAD_DIM), lambda bi, j: (bi, 0, j, 0, 0, 0)),
        out_shape=jax.ShapeDtypeStruct((b, N_HEADS, n1 // K1_BLOCK, n2, K1_BLOCK, HEAD_DIM), F32),
        compiler_params=pltpu.CompilerParams(
            dimension_semantics=("parallel", "parallel"), vmem_limit_bytes=VMEM_LIMIT_V7X),
        name="k2_dft2",
    )(g, tr, ts, f3)


def _pool_branch(xm, xp, xn, j, last, seq_len, g1_ref, winb_ref, hs_ref, zs_ref):
    tm = xm.shape[0]
    g1 = g1_ref[...]
    hs_ref[:HALO] = _rmsnorm(xp, g1).astype(BF16)
    hs_ref[HALO:HALO + tm] = _rmsnorm(xm, g1).astype(BF16)
    hs_ref[HALO + tm:] = _rmsnorm(xn, g1).astype(BF16)
    zb = _dot(hs_ref[...], winb_ref[...])
    row = jax.lax.broadcasted_iota(jnp.int32, (tm + 2 * HALO, 1), 0)
    inside = ((row >= HALO) | (j > 0)) & ((row < HALO + tm) | (j < last))
    zb = jnp.where(inside, zb, 0.0)
    for gi in range(N_HEADS):
        zs_ref[gi] = zb[:, gi * HEAD_DIM:(gi + 1) * HEAD_DIM]

    t = j * tm + jax.lax.broadcasted_iota(jnp.int32, (tm, 1), 0)
    d = []
    for gi, k in enumerate(POOL_WINDOWS):
        sums = zs_ref[gi, pl.ds(HALO - k // 2, tm), :]
        for s in range(1, k):
            sums = sums + zs_ref[gi, pl.ds(HALO - k // 2 + s, tm), :]
        cnt = jnp.minimum(t + k // 2, seq_len) - jnp.maximum(t - k // 2, 0)
        d.append((sums / cnt.astype(F32) - zs_ref[gi, pl.ds(HALO, tm), :]).astype(BF16))
    return jnp.concatenate(d, axis=-1)


def _k3_kernel(x_ref, xn_ref, xnn_ref, ya_ref, g1_ref, winb_ref, wouta_ref, wpo_ref, g2_ref,
               wg_ref, wu_ref, wd_ref, gf_ref, o_ref, hs_ref, zs_ref, act_ref, d_ref, *, seq_len):
    tm = x_ref.shape[1]
    i = pl.program_id(1)
    last = pl.num_programs(1) - 1
    x = x_ref[0]
    pool = functools.partial(_pool_branch, last=last, seq_len=seq_len, g1_ref=g1_ref,
                             winb_ref=winb_ref, hs_ref=hs_ref, zs_ref=zs_ref)

    slot = i % 2

    @pl.when(i == 0)
    def _():
        d_ref[0] = pool(x, x[:HALO], xn_ref[0, :HALO, :], i)

    d_next = pool(xn_ref[0], x[tm - HALO:], xnn_ref[0], i + 1)
    d_ref[1 - slot] = d_next
    bits = pltpu.bitcast(d_next[:8, :HEAD_DIM].astype(F32), jnp.uint32)
    zero = ((bits >> 16) >> 16).astype(F32)[:1]
    gf = gf_ref[...] + jnp.concatenate([zero] * (D_MODEL // HEAD_DIM), axis=-1)

    ya = jnp.concatenate(
        [jnp.swapaxes(ya_ref[0, hd], 0, 1).reshape(tm, HEAD_DIM).astype(BF16) for hd in range(N_HEADS)],
        axis=-1)
    x1 = x + _dot(ya, wouta_ref[...]) + _dot(d_ref[slot], wpo_ref[...])
    h2 = _rmsnorm(x1, g2_ref[...]).astype(BF16)
    for c in range(D_FF // FF_CHUNK):
        cols = slice(c * FF_CHUNK, (c + 1) * FF_CHUNK)
        gate = _dot(h2, wg_ref[:, cols])
        up = _dot(h2, wu_ref[:, cols])
        act_ref[:, cols] = (gate * jax.nn.sigmoid(gate) * up).astype(BF16)
    x2 = x1 + _dot(act_ref[...], wd_ref[...])
    o_ref[0] = _rmsnorm(x2, gf)


def _k3(x, ya, g1, w_in_b, w_out_a, w_po, g2, w_gate, w_up, w_down, gf):
    b, s, _ = x.shape
    tm = TOKEN_TILE
    n_t = s // tm
    hb = tm // HALO
    n_hb = s // HALO
    return pl.pallas_call(
        functools.partial(_k3_kernel, seq_len=s),
        grid=(b, n_t),
        in_specs=[
            pl.BlockSpec((1, tm, D_MODEL), lambda bi, i: (bi, i, 0)),
            pl.BlockSpec((1, tm, D_MODEL), lambda bi, i: (bi, jnp.minimum(i + 1, n_t - 1), 0)),
            pl.BlockSpec((1, HALO, D_MODEL), lambda bi, i: (bi, jnp.minimum((i + 2) * hb, n_hb - 1), 0)),
            pl.BlockSpec((1, N_HEADS, DFT_SPLIT // K1_BLOCK, tm // DFT_SPLIT, K1_BLOCK, HEAD_DIM),
                         lambda bi, i: (bi, 0, 0, i, 0, 0)),
            _const_spec((1, D_MODEL)),
            _const_spec((D_MODEL, B_WIDTH)),
            _const_spec((A_WIDTH, D_MODEL)),
            _const_spec((B_WIDTH, D_MODEL)),
            _const_spec((1, D_MODEL)),
            _const_spec((D_MODEL, D_FF)),
            _const_spec((D_MODEL, D_FF)),
            _const_spec((D_FF, D_MODEL)),
            _const_spec((1, D_MODEL)),
        ],
        out_specs=pl.BlockSpec((1, tm, D_MODEL), lambda bi, i: (bi, i, 0)),
        out_shape=jax.ShapeDtypeStruct((b, s, D_MODEL), F32),
        scratch_shapes=[
            pltpu.VMEM((tm + 2 * HALO, D_MODEL), BF16),
            pltpu.VMEM((N_HEADS, tm + 2 * HALO, HEAD_DIM), F32),
            pltpu.VMEM((tm, D_FF), BF16),
            pltpu.VMEM((2, tm, B_WIDTH), BF16),
        ],
        compiler_params=pltpu.CompilerParams(
            dimension_semantics=("parallel", "arbitrary"), vmem_limit_bytes=VMEM_LIMIT_V7X),
        name="k3_pool_out_ffn",
    )(x, x, x, ya, g1, w_in_b, w_out_a, w_po, g2, w_gate, w_up, w_down, gf)


def kernel(x_prompt, x_sample, norm1_g, w_in, w_fourier, w_pool, pool_scale, w_out, norm2_g,
           w_gate, w_up, w_down, normf_g):
    assert norm1_g.shape[0] == 1, "single-layer block"
    g1 = norm1_g[0][None]
    g2 = norm2_g[0][None]
    gf = normf_g[None]
    w_in_a = w_in[0, :, :A_WIDTH].astype(BF16)
    w_in_b = w_in[0, :, A_WIDTH:].astype(BF16)
    w_o_a = w_out[0, :A_WIDTH].astype(BF16)
    w_g = w_gate[0].astype(BF16)
    w_u = w_up[0].astype(BF16)
    w_d = w_down[0].astype(BF16)
    w2, w_po = _prep(w_fourier[0], w_pool[0], pool_scale[0][None], w_out[0])

    def trunk(x):
        f1, tr, ts, f3 = _dft_tables(x.shape[1], DFT_SPLIT)
        g = _k1(x, g1, w_in_a, w2, f1)
        ya = _k2(g, tr, ts, f3)
        return _k3(x, ya, g1, w_in_b, w_o_a, w_po, g2, w_g, w_u, w_d, gf)

    return trunk(x_prompt), trunk(x_sample)
```

```python
import functools

import jax
import jax.numpy as jnp
import numpy as np
from jax.experimental import pallas as pl
from jax.experimental.pallas import tpu as pltpu

F32 = jnp.float32
BF16 = jnp.bfloat16

D_MODEL = 1024
N_HEADS = 4
HEAD_DIM = 128
A_WIDTH = N_HEADS * HEAD_DIM
B_WIDTH = N_HEADS * HEAD_DIM
POOL_WINDOWS = (2, 4, 8, 16)
D_FF = 2816
EPS = 1e-6

HALO = 16
TOKEN_TILE = 512
FF_CHUNK = 256
DFT_SPLIT = 128
S2_BLOCK = 8
K1_BLOCK = 8
S2_STEP = 2
K1_STEP = 2
VMEM_LIMIT_V7X = 56 * 1024 * 1024


def _const_spec(shape):
    zeros = (0,) * len(shape)
    return pl.BlockSpec(shape, lambda *_: zeros, pipeline_mode=pl.Buffered(1))


def _rmsnorm(x, g):
    r = jax.lax.rsqrt(jnp.mean(x * x, axis=-1, keepdims=True) + EPS)
    return x * r * g


def _dot(a, b):
    return jnp.dot(a, b, preferred_element_type=F32)


def _split_bf16(a):
    hi = a.astype(BF16)
    lo = (a - hi.astype(F32)).astype(BF16)
    return hi, lo


def _dot_f32(a, b):
    ah, al = _split_bf16(a)
    bh, bl = _split_bf16(b)
    return _dot(ah, bh) + (_dot(ah, bl) + _dot(al, bh))


def _prep_kernel(cs_ref, wf_ref, wp_ref, ps_ref, wout_ref, w2_ref, wpo_ref):
    for h in range(N_HEADS):
        w = wf_ref[h]
        w2_ref[h, :, :HEAD_DIM] = _dot_f32(cs_ref[0], w).astype(BF16)
        w2_ref[h, :, HEAD_DIM:] = (-_dot_f32(cs_ref[1], w)).astype(BF16)
    for g in range(N_HEADS):
        lanes = slice(g * HEAD_DIM, (g + 1) * HEAD_DIM)
        rows = slice(A_WIDTH + g * HEAD_DIM, A_WIDTH + (g + 1) * HEAD_DIM)
        wpo_ref[lanes, :] = _dot_f32(wp_ref[g] * ps_ref[:, lanes], wout_ref[rows, :]).astype(BF16)


def _prep(w_fourier, w_pool, pool_scale, w_out):
    n = np.arange(HEAD_DIM)
    ang = 2.0 * np.pi * np.outer(n, n) / HEAD_DIM
    cs = np.stack([np.cos(ang), np.sin(ang)]) / np.sqrt(HEAD_DIM)
    return pl.pallas_call(
        _prep_kernel,
        out_shape=(jax.ShapeDtypeStruct((N_HEADS, HEAD_DIM, 2 * HEAD_DIM), BF16),
                   jax.ShapeDtypeStruct((B_WIDTH, D_MODEL), BF16)),
        name="prep_weights",
    )(jnp.asarray(cs, F32), w_fourier, w_pool, pool_scale, w_out)


def _k1_kernel(x_ref, g1_ref, win_ref, w2_ref, f1_ref, g_ref, ab_ref, *, n1):
    for sb in range(S2_STEP):
        _k1_group(x_ref, g1_ref, win_ref, w2_ref, f1_ref, g_ref, ab_ref, sb, n1)


def _k1_group(x_ref, g1_ref, win_ref, w2_ref, f1_ref, g_ref, ab_ref, sb, n1):
    rows = n1 * S2_BLOCK
    x = x_ref[0, :, sb * S2_BLOCK:(sb + 1) * S2_BLOCK, :].reshape(rows, D_MODEL)
    h = _rmsnorm(x, g1_ref[...]).astype(BF16)
    za = _dot(h, win_ref[...]).astype(BF16)
    for hd in range(N_HEADS):
        ab = _dot(za[:, hd * HEAD_DIM:(hd + 1) * HEAD_DIM], w2_ref[hd])
        ab_ref[hd, 0] = ab[:, :HEAD_DIM]
        ab_ref[hd, 1] = ab[:, HEAD_DIM:]
    for hd in range(N_HEADS):
        xx = jnp.concatenate(
            [jnp.concatenate([ab_ref.at[hd, ri][pl.ds(j, n1, stride=S2_BLOCK), :]
                              for j in range(S2_BLOCK)], axis=1) for ri in range(2)],
            axis=0).astype(BF16)
        g = _dot(f1_ref[...], xx)
        out_r = g_ref.at[0, hd, 0, sb].reshape(rows, HEAD_DIM)
        out_i = g_ref.at[0, hd, 1, sb].reshape(rows, HEAD_DIM)
        for j in range(S2_BLOCK):
            out_r[pl.ds(j, n1, stride=S2_BLOCK), :] = g[:n1, j * HEAD_DIM:(j + 1) * HEAD_DIM]
            out_i[pl.ds(j, n1, stride=S2_BLOCK), :] = g[n1:, j * HEAD_DIM:(j + 1) * HEAD_DIM]


def _dft_tables(s, n1):
    n2 = s // n1
    k1 = np.arange(n1)
    a1 = 2.0 * np.pi * np.outer(k1, k1) / n1
    c1, s1 = np.cos(a1), np.sin(a1)
    f1 = np.block([[c1, s1], [-s1, c1]]) / np.sqrt(n1)
    at = 2.0 * np.pi * np.outer(k1, np.arange(n2)) / s
    k2 = np.arange(n2)
    a2 = 2.0 * np.pi * np.outer(k2, k2) / n2
    f3 = np.concatenate([np.cos(a2), np.sin(a2)], axis=1) / np.sqrt(n2)
    lanes = np.ones((1, 1, HEAD_DIM))
    tr = jnp.asarray(np.cos(at)[:, :, None] * lanes, F32)
    ts = jnp.asarray(np.sin(at)[:, :, None] * lanes, F32)
    return jnp.asarray(f1, F32).astype(BF16), tr, ts, jnp.asarray(f3, F32).astype(BF16)


def _k1(x, g1, w_in_a, w2, f1):
    b, s, _ = x.shape
    n1 = DFT_SPLIT
    n2 = s // n1
    return pl.pallas_call(
        functools.partial(_k1_kernel, n1=n1),
        grid=(b, n2 // (S2_STEP * S2_BLOCK)),
        in_specs=[
            pl.BlockSpec((1, n1, S2_STEP * S2_BLOCK, D_MODEL), lambda bi, j: (bi, 0, j, 0)),
            _const_spec((1, D_MODEL)),
            _const_spec((D_MODEL, A_WIDTH)),
            _const_spec((N_HEADS, HEAD_DIM, 2 * HEAD_DIM)),
            _const_spec((2 * n1, 2 * n1)),
        ],
        out_specs=pl.BlockSpec((1, N_HEADS, 2, S2_STEP, n1, S2_BLOCK, HEAD_DIM),
                               lambda bi, j: (bi, 0, 0, j, 0, 0, 0)),
        out_shape=jax.ShapeDtypeStruct((b, N_HEADS, 2, n2 // S2_BLOCK, n1, S2_BLOCK, HEAD_DIM), F32),
        scratch_shapes=[pltpu.VMEM((N_HEADS, 2, n1 * S2_BLOCK, HEAD_DIM), F32)],
        compiler_params=pltpu.CompilerParams(
            dimension_semantics=("parallel", "parallel"), vmem_limit_bytes=VMEM_LIMIT_V7X),
        name="k1_in_proj_dft1",
    )(x.reshape(b, n1, n2, D_MODEL), g1, w_in_a, w2, f1)


def _k2_kernel(g_ref, tr_ref, ts_ref, f3_ref, y_ref, *, n2):
    for k in range(K1_STEP * K1_BLOCK):
        tr = tr_ref[k]
        ts = ts_ref[k]
        pr = []
        pi = []
        for hd in range(N_HEADS):
            gr = g_ref[0, hd, 0, :, k].reshape(n2, HEAD_DIM)
            gi = g_ref[0, hd, 1, :, k].reshape(n2, HEAD_DIM)
            pr.append(gr * tr + gi * ts)
            pi.append(gi * tr - gr * ts)
        x = jnp.concatenate([jnp.concatenate(pr, axis=1), jnp.concatenate(pi, axis=1)],
                            axis=0).astype(BF16)
        r = _dot(f3_ref[...], x)
        for hd in range(N_HEADS):
            out = y_ref.at[0, hd, k // K1_BLOCK].reshape(n2 * K1_BLOCK, HEAD_DIM)
            out[pl.ds(k % K1_BLOCK, n2, stride=K1_BLOCK), :] = r[:, hd * HEAD_DIM:(hd + 1) * HEAD_DIM]


def _k2(g, tr, ts, f3):
    b, _, _, nj, n1, _, _ = g.shape
    n2 = nj * S2_BLOCK
    return pl.pallas_call(
        functools.partial(_k2_kernel, n2=n2),
        grid=(b, n1 // (K1_STEP * K1_BLOCK)),
        in_specs=[
            pl.BlockSpec((1, N_HEADS, 2, nj, K1_STEP * K1_BLOCK, S2_BLOCK, HEAD_DIM),
                         lambda bi, j: (bi, 0, 0, 0, j, 0, 0)),
            pl.BlockSpec((K1_STEP * K1_BLOCK, n2, HEAD_DIM), lambda bi, j: (j, 0, 0)),
            pl.BlockSpec((K1_STEP * K1_BLOCK, n2, HEAD_DIM), lambda bi, j: (j, 0, 0)),
            _const_spec((n2, 2 * n2)),
        ],
        out_specs=pl.BlockSpec((1, N_HEADS, K1_STEP, n2, K1_BLOCK, HEAD_DIM),
                               lambda bi, j: (bi, 0, j, 0, 0, 0)),
        out_shape=jax.ShapeDtypeStruct((b, N_HEADS, n1 // K1_BLOCK, n2, K1_BLOCK, HEAD_DIM), F32),
        compiler_params=pltpu.CompilerParams(
            dimension_semantics=("parallel", "parallel"), vmem_limit_bytes=VMEM_LIMIT_V7X),
        name="k2_dft2",
    )(g, tr, ts, f3)


def _k3_kernel(x_ref, xp_ref, xn_ref, ya_ref, g1_ref, winb_ref, wouta_ref, wpo_ref, g2_ref,
               wg_ref, wu_ref, wd_ref, gf_ref, o_ref, hs_ref, zs_ref, act_ref, *, seq_len):
    tm = x_ref.shape[1]
    i = pl.program_id(1)
    last = pl.num_programs(1) - 1
    x = x_ref[0]

    ya = jnp.concatenate(
        [jnp.swapaxes(ya_ref[0, hd], 0, 1).reshape(tm, HEAD_DIM).astype(BF16) for hd in range(N_HEADS)],
        axis=-1)
    x1 = x + _dot(ya, wouta_ref[...])

    g1 = g1_ref[...]
    hs_ref[:HALO] = _rmsnorm(xp_ref[0], g1).astype(BF16)
    hs_ref[HALO:HALO + tm] = _rmsnorm(x, g1).astype(BF16)
    hs_ref[HALO + tm:] = _rmsnorm(xn_ref[0], g1).astype(BF16)
    zb = _dot(hs_ref[...], winb_ref[...])
    row = jax.lax.broadcasted_iota(jnp.int32, (tm + 2 * HALO, 1), 0)
    inside = ((row >= HALO) | (i > 0)) & ((row < HALO + tm) | (i < last))
    zb = jnp.where(inside, zb, 0.0)
    for gi in range(N_HEADS):
        zs_ref[gi] = zb[:, gi * HEAD_DIM:(gi + 1) * HEAD_DIM]

    t = i * tm + jax.lax.broadcasted_iota(jnp.int32, (tm, 1), 0)
    d = []
    for gi, k in enumerate(POOL_WINDOWS):
        sums = zs_ref[gi, pl.ds(HALO - k // 2, tm), :]
        for j in range(1, k):
            sums = sums + zs_ref[gi, pl.ds(HALO - k // 2 + j, tm), :]
        cnt = jnp.minimum(t + k // 2, seq_len) - jnp.maximum(t - k // 2, 0)
        d.append((sums / cnt.astype(F32) - zs_ref[gi, pl.ds(HALO, tm), :]).astype(BF16))

    x1 = x1 + _dot(jnp.concatenate(d, axis=-1), wpo_ref[...])
    h2 = _rmsnorm(x1, g2_ref[...]).astype(BF16)
    for c in range(D_FF // FF_CHUNK):
        cols = slice(c * FF_CHUNK, (c + 1) * FF_CHUNK)
        gate = _dot(h2, wg_ref[:, cols])
        up = _dot(h2, wu_ref[:, cols])
        act_ref[:, cols] = (gate * jax.nn.sigmoid(gate) * up).astype(BF16)
    x2 = x1 + _dot(act_ref[...], wd_ref[...])
    o_ref[0] = _rmsnorm(x2, gf_ref[...])


def _k3(x, ya, g1, w_in_b, w_out_a, w_po, g2, w_gate, w_up, w_down, gf):
    b, s, _ = x.shape
    tm = TOKEN_TILE
    hb = tm // HALO
    n_hb = s // HALO
    return pl.pallas_call(
        functools.partial(_k3_kernel, seq_len=s),
        grid=(b, s // tm),
        in_specs=[
            pl.BlockSpec((1, tm, D_MODEL), lambda bi, i: (bi, i, 0)),
            pl.BlockSpec((1, HALO, D_MODEL), lambda bi, i: (bi, jnp.maximum(i * hb - 1, 0), 0)),
            pl.BlockSpec((1, HALO, D_MODEL), lambda bi, i: (bi, jnp.minimum((i + 1) * hb, n_hb - 1), 0)),
            pl.BlockSpec((1, N_HEADS, DFT_SPLIT // K1_BLOCK, tm // DFT_SPLIT, K1_BLOCK, HEAD_DIM),
                         lambda bi, i: (bi, 0, 0, i, 0, 0)),
            _const_spec((1, D_MODEL)),
            _const_spec((D_MODEL, B_WIDTH)),
            _const_spec((A_WIDTH, D_MODEL)),
            _const_spec((B_WIDTH, D_MODEL)),
            _const_spec((1, D_MODEL)),
            _const_spec((D_MODEL, D_FF)),
            _const_spec((D_MODEL, D_FF)),
            _const_spec((D_FF, D_MODEL)),
            _const_spec((1, D_MODEL)),
        ],
        out_specs=pl.BlockSpec((1, tm, D_MODEL), lambda bi, i: (bi, i, 0)),
        out_shape=jax.ShapeDtypeStruct((b, s, D_MODEL), F32),
        scratch_shapes=[
            pltpu.VMEM((tm + 2 * HALO, D_MODEL), BF16),
            pltpu.VMEM((N_HEADS, tm + 2 * HALO, HEAD_DIM), F32),
            pltpu.VMEM((tm, D_FF), BF16),
        ],
        compiler_params=pltpu.CompilerParams(
            dimension_semantics=("parallel", "parallel"), vmem_limit_bytes=VMEM_LIMIT_V7X),
        name="k3_pool_out_ffn",
    )(x, x, x, ya, g1, w_in_b, w_out_a, w_po, g2, w_gate, w_up, w_down, gf)


def kernel(x_prompt, x_sample, norm1_g, w_in, w_fourier, w_pool, pool_scale, w_out, norm2_g,
           w_gate, w_up, w_down, normf_g):
    assert norm1_g.shape[0] == 1, "single-layer block"
    g1 = norm1_g[0][None]
    g2 = norm2_g[0][None]
    gf = normf_g[None]
    w_in_a = w_in[0, :, :A_WIDTH].astype(BF16)
    w_in_b = w_in[0, :, A_WIDTH:].astype(BF16)
    w_o_a = w_out[0, :A_WIDTH].astype(BF16)
    w_g = w_gate[0].astype(BF16)
    w_u = w_up[0].astype(BF16)
    w_d = w_down[0].astype(BF16)
    w2, w_po = _prep(w_fourier[0], w_pool[0], pool_scale[0][None], w_out[0])

    def trunk(x):
        f1, tr, ts, f3 = _dft_tables(x.shape[1], DFT_SPLIT)
        g = _k1(x, g1, w_in_a, w2, f1)
        ya = _k2(g, tr, ts, f3)
        return _k3(x, ya, g1, w_in_b, w_o_a, w_po, g2, w_g, w_u, w_d, gf)

    return trunk(x_prompt), trunk(x_sample)
```

```python
import functools

import jax
import jax.numpy as jnp
import numpy as np
from jax.experimental import pallas as pl
from jax.experimental.pallas import tpu as pltpu

F32 = jnp.float32
BF16 = jnp.bfloat16

D_MODEL = 1024
N_HEADS = 4
HEAD_DIM = 128
A_WIDTH = N_HEADS * HEAD_DIM
B_WIDTH = N_HEADS * HEAD_DIM
POOL_WINDOWS = (2, 4, 8, 16)
D_FF = 2816
EPS = 1e-6

HALO = 16
TOKEN_TILE = 1024
FF_CHUNK = 256
DFT_SPLIT = 128
S2_BLOCK = 8
K1_BLOCK = 8
S2_STEP = 2
K1_STEP = 2
VMEM_LIMIT_V7X = 58 * 1024 * 1024


def _const_spec(shape):
    zeros = (0,) * len(shape)
    return pl.BlockSpec(shape, lambda *_: zeros, pipeline_mode=pl.Buffered(1))


def _rmsnorm(x, g):
    r = jax.lax.rsqrt(jnp.mean(x * x, axis=-1, keepdims=True) + EPS)
    return x * r * g


def _dot(a, b):
    return jnp.dot(a, b, preferred_element_type=F32)


def _split_bf16(a):
    hi = a.astype(BF16)
    lo = (a - hi.astype(F32)).astype(BF16)
    return hi, lo


def _dot_f32(a, b):
    ah, al = _split_bf16(a)
    bh, bl = _split_bf16(b)
    return _dot(ah, bh) + (_dot(ah, bl) + _dot(al, bh))


def _prep_kernel(cs_ref, wf_ref, wp_ref, ps_ref, wout_ref, w2_ref, wpo_ref):
    for h in range(N_HEADS):
        w = wf_ref[h]
        w2_ref[h, :, :HEAD_DIM] = _dot_f32(cs_ref[0], w).astype(BF16)
        w2_ref[h, :, HEAD_DIM:] = (-_dot_f32(cs_ref[1], w)).astype(BF16)
    for g in range(N_HEADS):
        lanes = slice(g * HEAD_DIM, (g + 1) * HEAD_DIM)
        rows = slice(A_WIDTH + g * HEAD_DIM, A_WIDTH + (g + 1) * HEAD_DIM)
        wpo_ref[lanes, :] = _dot_f32(wp_ref[g] * ps_ref[:, lanes], wout_ref[rows, :]).astype(BF16)


def _prep(w_fourier, w_pool, pool_scale, w_out):
    n = np.arange(HEAD_DIM)
    ang = 2.0 * np.pi * np.outer(n, n) / HEAD_DIM
    cs = np.stack([np.cos(ang), np.sin(ang)]) / np.sqrt(HEAD_DIM)
    return pl.pallas_call(
        _prep_kernel,
        out_shape=(jax.ShapeDtypeStruct((N_HEADS, HEAD_DIM, 2 * HEAD_DIM), BF16),
                   jax.ShapeDtypeStruct((B_WIDTH, D_MODEL), BF16)),
        name="prep_weights",
    )(jnp.asarray(cs, F32), w_fourier, w_pool, pool_scale, w_out)


def _k1_kernel(x_ref, g1_ref, win_ref, w2_ref, f1_ref, g_ref, ab_ref, *, n1):
    for sb in range(S2_STEP):
        _k1_group(x_ref, g1_ref, win_ref, w2_ref, f1_ref, g_ref, ab_ref, sb, n1)


def _k1_group(x_ref, g1_ref, win_ref, w2_ref, f1_ref, g_ref, ab_ref, sb, n1):
    rows = n1 * S2_BLOCK
    x = x_ref[0, :, sb * S2_BLOCK:(sb + 1) * S2_BLOCK, :].reshape(rows, D_MODEL)
    h = _rmsnorm(x, g1_ref[...]).astype(BF16)
    za = _dot(h, win_ref[...]).astype(BF16)
    for hd in range(N_HEADS):
        ab = _dot(za[:, hd * HEAD_DIM:(hd + 1) * HEAD_DIM], w2_ref[hd])
        ab_ref[hd, 0] = ab[:, :HEAD_DIM]
        ab_ref[hd, 1] = ab[:, HEAD_DIM:]
    for hd in range(N_HEADS):
        xx = jnp.concatenate(
            [jnp.concatenate([ab_ref.at[hd, ri][pl.ds(j, n1, stride=S2_BLOCK), :]
                              for j in range(S2_BLOCK)], axis=1) for ri in range(2)],
            axis=0).astype(BF16)
        g = _dot(f1_ref[...], xx)
        out_r = g_ref.at[0, hd, 0, sb].reshape(rows, HEAD_DIM)
        out_i = g_ref.at[0, hd, 1, sb].reshape(rows, HEAD_DIM)
        for j in range(S2_BLOCK):
            out_r[pl.ds(j, n1, stride=S2_BLOCK), :] = g[:n1, j * HEAD_DIM:(j + 1) * HEAD_DIM]
            out_i[pl.ds(j, n1, stride=S2_BLOCK), :] = g[n1:, j * HEAD_DIM:(j + 1) * HEAD_DIM]


def _dft_tables(s, n1):
    n2 = s // n1
    k1 = np.arange(n1)
    a1 = 2.0 * np.pi * np.outer(k1, k1) / n1
    c1, s1 = np.cos(a1), np.sin(a1)
    f1 = np.block([[c1, s1], [-s1, c1]]) / np.sqrt(n1)
    at = 2.0 * np.pi * np.outer(k1, np.arange(n2)) / s
    k2 = np.arange(n2)
    a2 = 2.0 * np.pi * np.outer(k2, k2) / n2
    f3 = np.concatenate([np.cos(a2), np.sin(a2)], axis=1) / np.sqrt(n2)
    lanes = np.ones((1, 1, HEAD_DIM))
    tr = jnp.asarray(np.cos(at)[:, :, None] * lanes, F32)
    ts = jnp.asarray(np.sin(at)[:, :, None] * lanes, F32)
    return jnp.asarray(f1, F32).astype(BF16), tr, ts, jnp.asarray(f3, F32).astype(BF16)


def _k1(x, g1, w_in_a, w2, f1):
    b, s, _ = x.shape
    n1 = DFT_SPLIT
    n2 = s // n1
    return pl.pallas_call(
        functools.partial(_k1_kernel, n1=n1),
        grid=(b, n2 // (S2_STEP * S2_BLOCK)),
        in_specs=[
            pl.BlockSpec((1, n1, S2_STEP * S2_BLOCK, D_MODEL), lambda bi, j: (bi, 0, j, 0)),
            _const_spec((1, D_MODEL)),
            _const_spec((D_MODEL, A_WIDTH)),
            _const_spec((N_HEADS, HEAD_DIM, 2 * HEAD_DIM)),
            _const_spec((2 * n1, 2 * n1)),
        ],
        out_specs=pl.BlockSpec((1, N_HEADS, 2, S2_STEP, n1, S2_BLOCK, HEAD_DIM),
                               lambda bi, j: (bi, 0, 0, j, 0, 0, 0)),
        out_shape=jax.ShapeDtypeStruct((b, N_HEADS, 2, n2 // S2_BLOCK, n1, S2_BLOCK, HEAD_DIM), F32),
        scratch_shapes=[pltpu.VMEM((N_HEADS, 2, n1 * S2_BLOCK, HEAD_DIM), F32)],
        compiler_params=pltpu.CompilerParams(
            dimension_semantics=("parallel", "parallel"), vmem_limit_bytes=VMEM_LIMIT_V7X),
        name="k1_in_proj_dft1",
    )(x.reshape(b, n1, n2, D_MODEL), g1, w_in_a, w2, f1)


def _k2_kernel(g_ref, tr_ref, ts_ref, f3_ref, y_ref, *, n2):
    for k in range(K1_STEP * K1_BLOCK):
        tr = tr_ref[k]
        ts = ts_ref[k]
        pr = []
        pi = []
        for hd in range(N_HEADS):
            gr = g_ref[0, hd, 0, :, k].reshape(n2, HEAD_DIM)
            gi = g_ref[0, hd, 1, :, k].reshape(n2, HEAD_DIM)
            pr.append(gr * tr + gi * ts)
            pi.append(gi * tr - gr * ts)
        x = jnp.concatenate([jnp.concatenate(pr, axis=1), jnp.concatenate(pi, axis=1)],
                            axis=0).astype(BF16)
        r = _dot(f3_ref[...], x)
        for hd in range(N_HEADS):
            out = y_ref.at[0, hd, k // K1_BLOCK].reshape(n2 * K1_BLOCK, HEAD_DIM)
            out[pl.ds(k % K1_BLOCK, n2, stride=K1_BLOCK), :] = r[:, hd * HEAD_DIM:(hd + 1) * HEAD_DIM]


def _k2(g, tr, ts, f3):
    b, _, _, nj, n1, _, _ = g.shape
    n2 = nj * S2_BLOCK
    return pl.pallas_call(
        functools.partial(_k2_kernel, n2=n2),
        grid=(b, n1 // (K1_STEP * K1_BLOCK)),
        in_specs=[
            pl.BlockSpec((1, N_HEADS, 2, nj, K1_STEP * K1_BLOCK, S2_BLOCK, HEAD_DIM),
                         lambda bi, j: (bi, 0, 0, 0, j, 0, 0)),
            pl.BlockSpec((K1_STEP * K1_BLOCK, n2, HEAD_DIM), lambda bi, j: (j, 0, 0)),
            pl.BlockSpec((K1_STEP * K1_BLOCK, n2, HEAD_DIM), lambda bi, j: (j, 0, 0)),
            _const_spec((n2, 2 * n2)),
        ],
        out_specs=pl.BlockSpec((1, N_HEADS, K1_STEP, n2, K1_BLOCK, HEAD_DIM),
                               lambda bi, j: (bi, 0, j, 0, 0, 0)),
        out_shape=jax.ShapeDtypeStruct((b, N_HEADS, n1 // K1_BLOCK, n2, K1_BLOCK, HEAD_DIM), F32),
        compiler_params=pltpu.CompilerParams(
            dimension_semantics=("parallel", "parallel"), vmem_limit_bytes=VMEM_LIMIT_V7X),
        name="k2_dft2",
    )(g, tr, ts, f3)


def _k3_kernel(x_ref, xp_ref, xn_ref, ya_ref, g1_ref, winb_ref, wouta_ref, wpo_ref, g2_ref,
               wg_ref, wu_ref, wd_ref, gf_ref, o_ref, hs_ref, zs_ref, act_ref, *, seq_len):
    tm = x_ref.shape[1]
    i = pl.program_id(1)
    last = pl.num_programs(1) - 1
    x = x_ref[0]

    ya = jnp.concatenate(
        [jnp.swapaxes(ya_ref[0, hd], 0, 1).reshape(tm, HEAD_DIM).astype(BF16) for hd in range(N_HEADS)],
        axis=-1)
    x1 = x + _dot(ya, wouta_ref[...])

    g1 = g1_ref[...]
    hs_ref[:HALO] = _rmsnorm(xp_ref[0], g1).astype(BF16)
    hs_ref[HALO:HALO + tm] = _rmsnorm(x, g1).astype(BF16)
    hs_ref[HALO + tm:] = _rmsnorm(xn_ref[0], g1).astype(BF16)
    zb = _dot(hs_ref[...], winb_ref[...])
    row = jax.lax.broadcasted_iota(jnp.int32, (tm + 2 * HALO, 1), 0)
    inside = ((row >= HALO) | (i > 0)) & ((row < HALO + tm) | (i < last))
    zb = jnp.where(inside, zb, 0.0)
    for gi in range(N_HEADS):
        zs_ref[gi] = zb[:, gi * HEAD_DIM:(gi + 1) * HEAD_DIM]

    t = i * tm + jax.lax.broadcasted_iota(jnp.int32, (tm, 1), 0)
    d = []
    for gi, k in enumerate(POOL_WINDOWS):
        sums = zs_ref[gi, pl.ds(HALO - k // 2, tm), :]
        for j in range(1, k):
            sums = sums + zs_ref[gi, pl.ds(HALO - k // 2 + j, tm), :]
        cnt = jnp.minimum(t + k // 2, seq_len) - jnp.maximum(t - k // 2, 0)
        d.append((sums / cnt.astype(F32) - zs_ref[gi, pl.ds(HALO, tm), :]).astype(BF16))

    x1 = x1 + _dot(jnp.concatenate(d, axis=-1), wpo_ref[...])
    h2 = _rmsnorm(x1, g2_ref[...]).astype(BF16)
    for c in range(D_FF // FF_CHUNK):
        cols = slice(c * FF_CHUNK, (c + 1) * FF_CHUNK)
        gate = _dot(h2, wg_ref[:, cols])
        up = _dot(h2, wu_ref[:, cols])
        act_ref[:, cols] = (gate * jax.nn.sigmoid(gate) * up).astype(BF16)
    x2 = x1 + _dot(act_ref[...], wd_ref[...])
    o_ref[0] = _rmsnorm(x2, gf_ref[...])


def _k3(x, ya, g1, w_in_b, w_out_a, w_po, g2, w_gate, w_up, w_down, gf):
    b, s, _ = x.shape
    tm = TOKEN_TILE
    hb = tm // HALO
    n_hb = s // HALO
    return pl.pallas_call(
        functools.partial(_k3_kernel, seq_len=s),
        grid=(b, s // tm),
        in_specs=[
            pl.BlockSpec((1, tm, D_MODEL), lambda bi, i: (bi, i, 0)),
            pl.BlockSpec((1, HALO, D_MODEL), lambda bi, i: (bi, jnp.maximum(i * hb - 1, 0), 0)),
            pl.BlockSpec((1, HALO, D_MODEL), lambda bi, i: (bi, jnp.minimum((i + 1) * hb, n_hb - 1), 0)),
            pl.BlockSpec((1, N_HEADS, DFT_SPLIT // K1_BLOCK, tm // DFT_SPLIT, K1_BLOCK, HEAD_DIM),
                         lambda bi, i: (bi, 0, 0, i, 0, 0)),
            _const_spec((1, D_MODEL)),
            _const_spec((D_MODEL, B_WIDTH)),
            _const_spec((A_WIDTH, D_MODEL)),
            _const_spec((B_WIDTH, D_MODEL)),
            _const_spec((1, D_MODEL)),
            _const_spec((D_MODEL, D_FF)),
            _const_spec((D_MODEL, D_FF)),
            _const_spec((D_FF, D_MODEL)),
            _const_spec((1, D_MODEL)),
        ],
        out_specs=pl.BlockSpec((1, tm, D_MODEL), lambda bi, i: (bi, i, 0)),
        out_shape=jax.ShapeDtypeStruct((b, s, D_MODEL), F32),
        scratch_shapes=[
            pltpu.VMEM((tm + 2 * HALO, D_MODEL), BF16),
            pltpu.VMEM((N_HEADS, tm + 2 * HALO, HEAD_DIM), F32),
            pltpu.VMEM((tm, D_FF), BF16),
        ],
        compiler_params=pltpu.CompilerParams(
            dimension_semantics=("parallel", "parallel"), vmem_limit_bytes=VMEM_LIMIT_V7X),
        name="k3_pool_out_ffn",
    )(x, x, x, ya, g1, w_in_b, w_out_a, w_po, g2, w_gate, w_up, w_down, gf)


def kernel(x_prompt, x_sample, norm1_g, w_in, w_fourier, w_pool, pool_scale, w_out, norm2_g,
           w_gate, w_up, w_down, normf_g):
    assert norm1_g.shape[0] == 1, "single-layer block"
    g1 = norm1_g[0][None]
    g2 = norm2_g[0][None]
    gf = normf_g[None]
    w_in_a = w_in[0, :, :A_WIDTH].astype(BF16)
    w_in_b = w_in[0, :, A_WIDTH:].astype(BF16)
    w_o_a = w_out[0, :A_WIDTH].astype(BF16)
    w_g = w_gate[0].astype(BF16)
    w_u = w_up[0].astype(BF16)
    w_d = w_down[0].astype(BF16)
    w2, w_po = _prep(w_fourier[0], w_pool[0], pool_scale[0][None], w_out[0])

    def trunk(x):
        f1, tr, ts, f3 = _dft_tables(x.shape[1], DFT_SPLIT)
        g = _k1(x, g1, w_in_a, w2, f1)
        ya = _k2(g, tr, ts, f3)
        return _k3(x, ya, g1, w_in_b, w_o_a, w_po, g2, w_g, w_u, w_d, gf)

    return trunk(x_prompt), trunk(x_sample)
```

```python
import functools

import jax
import jax.numpy as jnp
import numpy as np
from jax.experimental import pallas as pl
from jax.experimental.pallas import tpu as pltpu

F32 = jnp.float32
BF16 = jnp.bfloat16

D_MODEL = 1024
N_HEADS = 4
HEAD_DIM = 128
A_WIDTH = N_HEADS * HEAD_DIM
B_WIDTH = N_HEADS * HEAD_DIM
POOL_WINDOWS = (2, 4, 8, 16)
D_FF = 2816
EPS = 1e-6

HALO = 16
TOKEN_TILE = 1024
FF_CHUNK = 256
DFT_SPLIT = 128
S2_BLOCK = 8
S2_STEP = 2
K1_PAIRS = 8
VMEM_LIMIT_V7X = 58 * 1024 * 1024


def _const_spec(shape):
    zeros = (0,) * len(shape)
    return pl.BlockSpec(shape, lambda *_: zeros, pipeline_mode=pl.Buffered(1))


def _rmsnorm(x, g):
    r = jax.lax.rsqrt(jnp.mean(x * x, axis=-1, keepdims=True) + EPS)
    return x * r * g


def _dot(a, b):
    return jnp.dot(a, b, preferred_element_type=F32)


def _split_bf16(a):
    hi = a.astype(BF16)
    lo = (a - hi.astype(F32)).astype(BF16)
    return hi, lo


def _dot_f32(a, b):
    ah, al = _split_bf16(a)
    bh, bl = _split_bf16(b)
    return _dot(ah, bh) + (_dot(ah, bl) + _dot(al, bh))


def _prep_kernel(cs_ref, wf_ref, wp_ref, ps_ref, wout_ref, w2_ref, wpo_ref):
    for h in range(N_HEADS):
        w = wf_ref[h]
        w2_ref[h, :, :HEAD_DIM] = _dot_f32(cs_ref[0], w).astype(BF16)
        w2_ref[h, :, HEAD_DIM:] = (-_dot_f32(cs_ref[1], w)).astype(BF16)
    for g in range(N_HEADS):
        lanes = slice(g * HEAD_DIM, (g + 1) * HEAD_DIM)
        rows = slice(A_WIDTH + g * HEAD_DIM, A_WIDTH + (g + 1) * HEAD_DIM)
        wpo_ref[lanes, :] = _dot_f32(wp_ref[g] * ps_ref[:, lanes], wout_ref[rows, :]).astype(BF16)


def _prep(w_fourier, w_pool, pool_scale, w_out):
    n = np.arange(HEAD_DIM)
    ang = 2.0 * np.pi * np.outer(n, n) / HEAD_DIM
    cs = np.stack([np.cos(ang), np.sin(ang)]) / np.sqrt(HEAD_DIM)
    return pl.pallas_call(
        _prep_kernel,
        out_shape=(jax.ShapeDtypeStruct((N_HEADS, HEAD_DIM, 2 * HEAD_DIM), BF16),
                   jax.ShapeDtypeStruct((B_WIDTH, D_MODEL), BF16)),
        name="prep_weights",
    )(jnp.asarray(cs, F32), w_fourier, w_pool, pool_scale, w_out)


def _k1_kernel(x_ref, g1_ref, win_ref, w2_ref, f1_ref, g_ref, ab_ref, *, n1):
    for sb in range(S2_STEP):
        _k1_group(x_ref, g1_ref, win_ref, w2_ref, f1_ref, g_ref, ab_ref, sb, n1)


def _k1_group(x_ref, g1_ref, win_ref, w2_ref, f1_ref, g_ref, ab_ref, sb, n1):
    rows = n1 * S2_BLOCK
    x = x_ref[0, :, sb * S2_BLOCK:(sb + 1) * S2_BLOCK, :].reshape(rows, D_MODEL)
    h = _rmsnorm(x, g1_ref[...]).astype(BF16)
    za = _dot(h, win_ref[...]).astype(BF16)
    for hd in range(N_HEADS):
        ab = _dot(za[:, hd * HEAD_DIM:(hd + 1) * HEAD_DIM], w2_ref[hd])
        ab_ref[hd, 0] = ab[:, :HEAD_DIM]
        ab_ref[hd, 1] = ab[:, HEAD_DIM:]
    for hd in range(N_HEADS):
        xx = jnp.concatenate(
            [jnp.concatenate([ab_ref.at[hd, ri][pl.ds(j, n1, stride=S2_BLOCK), :]
                              for j in range(S2_BLOCK)], axis=1) for ri in range(2)],
            axis=0).astype(BF16)
        g = _dot(f1_ref[...], xx)
        gp = pltpu.bitcast(g.astype(BF16), jnp.uint32)
        half = n1 // 2
        for ri in range(2):
            out = g_ref.at[0, hd, ri, sb].reshape(half * S2_BLOCK, HEAD_DIM)
            for j in range(S2_BLOCK):
                out[pl.ds(j, half, stride=S2_BLOCK), :] = (
                    gp[ri * half:(ri + 1) * half, j * HEAD_DIM:(j + 1) * HEAD_DIM])


def _dft_tables(s, n1):
    n2 = s // n1
    k1 = np.arange(n1)
    a1 = 2.0 * np.pi * np.outer(k1, k1) / n1
    c1, s1 = np.cos(a1), np.sin(a1)
    f1 = np.block([[c1, s1], [-s1, c1]]) / np.sqrt(n1)
    at = 2.0 * np.pi * np.outer(k1, np.arange(n2)) / s
    k2 = np.arange(n2)
    a2 = 2.0 * np.pi * np.outer(k2, k2) / n2
    f3 = np.concatenate([np.cos(a2), np.sin(a2)], axis=1) / np.sqrt(n2)
    f3p = np.einsum("krs,ef->kersf", f3.reshape(n2, 2, n2), np.eye(2)).reshape(2 * n2, 4 * n2)
    lanes = np.ones((1, 1, HEAD_DIM))
    atp = at.reshape(n1 // 2, 2, n2).transpose(0, 2, 1).reshape(n1 // 2, 2 * n2)
    tr = jnp.asarray(np.cos(atp)[:, :, None] * lanes, F32)
    ts = jnp.asarray(np.sin(atp)[:, :, None] * lanes, F32)
    return jnp.asarray(f1, F32).astype(BF16), tr, ts, jnp.asarray(f3p, F32).astype(BF16)


def _k1(x, g1, w_in_a, w2, f1):
    b, s, _ = x.shape
    n1 = DFT_SPLIT
    n2 = s // n1
    return pl.pallas_call(
        functools.partial(_k1_kernel, n1=n1),
        grid=(b, n2 // (S2_STEP * S2_BLOCK)),
        in_specs=[
            pl.BlockSpec((1, n1, S2_STEP * S2_BLOCK, D_MODEL), lambda bi, j: (bi, 0, j, 0)),
            _const_spec((1, D_MODEL)),
            _const_spec((D_MODEL, A_WIDTH)),
            _const_spec((N_HEADS, HEAD_DIM, 2 * HEAD_DIM)),
            _const_spec((2 * n1, 2 * n1)),
        ],
        out_specs=pl.BlockSpec((1, N_HEADS, 2, S2_STEP, n1 // 2, S2_BLOCK, HEAD_DIM),
                               lambda bi, j: (bi, 0, 0, j, 0, 0, 0)),
        out_shape=jax.ShapeDtypeStruct((b, N_HEADS, 2, n2 // S2_BLOCK, n1 // 2, S2_BLOCK, HEAD_DIM),
                                       jnp.uint32),
        scratch_shapes=[pltpu.VMEM((N_HEADS, 2, n1 * S2_BLOCK, HEAD_DIM), F32)],
        compiler_params=pltpu.CompilerParams(
            dimension_semantics=("parallel", "parallel"), vmem_limit_bytes=VMEM_LIMIT_V7X),
        name="k1_in_proj_dft1",
    )(x.reshape(b, n1, n2, D_MODEL), g1, w_in_a, w2, f1)


def _k2_kernel(g_ref, tr_ref, ts_ref, f3_ref, y_ref, *, n2):
    for p in range(K1_PAIRS):
        tr = tr_ref[p]
        ts = ts_ref[p]
        pr = []
        pi = []
        for hd in range(N_HEADS):
            gr = pltpu.bitcast(g_ref[0, hd, 0, :, p].reshape(n2, HEAD_DIM), BF16).astype(F32)
            gi = pltpu.bitcast(g_ref[0, hd, 1, :, p].reshape(n2, HEAD_DIM), BF16).astype(F32)
            pr.append(gr * tr + gi * ts)
            pi.append(gi * tr - gr * ts)
        x = jnp.concatenate([jnp.concatenate(pr, axis=1), jnp.concatenate(pi, axis=1)],
                            axis=0).astype(BF16)
        r = _dot(f3_ref[...], x)
        ru = pltpu.bitcast(r.astype(BF16), jnp.uint32)
        for hd in range(N_HEADS):
            out = y_ref.at[0, hd, 0].reshape(n2 * K1_PAIRS, HEAD_DIM)
            out[pl.ds(p, n2, stride=K1_PAIRS), :] = ru[:, hd * HEAD_DIM:(hd + 1) * HEAD_DIM]


def _k2(g, tr, ts, f3):
    b, _, _, nj, half, _, _ = g.shape
    n2 = nj * S2_BLOCK
    return pl.pallas_call(
        functools.partial(_k2_kernel, n2=n2),
        grid=(b, half // K1_PAIRS),
        in_specs=[
            pl.BlockSpec((1, N_HEADS, 2, nj, K1_PAIRS, S2_BLOCK, HEAD_DIM),
                         lambda bi, j: (bi, 0, 0, 0, j, 0, 0)),
            pl.BlockSpec((K1_PAIRS, 2 * n2, HEAD_DIM), lambda bi, j: (j, 0, 0)),
            pl.BlockSpec((K1_PAIRS, 2 * n2, HEAD_DIM), lambda bi, j: (j, 0, 0)),
            _const_spec((2 * n2, 4 * n2)),
        ],
        out_specs=pl.BlockSpec((1, N_HEADS, 1, n2, K1_PAIRS, HEAD_DIM), lambda bi, j: (bi, 0, j, 0, 0, 0)),
        out_shape=jax.ShapeDtypeStruct((b, N_HEADS, half // K1_PAIRS, n2, K1_PAIRS, HEAD_DIM), jnp.uint32),
        compiler_params=pltpu.CompilerParams(
            dimension_semantics=("parallel", "parallel"), vmem_limit_bytes=VMEM_LIMIT_V7X),
        name="k2_dft2",
    )(g, tr, ts, f3)


def _k3_kernel(x_ref, xp_ref, xn_ref, ya_ref, g1_ref, winb_ref, wouta_ref, wpo_ref, g2_ref,
               wg_ref, wu_ref, wd_ref, gf_ref, o_ref, hs_ref, zs_ref, act_ref, *, seq_len):
    tm = x_ref.shape[1]
    i = pl.program_id(1)
    last = pl.num_programs(1) - 1
    x = x_ref[0]

    ya = jnp.concatenate(
        [pltpu.bitcast(jnp.swapaxes(ya_ref[0, hd], 0, 1).reshape(tm // 2, HEAD_DIM), BF16)
         for hd in range(N_HEADS)], axis=-1)
    x1 = x + _dot(ya, wouta_ref[...])

    g1 = g1_ref[...]
    hs_ref[:HALO] = _rmsnorm(xp_ref[0], g1).astype(BF16)
    hs_ref[HALO:HALO + tm] = _rmsnorm(x, g1).astype(BF16)
    hs_ref[HALO + tm:] = _rmsnorm(xn_ref[0], g1).astype(BF16)
    zb = _dot(hs_ref[...], winb_ref[...])
    row = jax.lax.broadcasted_iota(jnp.int32, (tm + 2 * HALO, 1), 0)
    inside = ((row >= HALO) | (i > 0)) & ((row < HALO + tm) | (i < last))
    zb = jnp.where(inside, zb, 0.0)
    for gi in range(N_HEADS):
        zs_ref[gi] = zb[:, gi * HEAD_DIM:(gi + 1) * HEAD_DIM]

    t = i * tm + jax.lax.broadcasted_iota(jnp.int32, (tm, 1), 0)
    d = []
    for gi, k in enumerate(POOL_WINDOWS):
        sums = zs_ref[gi, pl.ds(HALO - k // 2, tm), :]
        for j in range(1, k):
            sums = sums + zs_ref[gi, pl.ds(HALO - k // 2 + j, tm), :]
        cnt = jnp.minimum(t + k // 2, seq_len) - jnp.maximum(t - k // 2, 0)
        d.append((sums / cnt.astype(F32) - zs_ref[gi, pl.ds(HALO, tm), :]).astype(BF16))

    x1 = x1 + _dot(jnp.concatenate(d, axis=-1), wpo_ref[...])
    h2 = _rmsnorm(x1, g2_ref[...]).astype(BF16)
    for c in range(D_FF // FF_CHUNK):
        cols = slice(c * FF_CHUNK, (c + 1) * FF_CHUNK)
        gate = _dot(h2, wg_ref[:, cols])
        up = _dot(h2, wu_ref[:, cols])
        act_ref[:, cols] = (gate * jax.nn.sigmoid(gate) * up).astype(BF16)
    x2 = x1 + _dot(act_ref[...], wd_ref[...])
    o_ref[0] = _rmsnorm(x2, gf_ref[...])


def _k3(x, ya, g1, w_in_b, w_out_a, w_po, g2, w_gate, w_up, w_down, gf):
    b, s, _ = x.shape
    tm = TOKEN_TILE
    hb = tm // HALO
    n_hb = s // HALO
    return pl.pallas_call(
        functools.partial(_k3_kernel, seq_len=s),
        grid=(b, s // tm),
        in_specs=[
            pl.BlockSpec((1, tm, D_MODEL), lambda bi, i: (bi, i, 0)),
            pl.BlockSpec((1, HALO, D_MODEL), lambda bi, i: (bi, jnp.maximum(i * hb - 1, 0), 0)),
            pl.BlockSpec((1, HALO, D_MODEL), lambda bi, i: (bi, jnp.minimum((i + 1) * hb, n_hb - 1), 0)),
            pl.BlockSpec((1, N_HEADS, DFT_SPLIT // (2 * K1_PAIRS), tm // DFT_SPLIT, K1_PAIRS, HEAD_DIM),
                         lambda bi, i: (bi, 0, 0, i, 0, 0)),
            _const_spec((1, D_MODEL)),
            _const_spec((D_MODEL, B_WIDTH)),
            _const_spec((A_WIDTH, D_MODEL)),
            _const_spec((B_WIDTH, D_MODEL)),
            _const_spec((1, D_MODEL)),
            _const_spec((D_MODEL, D_FF)),
            _const_spec((D_MODEL, D_FF)),
            _const_spec((D_FF, D_MODEL)),
            _const_spec((1, D_MODEL)),
        ],
        out_specs=pl.BlockSpec((1, tm, D_MODEL), lambda bi, i: (bi, i, 0)),
        out_shape=jax.ShapeDtypeStruct((b, s, D_MODEL), F32),
        scratch_shapes=[
            pltpu.VMEM((tm + 2 * HALO, D_MODEL), BF16),
            pltpu.VMEM((N_HEADS, tm + 2 * HALO, HEAD_DIM), F32),
            pltpu.VMEM((tm, D_FF), BF16),
        ],
        compiler_params=pltpu.CompilerParams(
            dimension_semantics=("parallel", "parallel"), vmem_limit_bytes=VMEM_LIMIT_V7X),
        name="k3_pool_out_ffn",
    )(x, x, x, ya, g1, w_in_b, w_out_a, w_po, g2, w_gate, w_up, w_down, gf)


def kernel(x_prompt, x_sample, norm1_g, w_in, w_fourier, w_pool, pool_scale, w_out, norm2_g,
           w_gate, w_up, w_down, normf_g):
    assert norm1_g.shape[0] == 1, "single-layer block"
    g1 = norm1_g[0][None]
    g2 = norm2_g[0][None]
    gf = normf_g[None]
    w_in_a = w_in[0, :, :A_WIDTH].astype(BF16)
    w_in_b = w_in[0, :, A_WIDTH:].astype(BF16)
    w_o_a = w_out[0, :A_WIDTH].astype(BF16)
    w_g = w_gate[0].astype(BF16)
    w_u = w_up[0].astype(BF16)
    w_d = w_down[0].astype(BF16)
    w2, w_po = _prep(w_fourier[0], w_pool[0], pool_scale[0][None], w_out[0])

    def trunk(x):
        f1, tr, ts, f3 = _dft_tables(x.shape[1], DFT_SPLIT)
        g = _k1(x, g1, w_in_a, w2, f1)
        ya = _k2(g, tr, ts, f3)
        return _k3(x, ya, g1, w_in_b, w_o_a, w_po, g2, w_g, w_u, w_d, gf)

    return trunk(x_prompt), trunk(x_sample)
```

```python
import functools

import jax
import jax.numpy as jnp
import numpy as np
from jax.experimental import pallas as pl
from jax.experimental.pallas import tpu as pltpu

F32 = jnp.float32
BF16 = jnp.bfloat16

D_MODEL = 1024
N_HEADS = 4
HEAD_DIM = 128
A_WIDTH = N_HEADS * HEAD_DIM
B_WIDTH = N_HEADS * HEAD_DIM
POOL_WINDOWS = (2, 4, 8, 16)
D_FF = 2816
EPS = 1e-6

HALO = 16
TOKEN_TILE = 1024
FF_CHUNK = 256
VPU_SLICE = 64
DFT_SPLIT = 128
S2_BLOCK = 8
S2_STEP = 2
K1_PAIRS = 8
VMEM_LIMIT_V7X = 58 * 1024 * 1024


def _const_spec(shape):
    zeros = (0,) * len(shape)
    return pl.BlockSpec(shape, lambda *_: zeros, pipeline_mode=pl.Buffered(1))


def _rmsnorm(x, g):
    r = jax.lax.rsqrt(jnp.mean(x * x, axis=-1, keepdims=True) + EPS)
    return x * r * g


def _dot(a, b):
    return jnp.dot(a, b, preferred_element_type=F32)


def _split_bf16(a):
    hi = a.astype(BF16)
    lo = (a - hi.astype(F32)).astype(BF16)
    return hi, lo


def _dot_f32(a, b):
    ah, al = _split_bf16(a)
    bh, bl = _split_bf16(b)
    return _dot(ah, bh) + (_dot(ah, bl) + _dot(al, bh))


def _prep_kernel(cs_ref, wf_ref, wp_ref, ps_ref, wout_ref, w2_ref, wpo_ref):
    for h in range(N_HEADS):
        w = wf_ref[h]
        w2_ref[h, :, :HEAD_DIM] = _dot_f32(cs_ref[0], w).astype(BF16)
        w2_ref[h, :, HEAD_DIM:] = (-_dot_f32(cs_ref[1], w)).astype(BF16)
    for g in range(N_HEADS):
        lanes = slice(g * HEAD_DIM, (g + 1) * HEAD_DIM)
        rows = slice(A_WIDTH + g * HEAD_DIM, A_WIDTH + (g + 1) * HEAD_DIM)
        wpo_ref[lanes, :] = _dot_f32(wp_ref[g] * ps_ref[:, lanes], wout_ref[rows, :]).astype(BF16)


def _prep(w_fourier, w_pool, pool_scale, w_out):
    n = np.arange(HEAD_DIM)
    ang = 2.0 * np.pi * np.outer(n, n) / HEAD_DIM
    cs = np.stack([np.cos(ang), np.sin(ang)]) / np.sqrt(HEAD_DIM)
    return pl.pallas_call(
        _prep_kernel,
        out_shape=(jax.ShapeDtypeStruct((N_HEADS, HEAD_DIM, 2 * HEAD_DIM), BF16),
                   jax.ShapeDtypeStruct((B_WIDTH, D_MODEL), BF16)),
        name="prep_weights",
    )(jnp.asarray(cs, F32), w_fourier, w_pool, pool_scale, w_out)


def _k1_kernel(x_ref, g1_ref, win_ref, w2_ref, f1_ref, g_ref, ab_ref, *, n1):
    for sb in range(S2_STEP):
        _k1_group(x_ref, g1_ref, win_ref, w2_ref, f1_ref, g_ref, ab_ref, sb, n1)


def _k1_group(x_ref, g1_ref, win_ref, w2_ref, f1_ref, g_ref, ab_ref, sb, n1):
    rows = n1 * S2_BLOCK
    x = x_ref[0, :, sb * S2_BLOCK:(sb + 1) * S2_BLOCK, :].reshape(rows, D_MODEL)
    h = _rmsnorm(x, g1_ref[...]).astype(BF16)
    za = _dot(h, win_ref[...]).astype(BF16)
    for hd in range(N_HEADS):
        ab = _dot(za[:, hd * HEAD_DIM:(hd + 1) * HEAD_DIM], w2_ref[hd])
        ab_ref[hd, 0] = ab[:, :HEAD_DIM]
        ab_ref[hd, 1] = ab[:, HEAD_DIM:]
    for hd in range(N_HEADS):
        xx = jnp.concatenate(
            [jnp.concatenate([ab_ref.at[hd, ri][pl.ds(j, n1, stride=S2_BLOCK), :]
                              for j in range(S2_BLOCK)], axis=1) for ri in range(2)],
            axis=0).astype(BF16)
        g = _dot(f1_ref[...], xx)
        gp = pltpu.bitcast(g.astype(BF16), jnp.uint32)
        half = n1 // 2
        for ri in range(2):
            out = g_ref.at[0, hd, ri, sb].reshape(half * S2_BLOCK, HEAD_DIM)
            for j in range(S2_BLOCK):
                out[pl.ds(j, half, stride=S2_BLOCK), :] = (
                    gp[ri * half:(ri + 1) * half, j * HEAD_DIM:(j + 1) * HEAD_DIM])


def _dft_tables(s, n1):
    n2 = s // n1
    k1 = np.arange(n1)
    a1 = 2.0 * np.pi * np.outer(k1, k1) / n1
    c1, s1 = np.cos(a1), np.sin(a1)
    f1 = np.block([[c1, s1], [-s1, c1]]) / np.sqrt(n1)
    at = 2.0 * np.pi * np.outer(k1, np.arange(n2)) / s
    k2 = np.arange(n2)
    a2 = 2.0 * np.pi * np.outer(k2, k2) / n2
    f3 = np.concatenate([np.cos(a2), np.sin(a2)], axis=1) / np.sqrt(n2)
    f3p = np.einsum("krs,ef->kersf", f3.reshape(n2, 2, n2), np.eye(2)).reshape(2 * n2, 4 * n2)
    lanes = np.ones((1, 1, HEAD_DIM))
    atp = at.reshape(n1 // 2, 2, n2).transpose(0, 2, 1).reshape(n1 // 2, 2 * n2)
    tr = jnp.asarray(np.cos(atp)[:, :, None] * lanes, F32)
    ts = jnp.asarray(np.sin(atp)[:, :, None] * lanes, F32)
    return jnp.asarray(f1, F32).astype(BF16), tr, ts, jnp.asarray(f3p, F32).astype(BF16)


def _k1(x, g1, w_in_a, w2, f1):
    b, s, _ = x.shape
    n1 = DFT_SPLIT
    n2 = s // n1
    return pl.pallas_call(
        functools.partial(_k1_kernel, n1=n1),
        grid=(b, n2 // (S2_STEP * S2_BLOCK)),
        in_specs=[
            pl.BlockSpec((1, n1, S2_STEP * S2_BLOCK, D_MODEL), lambda bi, j: (bi, 0, j, 0)),
            _const_spec((1, D_MODEL)),
            _const_spec((D_MODEL, A_WIDTH)),
            _const_spec((N_HEADS, HEAD_DIM, 2 * HEAD_DIM)),
            _const_spec((2 * n1, 2 * n1)),
        ],
        out_specs=pl.BlockSpec((1, N_HEADS, 2, S2_STEP, n1 // 2, S2_BLOCK, HEAD_DIM),
                               lambda bi, j: (bi, 0, 0, j, 0, 0, 0)),
        out_shape=jax.ShapeDtypeStruct((b, N_HEADS, 2, n2 // S2_BLOCK, n1 // 2, S2_BLOCK, HEAD_DIM),
                                       jnp.uint32),
        scratch_shapes=[pltpu.VMEM((N_HEADS, 2, n1 * S2_BLOCK, HEAD_DIM), F32)],
        compiler_params=pltpu.CompilerParams(
            dimension_semantics=("parallel", "parallel"), vmem_limit_bytes=VMEM_LIMIT_V7X),
        name="k1_in_proj_dft1",
    )(x.reshape(b, n1, n2, D_MODEL), g1, w_in_a, w2, f1)


def _k2_kernel(g_ref, tr_ref, ts_ref, f3_ref, y_ref, *, n2):
    for p in range(K1_PAIRS):
        tr = tr_ref[p]
        ts = ts_ref[p]
        pr = []
        pi = []
        for hd in range(N_HEADS):
            gr = pltpu.bitcast(g_ref[0, hd, 0, :, p].reshape(n2, HEAD_DIM), BF16).astype(F32)
            gi = pltpu.bitcast(g_ref[0, hd, 1, :, p].reshape(n2, HEAD_DIM), BF16).astype(F32)
            pr.append(gr * tr + gi * ts)
            pi.append(gi * tr - gr * ts)
        x = jnp.concatenate([jnp.concatenate(pr, axis=1), jnp.concatenate(pi, axis=1)],
                            axis=0).astype(BF16)
        r = _dot(f3_ref[...], x)
        ru = pltpu.bitcast(r.astype(BF16), jnp.uint32)
        for hd in range(N_HEADS):
            out = y_ref.at[0, hd, 0].reshape(n2 * K1_PAIRS, HEAD_DIM)
            out[pl.ds(p, n2, stride=K1_PAIRS), :] = ru[:, hd * HEAD_DIM:(hd + 1) * HEAD_DIM]


def _k2(g, tr, ts, f3):
    b, _, _, nj, half, _, _ = g.shape
    n2 = nj * S2_BLOCK
    return pl.pallas_call(
        functools.partial(_k2_kernel, n2=n2),
        grid=(b, half // K1_PAIRS),
        in_specs=[
            pl.BlockSpec((1, N_HEADS, 2, nj, K1_PAIRS, S2_BLOCK, HEAD_DIM),
                         lambda bi, j: (bi, 0, 0, 0, j, 0, 0)),
            pl.BlockSpec((K1_PAIRS, 2 * n2, HEAD_DIM), lambda bi, j: (j, 0, 0)),
            pl.BlockSpec((K1_PAIRS, 2 * n2, HEAD_DIM), lambda bi, j: (j, 0, 0)),
            _const_spec((2 * n2, 4 * n2)),
        ],
        out_specs=pl.BlockSpec((1, N_HEADS, 1, n2, K1_PAIRS, HEAD_DIM), lambda bi, j: (bi, 0, j, 0, 0, 0)),
        out_shape=jax.ShapeDtypeStruct((b, N_HEADS, half // K1_PAIRS, n2, K1_PAIRS, HEAD_DIM), jnp.uint32),
        compiler_params=pltpu.CompilerParams(
            dimension_semantics=("parallel", "parallel"), vmem_limit_bytes=VMEM_LIMIT_V7X),
        name="k2_dft2",
    )(g, tr, ts, f3)


def _k3_kernel(x_ref, xp_ref, xn_ref, ya_ref, g1_ref, winb_ref, wouta_ref, wpo_ref, g2_ref,
               wg_ref, wu_ref, wd_ref, gf_ref, o_ref, hs_ref, zs_ref, act_ref, d_ref, h2_ref, *, seq_len):
    tm = x_ref.shape[1]
    th = tm // 2
    i = pl.program_id(1)
    last = pl.num_programs(1) - 1
    x = x_ref[0]
    g1 = g1_ref[...]
    g2 = g2_ref[...]
    gf = gf_ref[...]
    half_a = slice(0, th)
    half_b = slice(th, tm)

    ya = jnp.concatenate(
        [pltpu.bitcast(jnp.swapaxes(ya_ref[0, hd], 0, 1).reshape(tm // 2, HEAD_DIM), BF16)
         for hd in range(N_HEADS)], axis=-1)
    x1p = x + _dot(ya, wouta_ref[...])

    hs_ref[:HALO] = _rmsnorm(xp_ref[0], g1).astype(BF16)
    hs_ref[HALO:HALO + tm] = _rmsnorm(x, g1).astype(BF16)
    hs_ref[HALO + tm:] = _rmsnorm(xn_ref[0], g1).astype(BF16)
    zb = _dot(hs_ref[...], winb_ref[...])
    row = jax.lax.broadcasted_iota(jnp.int32, (tm + 2 * HALO, 1), 0)
    inside = ((row >= HALO) | (i > 0)) & ((row < HALO + tm) | (i < last))
    zb = jnp.where(inside, zb, 0.0)
    for gi in range(N_HEADS):
        zs_ref[gi] = zb[:, gi * HEAD_DIM:(gi + 1) * HEAD_DIM]

    def pool_rows(r0):
        n = VPU_SLICE
        t = i * tm + r0 + jax.lax.broadcasted_iota(jnp.int32, (n, 1), 0)
        d = []
        for gi, k in enumerate(POOL_WINDOWS):
            sums = zs_ref[gi, pl.ds(HALO + r0 - k // 2, n), :]
            for j in range(1, k):
                sums = sums + zs_ref[gi, pl.ds(HALO + r0 - k // 2 + j, n), :]
            cnt = jnp.minimum(t + k // 2, seq_len) - jnp.maximum(t - k // 2, 0)
            d.append((sums / cnt.astype(F32) - zs_ref[gi, pl.ds(HALO + r0, n), :]).astype(BF16))
        d_ref[pl.ds(r0, n), :] = jnp.concatenate(d, axis=-1)

    def ffn_chunk(rows, c):
        cols = slice(c * FF_CHUNK, (c + 1) * FF_CHUNK)
        h2 = h2_ref[rows, :]
        gate = _dot(h2, wg_ref[:, cols])
        up = _dot(h2, wu_ref[:, cols])
        act_ref[rows, cols] = (gate * jax.nn.sigmoid(gate) * up).astype(BF16)

    n_chunks = D_FF // FF_CHUNK
    n_slices = th // VPU_SLICE

    for s in range(n_slices):
        pool_rows(s * VPU_SLICE)
    x1a = x1p[half_a] + _dot(d_ref[half_a, :], wpo_ref[...])
    h2_ref[half_a, :] = _rmsnorm(x1a, g2).astype(BF16)

    box = {}

    def proj_b():
        box["x1b"] = x1p[half_b] + _dot(d_ref[half_b, :], wpo_ref[...])

    def norm2_b(s):
        r = slice(s * VPU_SLICE, (s + 1) * VPU_SLICE)
        h2_ref[pl.ds(th + s * VPU_SLICE, VPU_SLICE), :] = _rmsnorm(box["x1b"][r], g2).astype(BF16)

    tasks = [functools.partial(pool_rows, th + s * VPU_SLICE) for s in range(n_slices)]
    tasks += [proj_b] + [functools.partial(norm2_b, s) for s in range(n_slices)]
    per_chunk = -(-len(tasks) // n_chunks)
    for c in range(n_chunks):
        ffn_chunk(half_a, c)
        for task in tasks[c * per_chunk:(c + 1) * per_chunk]:
            task()
    x2a = x1a + _dot(act_ref[half_a, :], wd_ref[...])

    def final_a(s):
        r = slice(s * VPU_SLICE, (s + 1) * VPU_SLICE)
        o_ref[0, r, :] = _rmsnorm(x2a[r], gf)

    tasks = [functools.partial(final_a, s) for s in range(n_slices)]
    per_chunk = -(-len(tasks) // n_chunks)
    for c in range(n_chunks):
        ffn_chunk(half_b, c)
        for task in tasks[c * per_chunk:(c + 1) * per_chunk]:
            task()
    x2b = box["x1b"] + _dot(act_ref[half_b, :], wd_ref[...])
    o_ref[0, half_b, :] = _rmsnorm(x2b, gf)


def _k3(x, ya, g1, w_in_b, w_out_a, w_po, g2, w_gate, w_up, w_down, gf):
    b, s, _ = x.shape
    tm = TOKEN_TILE
    hb = tm // HALO
    n_hb = s // HALO
    return pl.pallas_call(
        functools.partial(_k3_kernel, seq_len=s),
        grid=(b, s // tm),
        in_specs=[
            pl.BlockSpec((1, tm, D_MODEL), lambda bi, i: (bi, i, 0)),
            pl.BlockSpec((1, HALO, D_MODEL), lambda bi, i: (bi, jnp.maximum(i * hb - 1, 0), 0)),
            pl.BlockSpec((1, HALO, D_MODEL), lambda bi, i: (bi, jnp.minimum((i + 1) * hb, n_hb - 1), 0)),
            pl.BlockSpec((1, N_HEADS, DFT_SPLIT // (2 * K1_PAIRS), tm // DFT_SPLIT, K1_PAIRS, HEAD_DIM),
                         lambda bi, i: (bi, 0, 0, i, 0, 0)),
            _const_spec((1, D_MODEL)),
            _const_spec((D_MODEL, B_WIDTH)),
            _const_spec((A_WIDTH, D_MODEL)),
            _const_spec((B_WIDTH, D_MODEL)),
            _const_spec((1, D_MODEL)),
            _const_spec((D_MODEL, D_FF)),
            _const_spec((D_MODEL, D_FF)),
            _const_spec((D_FF, D_MODEL)),
            _const_spec((1, D_MODEL)),
        ],
        out_specs=pl.BlockSpec((1, tm, D_MODEL), lambda bi, i: (bi, i, 0)),
        out_shape=jax.ShapeDtypeStruct((b, s, D_MODEL), F32),
        scratch_shapes=[
            pltpu.VMEM((tm + 2 * HALO, D_MODEL), BF16),
            pltpu.VMEM((N_HEADS, tm + 2 * HALO, HEAD_DIM), F32),
            pltpu.VMEM((tm, D_FF), BF16),
            pltpu.VMEM((tm, B_WIDTH), BF16),
            pltpu.VMEM((tm, D_MODEL), BF16),
        ],
        compiler_params=pltpu.CompilerParams(
            dimension_semantics=("parallel", "parallel"), vmem_limit_bytes=VMEM_LIMIT_V7X),
        name="k3_pool_out_ffn",
    )(x, x, x, ya, g1, w_in_b, w_out_a, w_po, g2, w_gate, w_up, w_down, gf)


def kernel(x_prompt, x_sample, norm1_g, w_in, w_fourier, w_pool, pool_scale, w_out, norm2_g,
           w_gate, w_up, w_down, normf_g):
    assert norm1_g.shape[0] == 1, "single-layer block"
    g1 = norm1_g[0][None]
    g2 = norm2_g[0][None]
    gf = normf_g[None]
    w_in_a = w_in[0, :, :A_WIDTH].astype(BF16)
    w_in_b = w_in[0, :, A_WIDTH:].astype(BF16)
    w_o_a = w_out[0, :A_WIDTH].astype(BF16)
    w_g = w_gate[0].astype(BF16)
    w_u = w_up[0].astype(BF16)
    w_d = w_down[0].astype(BF16)
    w2, w_po = _prep(w_fourier[0], w_pool[0], pool_scale[0][None], w_out[0])

    def trunk(x):
        f1, tr, ts, f3 = _dft_tables(x.shape[1], DFT_SPLIT)
        g = _k1(x, g1, w_in_a, w2, f1)
        ya = _k2(g, tr, ts, f3)
        return _k3(x, ya, g1, w_in_b, w_o_a, w_po, g2, w_g, w_u, w_d, gf)

    return trunk(x_prompt), trunk(x_sample)
```

```python
import functools

import jax
import jax.numpy as jnp
import numpy as np
from jax.experimental import pallas as pl
from jax.experimental.pallas import tpu as pltpu

F32 = jnp.float32
BF16 = jnp.bfloat16

D_MODEL = 1024
N_HEADS = 4
HEAD_DIM = 128
A_WIDTH = N_HEADS * HEAD_DIM
B_WIDTH = N_HEADS * HEAD_DIM
POOL_WINDOWS = (2, 4, 8, 16)
D_FF = 2816
EPS = 1e-6

HALO = 16
TOKEN_TILE = 1024
FF_CHUNK = 256
DFT_SPLIT = 128
S2_BLOCK = 8
MAX_S2_STEP = 2
K12_FIXED_VMEM = 12 * 1024 * 1024
K1_PAIRS = 8
VMEM_LIMIT_V7X = 58 * 1024 * 1024


def _const_spec(shape):
    zeros = (0,) * len(shape)
    return pl.BlockSpec(shape, lambda *_: zeros, pipeline_mode=pl.Buffered(1))


def _rmsnorm(x, g):
    r = jax.lax.rsqrt(jnp.mean(x * x, axis=-1, keepdims=True) + EPS)
    return x * r * g


def _dot(a, b):
    return jnp.dot(a, b, preferred_element_type=F32)


def _split_bf16(a):
    hi = a.astype(BF16)
    lo = (a - hi.astype(F32)).astype(BF16)
    return hi, lo


def _dot_f32(a, b):
    ah, al = _split_bf16(a)
    bh, bl = _split_bf16(b)
    return _dot(ah, bh) + (_dot(ah, bl) + _dot(al, bh))


def _prep_kernel(cs_ref, wf_ref, wp_ref, ps_ref, wout_ref, w2_ref, wpo_ref):
    for h in range(N_HEADS):
        w = wf_ref[h]
        w2_ref[h, :, :HEAD_DIM] = _dot_f32(cs_ref[0], w).astype(BF16)
        w2_ref[h, :, HEAD_DIM:] = (-_dot_f32(cs_ref[1], w)).astype(BF16)
    for g in range(N_HEADS):
        lanes = slice(g * HEAD_DIM, (g + 1) * HEAD_DIM)
        rows = slice(A_WIDTH + g * HEAD_DIM, A_WIDTH + (g + 1) * HEAD_DIM)
        wpo_ref[lanes, :] = _dot_f32(wp_ref[g] * ps_ref[:, lanes], wout_ref[rows, :]).astype(BF16)


def _prep(w_fourier, w_pool, pool_scale, w_out):
    n = np.arange(HEAD_DIM)
    ang = 2.0 * np.pi * np.outer(n, n) / HEAD_DIM
    cs = np.stack([np.cos(ang), np.sin(ang)]) / np.sqrt(HEAD_DIM)
    return pl.pallas_call(
        _prep_kernel,
        out_shape=(jax.ShapeDtypeStruct((N_HEADS, HEAD_DIM, 2 * HEAD_DIM), BF16),
                   jax.ShapeDtypeStruct((B_WIDTH, D_MODEL), BF16)),
        name="prep_weights",
    )(jnp.asarray(cs, F32), w_fourier, w_pool, pool_scale, w_out)


def _stage1_group(x_ref, g1_ref, win_ref, w2_ref, f1_ref, gs_ref, ab_ref, sb, jg, n1):
    rows = n1 * S2_BLOCK
    half = n1 // 2
    x = x_ref[0, :, sb * S2_BLOCK:(sb + 1) * S2_BLOCK, :].reshape(rows, D_MODEL)
    h = _rmsnorm(x, g1_ref[...]).astype(BF16)
    za = _dot(h, win_ref[...]).astype(BF16)
    for hd in range(N_HEADS):
        ab = _dot(za[:, hd * HEAD_DIM:(hd + 1) * HEAD_DIM], w2_ref[hd])
        buf = ab_ref.at[hd % 2]
        buf[0] = ab[:, :HEAD_DIM]
        buf[1] = ab[:, HEAD_DIM:]
        xx = jnp.concatenate(
            [jnp.concatenate([buf.at[ri][pl.ds(j, n1, stride=S2_BLOCK), :]
                              for j in range(S2_BLOCK)], axis=1) for ri in range(2)],
            axis=0).astype(BF16)
        g = _dot(f1_ref[...], xx)
        gp = pltpu.bitcast(g.astype(BF16), jnp.uint32)
        for ri in range(2):
            out = gs_ref.at[hd, ri, jg].reshape(half * S2_BLOCK, HEAD_DIM)
            for j in range(S2_BLOCK):
                out[pl.ds(j, half, stride=S2_BLOCK), :] = (
                    gp[ri * half:(ri + 1) * half, j * HEAD_DIM:(j + 1) * HEAD_DIM])


def _stage2_pairs(gs_ref, tr_ref, ts_ref, f3_ref, y_ref, jb, n2):
    for p in range(K1_PAIRS):
        tr = tr_ref[p]
        ts = ts_ref[p]
        pr = []
        pi = []
        for hd in range(N_HEADS):
            wr = gs_ref[hd, 0, :, jb * K1_PAIRS + p].reshape(n2, HEAD_DIM)
            wi = gs_ref[hd, 1, :, jb * K1_PAIRS + p].reshape(n2, HEAD_DIM)
            gr = pltpu.bitcast(wr, BF16).astype(F32)
            gi = pltpu.bitcast(wi, BF16).astype(F32)
            pr.append(gr * tr + gi * ts)
            pi.append(gi * tr - gr * ts)
        x = jnp.concatenate([jnp.concatenate(pr, axis=1), jnp.concatenate(pi, axis=1)],
                            axis=0).astype(BF16)
        r = _dot(f3_ref[...], x)
        ru = pltpu.bitcast(r.astype(BF16), jnp.uint32)
        for hd in range(N_HEADS):
            out = y_ref.at[0, hd, 0].reshape(n2 * K1_PAIRS, HEAD_DIM)
            out[pl.ds(p, n2, stride=K1_PAIRS), :] = ru[:, hd * HEAD_DIM:(hd + 1) * HEAD_DIM]


def _k12_kernel(x_ref, g1_ref, win_ref, w2_ref, f1_ref, tr_ref, ts_ref, f3_ref, y_ref, ab_ref, gs_ref,
                *, n1, n2, s2_step):
    j = pl.program_id(1)
    n_s1 = n2 // (s2_step * S2_BLOCK)

    @pl.when(j < n_s1)
    def _():
        for sb in range(s2_step):
            _stage1_group(x_ref, g1_ref, win_ref, w2_ref, f1_ref, gs_ref, ab_ref, sb, j * s2_step + sb, n1)

    @pl.when(j >= n_s1)
    def _():
        _stage2_pairs(gs_ref, tr_ref, ts_ref, f3_ref, y_ref, j - n_s1, n2)


def _dft_tables(s, n1):
    n2 = s // n1
    k1 = np.arange(n1)
    a1 = 2.0 * np.pi * np.outer(k1, k1) / n1
    c1, s1 = np.cos(a1), np.sin(a1)
    f1 = np.block([[c1, s1], [-s1, c1]]) / np.sqrt(n1)
    at = 2.0 * np.pi * np.outer(k1, np.arange(n2)) / s
    k2 = np.arange(n2)
    a2 = 2.0 * np.pi * np.outer(k2, k2) / n2
    f3 = np.concatenate([np.cos(a2), np.sin(a2)], axis=1) / np.sqrt(n2)
    f3p = np.einsum("krs,ef->kersf", f3.reshape(n2, 2, n2), np.eye(2)).reshape(2 * n2, 4 * n2)
    lanes = np.ones((1, 1, HEAD_DIM))
    atp = at.reshape(n1 // 2, 2, n2).transpose(0, 2, 1).reshape(n1 // 2, 2 * n2)
    tr = jnp.asarray(np.cos(atp)[:, :, None] * lanes, F32)
    ts = jnp.asarray(np.sin(atp)[:, :, None] * lanes, F32)
    return jnp.asarray(f1, F32).astype(BF16), tr, ts, jnp.asarray(f3p, F32).astype(BF16)


def _s2_step(s):
    spectrum = s * A_WIDTH * 4
    x_group = 2 * DFT_SPLIT * S2_BLOCK * D_MODEL * 4
    return max(1, min(MAX_S2_STEP, (VMEM_LIMIT_V7X - spectrum - K12_FIXED_VMEM) // x_group))


def _k12(x, g1, w_in_a, w2):
    b, s, _ = x.shape
    n1 = DFT_SPLIT
    n2 = s // n1
    half = n1 // 2
    nj = n2 // S2_BLOCK
    f1, tr, ts, f3 = _dft_tables(s, n1)
    s2_step = _s2_step(s)
    n_s1 = n2 // (s2_step * S2_BLOCK)
    n_s2 = half // K1_PAIRS
    return pl.pallas_call(
        functools.partial(_k12_kernel, n1=n1, n2=n2, s2_step=s2_step),
        grid=(b, n_s1 + n_s2),
        in_specs=[
            pl.BlockSpec((1, n1, s2_step * S2_BLOCK, D_MODEL),
                         lambda bi, j: (bi, 0, jnp.minimum(j, n_s1 - 1), 0)),
            _const_spec((1, D_MODEL)),
            _const_spec((D_MODEL, A_WIDTH)),
            _const_spec((N_HEADS, HEAD_DIM, 2 * HEAD_DIM)),
            _const_spec((2 * n1, 2 * n1)),
            pl.BlockSpec((K1_PAIRS, 2 * n2, HEAD_DIM), lambda bi, j: (jnp.maximum(j - n_s1, 0), 0, 0)),
            pl.BlockSpec((K1_PAIRS, 2 * n2, HEAD_DIM), lambda bi, j: (jnp.maximum(j - n_s1, 0), 0, 0)),
            _const_spec((2 * n2, 4 * n2)),
        ],
        out_specs=pl.BlockSpec((1, N_HEADS, 1, n2, K1_PAIRS, HEAD_DIM),
                               lambda bi, j: (bi, 0, jnp.maximum(j - n_s1, 0), 0, 0, 0)),
        out_shape=jax.ShapeDtypeStruct((b, N_HEADS, n_s2, n2, K1_PAIRS, HEAD_DIM), jnp.uint32),
        scratch_shapes=[
            pltpu.VMEM((2, 2, n1 * S2_BLOCK, HEAD_DIM), F32),
            pltpu.VMEM((N_HEADS, 2, nj, half, S2_BLOCK, HEAD_DIM), jnp.uint32),
        ],
        compiler_params=pltpu.CompilerParams(
            dimension_semantics=("parallel", "arbitrary"), vmem_limit_bytes=VMEM_LIMIT_V7X),
        name="k12_in_proj_dft",
    )(x.reshape(b, n1, n2, D_MODEL), g1, w_in_a, w2, f1, tr, ts, f3)


def _k3_kernel(x_ref, xp_ref, xn_ref, ya_ref, g1_ref, winb_ref, wouta_ref, wpo_ref, g2_ref,
               wg_ref, wu_ref, wd_ref, gf_ref, o_ref, hs_ref, zs_ref, act_ref, *, seq_len):
    tm = x_ref.shape[1]
    i = pl.program_id(1)
    last = pl.num_programs(1) - 1
    x = x_ref[0]

    ya = jnp.concatenate(
        [pltpu.bitcast(jnp.swapaxes(ya_ref[0, hd], 0, 1).reshape(tm // 2, HEAD_DIM), BF16)
         for hd in range(N_HEADS)], axis=-1)
    x1 = x + _dot(ya, wouta_ref[...])

    g1 = g1_ref[...]
    hs_ref[:HALO] = _rmsnorm(xp_ref[0], g1).astype(BF16)
    hs_ref[HALO:HALO + tm] = _rmsnorm(x, g1).astype(BF16)
    hs_ref[HALO + tm:] = _rmsnorm(xn_ref[0], g1).astype(BF16)
    zb = _dot(hs_ref[...], winb_ref[...])
    row = jax.lax.broadcasted_iota(jnp.int32, (tm + 2 * HALO, 1), 0)
    inside = ((row >= HALO) | (i > 0)) & ((row < HALO + tm) | (i < last))
    zb = jnp.where(inside, zb, 0.0)
    for gi in range(N_HEADS):
        zs_ref[gi] = zb[:, gi * HEAD_DIM:(gi + 1) * HEAD_DIM]

    t = i * tm + jax.lax.broadcasted_iota(jnp.int32, (tm, 1), 0)
    d = []
    for gi, k in enumerate(POOL_WINDOWS):
        sums = zs_ref[gi, pl.ds(HALO - k // 2, tm), :]
        for j in range(1, k):
            sums = sums + zs_ref[gi, pl.ds(HALO - k // 2 + j, tm), :]
        cnt = jnp.minimum(t + k // 2, seq_len) - jnp.maximum(t - k // 2, 0)
        d.append((sums / cnt.astype(F32) - zs_ref[gi, pl.ds(HALO, tm), :]).astype(BF16))

    x1 = x1 + _dot(jnp.concatenate(d, axis=-1), wpo_ref[...])
    h2 = _rmsnorm(x1, g2_ref[...]).astype(BF16)
    for c in range(D_FF // FF_CHUNK):
        cols = slice(c * FF_CHUNK, (c + 1) * FF_CHUNK)
        gate = _dot(h2, wg_ref[:, cols])
        up = _dot(h2, wu_ref[:, cols])
        act_ref[:, cols] = (gate * jax.nn.sigmoid(gate) * up).astype(BF16)
    x2 = x1 + _dot(act_ref[...], wd_ref[...])
    o_ref[0] = _rmsnorm(x2, gf_ref[...])


def _k3(x, ya, g1, w_in_b, w_out_a, w_po, g2, w_gate, w_up, w_down, gf):
    b, s, _ = x.shape
    tm = TOKEN_TILE
    hb = tm // HALO
    n_hb = s // HALO
    return pl.pallas_call(
        functools.partial(_k3_kernel, seq_len=s),
        grid=(b, s // tm),
        in_specs=[
            pl.BlockSpec((1, tm, D_MODEL), lambda bi, i: (bi, i, 0)),
            pl.BlockSpec((1, HALO, D_MODEL), lambda bi, i: (bi, jnp.maximum(i * hb - 1, 0), 0)),
            pl.BlockSpec((1, HALO, D_MODEL), lambda bi, i: (bi, jnp.minimum((i + 1) * hb, n_hb - 1), 0)),
            pl.BlockSpec((1, N_HEADS, DFT_SPLIT // (2 * K1_PAIRS), tm // DFT_SPLIT, K1_PAIRS, HEAD_DIM),
                         lambda bi, i: (bi, 0, 0, i, 0, 0)),
            _const_spec((1, D_MODEL)),
            _const_spec((D_MODEL, B_WIDTH)),
            _const_spec((A_WIDTH, D_MODEL)),
            _const_spec((B_WIDTH, D_MODEL)),
            _const_spec((1, D_MODEL)),
            _const_spec((D_MODEL, D_FF)),
            _const_spec((D_MODEL, D_FF)),
            _const_spec((D_FF, D_MODEL)),
            _const_spec((1, D_MODEL)),
        ],
        out_specs=pl.BlockSpec((1, tm, D_MODEL), lambda bi, i: (bi, i, 0)),
        out_shape=jax.ShapeDtypeStruct((b, s, D_MODEL), F32),
        scratch_shapes=[
            pltpu.VMEM((tm + 2 * HALO, D_MODEL), BF16),
            pltpu.VMEM((N_HEADS, tm + 2 * HALO, HEAD_DIM), F32),
            pltpu.VMEM((tm, D_FF), BF16),
        ],
        compiler_params=pltpu.CompilerParams(
            dimension_semantics=("parallel", "parallel"), vmem_limit_bytes=VMEM_LIMIT_V7X),
        name="k3_pool_out_ffn",
    )(x, x, x, ya, g1, w_in_b, w_out_a, w_po, g2, w_gate, w_up, w_down, gf)


def kernel(x_prompt, x_sample, norm1_g, w_in, w_fourier, w_pool, pool_scale, w_out, norm2_g,
           w_gate, w_up, w_down, normf_g):
    assert norm1_g.shape[0] == 1, "single-layer block"
    g1 = norm1_g[0][None]
    g2 = norm2_g[0][None]
    gf = normf_g[None]
    w_in_a = w_in[0, :, :A_WIDTH].astype(BF16)
    w_in_b = w_in[0, :, A_WIDTH:].astype(BF16)
    w_o_a = w_out[0, :A_WIDTH].astype(BF16)
    w_g = w_gate[0].astype(BF16)
    w_u = w_up[0].astype(BF16)
    w_d = w_down[0].astype(BF16)
    w2, w_po = _prep(w_fourier[0], w_pool[0], pool_scale[0][None], w_out[0])

    def trunk(x):
        ya = _k12(x, g1, w_in_a, w2)
        return _k3(x, ya, g1, w_in_b, w_o_a, w_po, g2, w_g, w_u, w_d, gf)

    return trunk(x_prompt), trunk(x_sample)
```

```python
import functools

import jax
import jax.numpy as jnp
import numpy as np
from jax.experimental import pallas as pl
from jax.experimental.pallas import tpu as pltpu

F32 = jnp.float32
BF16 = jnp.bfloat16

D_MODEL = 1024
N_HEADS = 4
HEAD_DIM = 128
A_WIDTH = N_HEADS * HEAD_DIM
B_WIDTH = N_HEADS * HEAD_DIM
POOL_WINDOWS = (2, 4, 8, 16)
D_FF = 2816
EPS = 1e-6

HALO = 16
TOKEN_TILE = 1024
FF_CHUNK = 256
DFT_SPLIT = 128
S2_BLOCK = 8
MAX_S2_STEP = 2
K12_FIXED_VMEM = 14 * 1024 * 1024
K1_PAIRS = 8
PAIR_BLOCKS = 2
VMEM_LIMIT_V7X = 58 * 1024 * 1024


def _const_spec(shape):
    zeros = (0,) * len(shape)
    return pl.BlockSpec(shape, lambda *_: zeros, pipeline_mode=pl.Buffered(1))


def _rmsnorm(x, g):
    r = jax.lax.rsqrt(jnp.mean(x * x, axis=-1, keepdims=True) + EPS)
    return x * r * g


def _dot(a, b):
    return jnp.dot(a, b, preferred_element_type=F32)


def _split_bf16(a):
    hi = a.astype(BF16)
    lo = (a - hi.astype(F32)).astype(BF16)
    return hi, lo


def _dot_f32(a, b):
    ah, al = _split_bf16(a)
    bh, bl = _split_bf16(b)
    return _dot(ah, bh) + (_dot(ah, bl) + _dot(al, bh))


def _prep_kernel(cs_ref, wf_ref, wp_ref, ps_ref, wout_ref, w2_ref, wpo_ref):
    for h in range(N_HEADS):
        w = wf_ref[h]
        w2_ref[h, :, :HEAD_DIM] = _dot_f32(cs_ref[0], w).astype(BF16)
        w2_ref[h, :, HEAD_DIM:] = (-_dot_f32(cs_ref[1], w)).astype(BF16)
    for g in range(N_HEADS):
        lanes = slice(g * HEAD_DIM, (g + 1) * HEAD_DIM)
        rows = slice(A_WIDTH + g * HEAD_DIM, A_WIDTH + (g + 1) * HEAD_DIM)
        wpo_ref[lanes, :] = _dot_f32(wp_ref[g] * ps_ref[:, lanes], wout_ref[rows, :]).astype(BF16)


def _prep(w_fourier, w_pool, pool_scale, w_out):
    n = np.arange(HEAD_DIM)
    ang = 2.0 * np.pi * np.outer(n, n) / HEAD_DIM
    cs = np.stack([np.cos(ang), np.sin(ang)]) / np.sqrt(HEAD_DIM)
    return pl.pallas_call(
        _prep_kernel,
        out_shape=(jax.ShapeDtypeStruct((N_HEADS, HEAD_DIM, 2 * HEAD_DIM), BF16),
                   jax.ShapeDtypeStruct((B_WIDTH, D_MODEL), BF16)),
        name="prep_weights",
    )(jnp.asarray(cs, F32), w_fourier, w_pool, pool_scale, w_out)


def _stage1_group(x_ref, g1_ref, win_ref, w2_ref, f1_ref, gs_ref, ab_ref, sb, jg, n1):
    rows = n1 * S2_BLOCK
    half = n1 // 2
    x = x_ref[0, :, sb * S2_BLOCK:(sb + 1) * S2_BLOCK, :].reshape(rows, D_MODEL)
    h = _rmsnorm(x, g1_ref[...]).astype(BF16)
    za = _dot(h, win_ref[...]).astype(BF16)
    for hd in range(N_HEADS):
        ab = _dot(za[:, hd * HEAD_DIM:(hd + 1) * HEAD_DIM], w2_ref[hd])
        buf = ab_ref.at[hd % 2]
        buf[0] = ab[:, :HEAD_DIM]
        buf[1] = ab[:, HEAD_DIM:]
        xx = jnp.concatenate(
            [jnp.concatenate([buf.at[ri][pl.ds(j, n1, stride=S2_BLOCK), :]
                              for j in range(S2_BLOCK)], axis=1) for ri in range(2)],
            axis=0).astype(BF16)
        g = _dot(f1_ref[...], xx)
        gp = pltpu.bitcast(g.astype(BF16), jnp.uint32)
        for ri in range(2):
            out = gs_ref.at[hd, ri, jg].reshape(half * S2_BLOCK, HEAD_DIM)
            for j in range(S2_BLOCK):
                out[pl.ds(j, half, stride=S2_BLOCK), :] = (
                    gp[ri * half:(ri + 1) * half, j * HEAD_DIM:(j + 1) * HEAD_DIM])


def _stage2_pairs(gs_ref, t0_ref, tw_ref, f3_ref, y_ref, jb, n2):
    pairs = PAIR_BLOCKS * K1_PAIRS
    tr = t0_ref[0, 0]
    ts = t0_ref[0, 1]
    for p in range(pairs):
        if p:
            tr, ts = tr * tw_ref[0] - ts * tw_ref[1], tr * tw_ref[1] + ts * tw_ref[0]
        pr = []
        pi = []
        for hd in range(N_HEADS):
            wr = gs_ref[hd, 0, :, jb * pairs + p].reshape(n2, HEAD_DIM)
            wi = gs_ref[hd, 1, :, jb * pairs + p].reshape(n2, HEAD_DIM)
            gr = pltpu.bitcast(wr, BF16).astype(F32)
            gi = pltpu.bitcast(wi, BF16).astype(F32)
            pr.append(gr * tr + gi * ts)
            pi.append(gi * tr - gr * ts)
        x = jnp.concatenate([jnp.concatenate(pr, axis=1), jnp.concatenate(pi, axis=1)],
                            axis=0).astype(BF16)
        r = _dot(f3_ref[...], x)
        ru = pltpu.bitcast(r.astype(BF16), jnp.uint32)
        for hd in range(N_HEADS):
            out = y_ref.at[0, hd, p // K1_PAIRS].reshape(n2 * K1_PAIRS, HEAD_DIM)
            out[pl.ds(p % K1_PAIRS, n2, stride=K1_PAIRS), :] = ru[:, hd * HEAD_DIM:(hd + 1) * HEAD_DIM]


def _k12_kernel(x_ref, g1_ref, win_ref, w2_ref, f1_ref, t0_ref, tw_ref, f3_ref, y_ref, ab_ref, gs_ref,
                *, n1, n2, s2_step):
    j = pl.program_id(1)
    n_s1 = n2 // (s2_step * S2_BLOCK)

    @pl.when(j < n_s1)
    def _():
        for sb in range(s2_step):
            _stage1_group(x_ref, g1_ref, win_ref, w2_ref, f1_ref, gs_ref, ab_ref, sb, j * s2_step + sb, n1)

    @pl.when(j >= n_s1)
    def _():
        _stage2_pairs(gs_ref, t0_ref, tw_ref, f3_ref, y_ref, j - n_s1, n2)


def _dft_tables(s, n1):
    n2 = s // n1
    k1 = np.arange(n1)
    a1 = 2.0 * np.pi * np.outer(k1, k1) / n1
    c1, s1 = np.cos(a1), np.sin(a1)
    f1 = np.block([[c1, s1], [-s1, c1]]) / np.sqrt(n1)
    at = 2.0 * np.pi * np.outer(k1, np.arange(n2)) / s
    k2 = np.arange(n2)
    a2 = 2.0 * np.pi * np.outer(k2, k2) / n2
    f3 = np.concatenate([np.cos(a2), np.sin(a2)], axis=1) / np.sqrt(n2)
    f3p = np.einsum("krs,ef->kersf", f3.reshape(n2, 2, n2), np.eye(2)).reshape(2 * n2, 4 * n2)
    lanes = np.ones((1, 1, HEAD_DIM))
    atp = at.reshape(n1 // 2, 2, n2).transpose(0, 2, 1).reshape(n1 // 2, 2 * n2)
    at0 = atp[::PAIR_BLOCKS * K1_PAIRS]
    t0 = np.stack([np.cos(at0), np.sin(at0)], axis=1)[:, :, :, None] * lanes
    a_next = np.repeat(2.0 * np.pi * 2.0 * np.arange(n2) / s, 2)
    tw = np.stack([np.cos(a_next), np.sin(a_next)])[:, :, None] * lanes
    return (jnp.asarray(f1, F32).astype(BF16), jnp.asarray(t0, F32), jnp.asarray(tw, F32),
            jnp.asarray(f3p, F32).astype(BF16))


def _s2_step(s):
    spectrum = s * A_WIDTH * 4
    x_group = 2 * DFT_SPLIT * S2_BLOCK * D_MODEL * 4
    return max(1, min(MAX_S2_STEP, (VMEM_LIMIT_V7X - spectrum - K12_FIXED_VMEM) // x_group))


def _k12(x, g1, w_in_a, w2):
    b, s, _ = x.shape
    n1 = DFT_SPLIT
    n2 = s // n1
    half = n1 // 2
    nj = n2 // S2_BLOCK
    f1, t0, tw, f3 = _dft_tables(s, n1)
    s2_step = _s2_step(s)
    n_s1 = n2 // (s2_step * S2_BLOCK)
    n_s2 = half // (PAIR_BLOCKS * K1_PAIRS)
    return pl.pallas_call(
        functools.partial(_k12_kernel, n1=n1, n2=n2, s2_step=s2_step),
        grid=(b, n_s1 + n_s2),
        in_specs=[
            pl.BlockSpec((1, n1, s2_step * S2_BLOCK, D_MODEL),
                         lambda bi, j: (bi, 0, jnp.minimum(j, n_s1 - 1), 0)),
            _const_spec((1, D_MODEL)),
            _const_spec((D_MODEL, A_WIDTH)),
            _const_spec((N_HEADS, HEAD_DIM, 2 * HEAD_DIM)),
            _const_spec((2 * n1, 2 * n1)),
            pl.BlockSpec((1, 2, 2 * n2, HEAD_DIM), lambda bi, j: (jnp.maximum(j - n_s1, 0), 0, 0, 0)),
            _const_spec((2, 2 * n2, HEAD_DIM)),
            _const_spec((2 * n2, 4 * n2)),
        ],
        out_specs=pl.BlockSpec((1, N_HEADS, PAIR_BLOCKS, n2, K1_PAIRS, HEAD_DIM),
                               lambda bi, j: (bi, 0, jnp.maximum(j - n_s1, 0), 0, 0, 0)),
        out_shape=jax.ShapeDtypeStruct((b, N_HEADS, half // K1_PAIRS, n2, K1_PAIRS, HEAD_DIM), jnp.uint32),
        scratch_shapes=[
            pltpu.VMEM((2, 2, n1 * S2_BLOCK, HEAD_DIM), F32),
            pltpu.VMEM((N_HEADS, 2, nj, half, S2_BLOCK, HEAD_DIM), jnp.uint32),
        ],
        compiler_params=pltpu.CompilerParams(
            dimension_semantics=("parallel", "arbitrary"), vmem_limit_bytes=VMEM_LIMIT_V7X),
        name="k12_in_proj_dft",
    )(x.reshape(b, n1, n2, D_MODEL), g1, w_in_a, w2, f1, t0, tw, f3)


def _k3_kernel(x_ref, xp_ref, xn_ref, ya_ref, g1_ref, winb_ref, wouta_ref, wpo_ref, g2_ref,
               wg_ref, wu_ref, wd_ref, gf_ref, o_ref, hs_ref, zs_ref, act_ref, *, seq_len):
    tm = x_ref.shape[1]
    i = pl.program_id(1)
    last = pl.num_programs(1) - 1
    x = x_ref[0]

    ya = jnp.concatenate(
        [pltpu.bitcast(jnp.swapaxes(ya_ref[0, hd], 0, 1).reshape(tm // 2, HEAD_DIM), BF16)
         for hd in range(N_HEADS)], axis=-1)
    x1 = x + _dot(ya, wouta_ref[...])

    g1 = g1_ref[...]
    hs_ref[:HALO] = _rmsnorm(xp_ref[0], g1).astype(BF16)
    hs_ref[HALO:HALO + tm] = _rmsnorm(x, g1).astype(BF16)
    hs_ref[HALO + tm:] = _rmsnorm(xn_ref[0], g1).astype(BF16)
    zb = _dot(hs_ref[...], winb_ref[...])
    row = jax.lax.broadcasted_iota(jnp.int32, (tm + 2 * HALO, 1), 0)
    inside = ((row >= HALO) | (i > 0)) & ((row < HALO + tm) | (i < last))
    zb = jnp.where(inside, zb, 0.0)
    for gi in range(N_HEADS):
        zs_ref[gi] = zb[:, gi * HEAD_DIM:(gi + 1) * HEAD_DIM]

    t = i * tm + jax.lax.broadcasted_iota(jnp.int32, (tm, 1), 0)
    d = []
    for gi, k in enumerate(POOL_WINDOWS):
        sums = zs_ref[gi, pl.ds(HALO - k // 2, tm), :]
        for j in range(1, k):
            sums = sums + zs_ref[gi, pl.ds(HALO - k // 2 + j, tm), :]
        cnt = jnp.minimum(t + k // 2, seq_len) - jnp.maximum(t - k // 2, 0)
        d.append((sums / cnt.astype(F32) - zs_ref[gi, pl.ds(HALO, tm), :]).astype(BF16))

    x1 = x1 + _dot(jnp.concatenate(d, axis=-1), wpo_ref[...])
    h2 = _rmsnorm(x1, g2_ref[...]).astype(BF16)
    for c in range(D_FF // FF_CHUNK):
        cols = slice(c * FF_CHUNK, (c + 1) * FF_CHUNK)
        gate = _dot(h2, wg_ref[:, cols])
        up = _dot(h2, wu_ref[:, cols])
        act_ref[:, cols] = (gate * jax.nn.sigmoid(gate) * up).astype(BF16)
    x2 = x1 + _dot(act_ref[...], wd_ref[...])
    o_ref[0] = _rmsnorm(x2, gf_ref[...])


def _k3(x, ya, g1, w_in_b, w_out_a, w_po, g2, w_gate, w_up, w_down, gf):
    b, s, _ = x.shape
    tm = TOKEN_TILE
    hb = tm // HALO
    n_hb = s // HALO
    return pl.pallas_call(
        functools.partial(_k3_kernel, seq_len=s),
        grid=(b, s // tm),
        in_specs=[
            pl.BlockSpec((1, tm, D_MODEL), lambda bi, i: (bi, i, 0)),
            pl.BlockSpec((1, HALO, D_MODEL), lambda bi, i: (bi, jnp.maximum(i * hb - 1, 0), 0)),
            pl.BlockSpec((1, HALO, D_MODEL), lambda bi, i: (bi, jnp.minimum((i + 1) * hb, n_hb - 1), 0)),
            pl.BlockSpec((1, N_HEADS, DFT_SPLIT // (2 * K1_PAIRS), tm // DFT_SPLIT, K1_PAIRS, HEAD_DIM),
                         lambda bi, i: (bi, 0, 0, i, 0, 0)),
            _const_spec((1, D_MODEL)),
            _const_spec((D_MODEL, B_WIDTH)),
            _const_spec((A_WIDTH, D_MODEL)),
            _const_spec((B_WIDTH, D_MODEL)),
            _const_spec((1, D_MODEL)),
            _const_spec((D_MODEL, D_FF)),
            _const_spec((D_MODEL, D_FF)),
            _const_spec((D_FF, D_MODEL)),
            _const_spec((1, D_MODEL)),
        ],
        out_specs=pl.BlockSpec((1, tm, D_MODEL), lambda bi, i: (bi, i, 0)),
        out_shape=jax.ShapeDtypeStruct((b, s, D_MODEL), F32),
        scratch_shapes=[
            pltpu.VMEM((tm + 2 * HALO, D_MODEL), BF16),
            pltpu.VMEM((N_HEADS, tm + 2 * HALO, HEAD_DIM), F32),
            pltpu.VMEM((tm, D_FF), BF16),
        ],
        compiler_params=pltpu.CompilerParams(
            dimension_semantics=("parallel", "parallel"), vmem_limit_bytes=VMEM_LIMIT_V7X),
        name="k3_pool_out_ffn",
    )(x, x, x, ya, g1, w_in_b, w_out_a, w_po, g2, w_gate, w_up, w_down, gf)


def kernel(x_prompt, x_sample, norm1_g, w_in, w_fourier, w_pool, pool_scale, w_out, norm2_g,
           w_gate, w_up, w_down, normf_g):
    assert norm1_g.shape[0] == 1, "single-layer block"
    g1 = norm1_g[0][None]
    g2 = norm2_g[0][None]
    gf = normf_g[None]
    w_in_a = w_in[0, :, :A_WIDTH].astype(BF16)
    w_in_b = w_in[0, :, A_WIDTH:].astype(BF16)
    w_o_a = w_out[0, :A_WIDTH].astype(BF16)
    w_g = w_gate[0].astype(BF16)
    w_u = w_up[0].astype(BF16)
    w_d = w_down[0].astype(BF16)
    w2, w_po = _prep(w_fourier[0], w_pool[0], pool_scale[0][None], w_out[0])

    def trunk(x):
        ya = _k12(x, g1, w_in_a, w2)
        return _k3(x, ya, g1, w_in_b, w_o_a, w_po, g2, w_g, w_u, w_d, gf)

    return trunk(x_prompt), trunk(x_sample)
```

```python
import functools

import jax
import jax.numpy as jnp
import numpy as np
from jax.experimental import pallas as pl
from jax.experimental.pallas import tpu as pltpu

F32 = jnp.float32
BF16 = jnp.bfloat16

D_MODEL = 1024
N_HEADS = 4
HEAD_DIM = 128
A_WIDTH = N_HEADS * HEAD_DIM
B_WIDTH = N_HEADS * HEAD_DIM
POOL_WINDOWS = (2, 4, 8, 16)
D_FF = 2816
EPS = 1e-6

HALO = 16
TOKEN_TILE = 1024
FF_CHUNK = 256
VPU_SLICE = 64
DFT_SPLIT = 128
S2_BLOCK = 8
MAX_S2_STEP = 2
K12_FIXED_VMEM = 14 * 1024 * 1024
K1_PAIRS = 8
PAIR_BLOCKS = 2
VMEM_LIMIT_V7X = 58 * 1024 * 1024


def _const_spec(shape):
    zeros = (0,) * len(shape)
    return pl.BlockSpec(shape, lambda *_: zeros, pipeline_mode=pl.Buffered(1))


def _rmsnorm(x, g):
    r = jax.lax.rsqrt(jnp.mean(x * x, axis=-1, keepdims=True) + EPS)
    return x * r * g


def _dot(a, b):
    return jnp.dot(a, b, preferred_element_type=F32)


def _split_bf16(a):
    hi = a.astype(BF16)
    lo = (a - hi.astype(F32)).astype(BF16)
    return hi, lo


def _dot_f32(a, b):
    ah, al = _split_bf16(a)
    bh, bl = _split_bf16(b)
    return _dot(ah, bh) + (_dot(ah, bl) + _dot(al, bh))


def _prep_kernel(cs_ref, wf_ref, wp_ref, ps_ref, wout_ref, w2_ref, wpo_ref):
    for h in range(N_HEADS):
        w = wf_ref[h]
        w2_ref[h, :, :HEAD_DIM] = _dot_f32(cs_ref[0], w).astype(BF16)
        w2_ref[h, :, HEAD_DIM:] = (-_dot_f32(cs_ref[1], w)).astype(BF16)
    for g in range(N_HEADS):
        lanes = slice(g * HEAD_DIM, (g + 1) * HEAD_DIM)
        rows = slice(A_WIDTH + g * HEAD_DIM, A_WIDTH + (g + 1) * HEAD_DIM)
        wpo_ref[lanes, :] = _dot_f32(wp_ref[g] * ps_ref[:, lanes], wout_ref[rows, :]).astype(BF16)


def _prep(w_fourier, w_pool, pool_scale, w_out):
    n = np.arange(HEAD_DIM)
    ang = 2.0 * np.pi * np.outer(n, n) / HEAD_DIM
    cs = np.stack([np.cos(ang), np.sin(ang)]) / np.sqrt(HEAD_DIM)
    return pl.pallas_call(
        _prep_kernel,
        out_shape=(jax.ShapeDtypeStruct((N_HEADS, HEAD_DIM, 2 * HEAD_DIM), BF16),
                   jax.ShapeDtypeStruct((B_WIDTH, D_MODEL), BF16)),
        name="prep_weights",
    )(jnp.asarray(cs, F32), w_fourier, w_pool, pool_scale, w_out)


def _stage1_group(x_ref, g1_ref, win_ref, w2_ref, f1_ref, gs_ref, ab_ref, sb, jg, n1):
    rows = n1 * S2_BLOCK
    half = n1 // 2
    x = x_ref[0, :, sb * S2_BLOCK:(sb + 1) * S2_BLOCK, :].reshape(rows, D_MODEL)
    h = _rmsnorm(x, g1_ref[...]).astype(BF16)
    za = _dot(h, win_ref[...]).astype(BF16)
    for hd in range(N_HEADS):
        ab = _dot(za[:, hd * HEAD_DIM:(hd + 1) * HEAD_DIM], w2_ref[hd])
        buf = ab_ref.at[hd % 2]
        buf[0] = ab[:, :HEAD_DIM]
        buf[1] = ab[:, HEAD_DIM:]
        xx = jnp.concatenate(
            [jnp.concatenate([buf.at[ri][pl.ds(j, n1, stride=S2_BLOCK), :]
                              for j in range(S2_BLOCK)], axis=1) for ri in range(2)],
            axis=0).astype(BF16)
        g = _dot(f1_ref[...], xx)
        gp = pltpu.bitcast(g.astype(BF16), jnp.uint32)
        for ri in range(2):
            out = gs_ref.at[hd, ri, jg].reshape(half * S2_BLOCK, HEAD_DIM)
            for j in range(S2_BLOCK):
                out[pl.ds(j, half, stride=S2_BLOCK), :] = (
                    gp[ri * half:(ri + 1) * half, j * HEAD_DIM:(j + 1) * HEAD_DIM])


def _stage2_pairs(gs_ref, t0_ref, tw_ref, f3_ref, y_ref, jb, n2):
    pairs = PAIR_BLOCKS * K1_PAIRS
    tr = t0_ref[0, 0]
    ts = t0_ref[0, 1]
    for p in range(pairs):
        if p:
            tr, ts = tr * tw_ref[0] - ts * tw_ref[1], tr * tw_ref[1] + ts * tw_ref[0]
        pr = []
        pi = []
        for hd in range(N_HEADS):
            wr = gs_ref[hd, 0, :, jb * pairs + p].reshape(n2, HEAD_DIM)
            wi = gs_ref[hd, 1, :, jb * pairs + p].reshape(n2, HEAD_DIM)
            gr = pltpu.bitcast(wr, BF16).astype(F32)
            gi = pltpu.bitcast(wi, BF16).astype(F32)
            pr.append(gr * tr + gi * ts)
            pi.append(gi * tr - gr * ts)
        x = jnp.concatenate([jnp.concatenate(pr, axis=1), jnp.concatenate(pi, axis=1)],
                            axis=0).astype(BF16)
        r = _dot(f3_ref[...], x)
        ru = pltpu.bitcast(r.astype(BF16), jnp.uint32)
        for hd in range(N_HEADS):
            out = y_ref.at[0, hd, p // K1_PAIRS].reshape(n2 * K1_PAIRS, HEAD_DIM)
            out[pl.ds(p % K1_PAIRS, n2, stride=K1_PAIRS), :] = ru[:, hd * HEAD_DIM:(hd + 1) * HEAD_DIM]


def _k12_kernel(x_ref, g1_ref, win_ref, w2_ref, f1_ref, t0_ref, tw_ref, f3_ref, y_ref, ab_ref, gs_ref,
                *, n1, n2, s2_step):
    j = pl.program_id(1)
    n_s1 = n2 // (s2_step * S2_BLOCK)

    @pl.when(j < n_s1)
    def _():
        for sb in range(s2_step):
            _stage1_group(x_ref, g1_ref, win_ref, w2_ref, f1_ref, gs_ref, ab_ref, sb, j * s2_step + sb, n1)

    @pl.when(j >= n_s1)
    def _():
        _stage2_pairs(gs_ref, t0_ref, tw_ref, f3_ref, y_ref, j - n_s1, n2)


def _dft_tables(s, n1):
    n2 = s // n1
    k1 = np.arange(n1)
    a1 = 2.0 * np.pi * np.outer(k1, k1) / n1
    c1, s1 = np.cos(a1), np.sin(a1)
    f1 = np.block([[c1, s1], [-s1, c1]]) / np.sqrt(n1)
    at = 2.0 * np.pi * np.outer(k1, np.arange(n2)) / s
    k2 = np.arange(n2)
    a2 = 2.0 * np.pi * np.outer(k2, k2) / n2
    f3 = np.concatenate([np.cos(a2), np.sin(a2)], axis=1) / np.sqrt(n2)
    f3p = np.einsum("krs,ef->kersf", f3.reshape(n2, 2, n2), np.eye(2)).reshape(2 * n2, 4 * n2)
    lanes = np.ones((1, 1, HEAD_DIM))
    atp = at.reshape(n1 // 2, 2, n2).transpose(0, 2, 1).reshape(n1 // 2, 2 * n2)
    at0 = atp[::PAIR_BLOCKS * K1_PAIRS]
    t0 = np.stack([np.cos(at0), np.sin(at0)], axis=1)[:, :, :, None] * lanes
    a_next = np.repeat(2.0 * np.pi * 2.0 * np.arange(n2) / s, 2)
    tw = np.stack([np.cos(a_next), np.sin(a_next)])[:, :, None] * lanes
    return (jnp.asarray(f1, F32).astype(BF16), jnp.asarray(t0, F32), jnp.asarray(tw, F32),
            jnp.asarray(f3p, F32).astype(BF16))


def _s2_step(s):
    spectrum = s * A_WIDTH * 4
    x_group = 2 * DFT_SPLIT * S2_BLOCK * D_MODEL * 4
    return max(1, min(MAX_S2_STEP, (VMEM_LIMIT_V7X - spectrum - K12_FIXED_VMEM) // x_group))


def _k12(x, g1, w_in_a, w2):
    b, s, _ = x.shape
    n1 = DFT_SPLIT
    n2 = s // n1
    half = n1 // 2
    nj = n2 // S2_BLOCK
    f1, t0, tw, f3 = _dft_tables(s, n1)
    s2_step = _s2_step(s)
    n_s1 = n2 // (s2_step * S2_BLOCK)
    n_s2 = half // (PAIR_BLOCKS * K1_PAIRS)
    return pl.pallas_call(
        functools.partial(_k12_kernel, n1=n1, n2=n2, s2_step=s2_step),
        grid=(b, n_s1 + n_s2),
        in_specs=[
            pl.BlockSpec((1, n1, s2_step * S2_BLOCK, D_MODEL),
                         lambda bi, j: (bi, 0, jnp.minimum(j, n_s1 - 1), 0)),
            _const_spec((1, D_MODEL)),
            _const_spec((D_MODEL, A_WIDTH)),
            _const_spec((N_HEADS, HEAD_DIM, 2 * HEAD_DIM)),
            _const_spec((2 * n1, 2 * n1)),
            pl.BlockSpec((1, 2, 2 * n2, HEAD_DIM), lambda bi, j: (jnp.maximum(j - n_s1, 0), 0, 0, 0)),
            _const_spec((2, 2 * n2, HEAD_DIM)),
            _const_spec((2 * n2, 4 * n2)),
        ],
        out_specs=pl.BlockSpec((1, N_HEADS, PAIR_BLOCKS, n2, K1_PAIRS, HEAD_DIM),
                               lambda bi, j: (bi, 0, jnp.maximum(j - n_s1, 0), 0, 0, 0)),
        out_shape=jax.ShapeDtypeStruct((b, N_HEADS, half // K1_PAIRS, n2, K1_PAIRS, HEAD_DIM), jnp.uint32),
        scratch_shapes=[
            pltpu.VMEM((2, 2, n1 * S2_BLOCK, HEAD_DIM), F32),
            pltpu.VMEM((N_HEADS, 2, nj, half, S2_BLOCK, HEAD_DIM), jnp.uint32),
        ],
        compiler_params=pltpu.CompilerParams(
            dimension_semantics=("parallel", "arbitrary"), vmem_limit_bytes=VMEM_LIMIT_V7X),
        name="k12_in_proj_dft",
    )(x.reshape(b, n1, n2, D_MODEL), g1, w_in_a, w2, f1, t0, tw, f3)


def _k3_kernel(x_ref, xp_ref, xn_ref, ya_ref, g1_ref, winb_ref, wouta_ref, wpo_ref, g2_ref,
               wg_ref, wu_ref, wd_ref, gf_ref, o_ref, hs_ref, zs_ref, act_ref, d_ref, h2_ref, *, seq_len):
    tm = x_ref.shape[1]
    th = tm // 2
    i = pl.program_id(1)
    last = pl.num_programs(1) - 1
    x = x_ref[0]
    g1 = g1_ref[...]
    g2 = g2_ref[...]
    gf = gf_ref[...]
    half_a = slice(0, th)
    half_b = slice(th, tm)

    ya = jnp.concatenate(
        [pltpu.bitcast(jnp.swapaxes(ya_ref[0, hd], 0, 1).reshape(tm // 2, HEAD_DIM), BF16)
         for hd in range(N_HEADS)], axis=-1)
    x1p = x + _dot(ya, wouta_ref[...])

    hs_ref[:HALO] = _rmsnorm(xp_ref[0], g1).astype(BF16)
    hs_ref[HALO:HALO + tm] = _rmsnorm(x, g1).astype(BF16)
    hs_ref[HALO + tm:] = _rmsnorm(xn_ref[0], g1).astype(BF16)
    zb = _dot(hs_ref[...], winb_ref[...])
    row = jax.lax.broadcasted_iota(jnp.int32, (tm + 2 * HALO, 1), 0)
    inside = ((row >= HALO) | (i > 0)) & ((row < HALO + tm) | (i < last))
    zb = jnp.where(inside, zb, 0.0)
    for gi in range(N_HEADS):
        zs_ref[gi] = zb[:, gi * HEAD_DIM:(gi + 1) * HEAD_DIM]

    def pool_rows(r0):
        n = VPU_SLICE
        t = i * tm + r0 + jax.lax.broadcasted_iota(jnp.int32, (n, 1), 0)
        d = []
        for gi, k in enumerate(POOL_WINDOWS):
            sums = zs_ref[gi, pl.ds(HALO + r0 - k // 2, n), :]
            for j in range(1, k):
                sums = sums + zs_ref[gi, pl.ds(HALO + r0 - k // 2 + j, n), :]
            cnt = jnp.minimum(t + k // 2, seq_len) - jnp.maximum(t - k // 2, 0)
            d.append((sums / cnt.astype(F32) - zs_ref[gi, pl.ds(HALO + r0, n), :]).astype(BF16))
        d_ref[pl.ds(r0, n), :] = jnp.concatenate(d, axis=-1)

    def ffn_chunk(rows, c):
        cols = slice(c * FF_CHUNK, (c + 1) * FF_CHUNK)
        h2 = h2_ref[rows, :]
        gate = _dot(h2, wg_ref[:, cols])
        up = _dot(h2, wu_ref[:, cols])
        act_ref[rows, cols] = (gate * jax.nn.sigmoid(gate) * up).astype(BF16)

    n_chunks = D_FF // FF_CHUNK
    n_slices = th // VPU_SLICE

    for s in range(n_slices):
        pool_rows(s * VPU_SLICE)
    x1a = x1p[half_a] + _dot(d_ref[half_a, :], wpo_ref[...])
    h2_ref[half_a, :] = _rmsnorm(x1a, g2).astype(BF16)

    box = {}

    def proj_b():
        box["x1b"] = x1p[half_b] + _dot(d_ref[half_b, :], wpo_ref[...])

    def norm2_b(s):
        r = slice(s * VPU_SLICE, (s + 1) * VPU_SLICE)
        h2_ref[pl.ds(th + s * VPU_SLICE, VPU_SLICE), :] = _rmsnorm(box["x1b"][r], g2).astype(BF16)

    tasks = [functools.partial(pool_rows, th + s * VPU_SLICE) for s in range(n_slices)]
    tasks += [proj_b] + [functools.partial(norm2_b, s) for s in range(n_slices)]
    per_chunk = -(-len(tasks) // n_chunks)
    for c in range(n_chunks):
        ffn_chunk(half_a, c)
        for task in tasks[c * per_chunk:(c + 1) * per_chunk]:
            task()
    x2a = x1a + _dot(act_ref[half_a, :], wd_ref[...])

    def final_a(s):
        r = slice(s * VPU_SLICE, (s + 1) * VPU_SLICE)
        o_ref[0, r, :] = _rmsnorm(x2a[r], gf)

    tasks = [functools.partial(final_a, s) for s in range(n_slices)]
    per_chunk = -(-len(tasks) // n_chunks)
    for c in range(n_chunks):
        ffn_chunk(half_b, c)
        for task in tasks[c * per_chunk:(c + 1) * per_chunk]:
            task()
    x2b = box["x1b"] + _dot(act_ref[half_b, :], wd_ref[...])
    o_ref[0, half_b, :] = _rmsnorm(x2b, gf)


def _k3(x, ya, g1, w_in_b, w_out_a, w_po, g2, w_gate, w_up, w_down, gf):
    b, s, _ = x.shape
    tm = TOKEN_TILE
    hb = tm // HALO
    n_hb = s // HALO
    return pl.pallas_call(
        functools.partial(_k3_kernel, seq_len=s),
        grid=(b, s // tm),
        in_specs=[
            pl.BlockSpec((1, tm, D_MODEL), lambda bi, i: (bi, i, 0)),
            pl.BlockSpec((1, HALO, D_MODEL), lambda bi, i: (bi, jnp.maximum(i * hb - 1, 0), 0)),
            pl.BlockSpec((1, HALO, D_MODEL), lambda bi, i: (bi, jnp.minimum((i + 1) * hb, n_hb - 1), 0)),
            pl.BlockSpec((1, N_HEADS, DFT_SPLIT // (2 * K1_PAIRS), tm // DFT_SPLIT, K1_PAIRS, HEAD_DIM),
                         lambda bi, i: (bi, 0, 0, i, 0, 0)),
            _const_spec((1, D_MODEL)),
            _const_spec((D_MODEL, B_WIDTH)),
            _const_spec((A_WIDTH, D_MODEL)),
            _const_spec((B_WIDTH, D_MODEL)),
            _const_spec((1, D_MODEL)),
            _const_spec((D_MODEL, D_FF)),
            _const_spec((D_MODEL, D_FF)),
            _const_spec((D_FF, D_MODEL)),
            _const_spec((1, D_MODEL)),
        ],
        out_specs=pl.BlockSpec((1, tm, D_MODEL), lambda bi, i: (bi, i, 0)),
        out_shape=jax.ShapeDtypeStruct((b, s, D_MODEL), F32),
        scratch_shapes=[
            pltpu.VMEM((tm + 2 * HALO, D_MODEL), BF16),
            pltpu.VMEM((N_HEADS, tm + 2 * HALO, HEAD_DIM), F32),
            pltpu.VMEM((tm, D_FF), BF16),
            pltpu.VMEM((tm, B_WIDTH), BF16),
            pltpu.VMEM((tm, D_MODEL), BF16),
        ],
        compiler_params=pltpu.CompilerParams(
            dimension_semantics=("parallel", "parallel"), vmem_limit_bytes=VMEM_LIMIT_V7X),
        name="k3_pool_out_ffn",
    )(x, x, x, ya, g1, w_in_b, w_out_a, w_po, g2, w_gate, w_up, w_down, gf)


def kernel(x_prompt, x_sample, norm1_g, w_in, w_fourier, w_pool, pool_scale, w_out, norm2_g,
           w_gate, w_up, w_down, normf_g):
    assert norm1_g.shape[0] == 1, "single-layer block"
    g1 = norm1_g[0][None]
    g2 = norm2_g[0][None]
    gf = normf_g[None]
    w_in_a = w_in[0, :, :A_WIDTH].astype(BF16)
    w_in_b = w_in[0, :, A_WIDTH:].astype(BF16)
    w_o_a = w_out[0, :A_WIDTH].astype(BF16)
    w_g = w_gate[0].astype(BF16)
    w_u = w_up[0].astype(BF16)
    w_d = w_down[0].astype(BF16)
    w2, w_po = _prep(w_fourier[0], w_pool[0], pool_scale[0][None], w_out[0])

    def trunk(x):
        ya = _k12(x, g1, w_in_a, w2)
        return _k3(x, ya, g1, w_in_b, w_o_a, w_po, g2, w_g, w_u, w_d, gf)

    return trunk(x_prompt), trunk(x_sample)
```

```python
import functools

import jax
import jax.numpy as jnp
import numpy as np
from jax.experimental import pallas as pl
from jax.experimental.pallas import tpu as pltpu

F32 = jnp.float32
BF16 = jnp.bfloat16

D_MODEL = 1024
N_HEADS = 4
HEAD_DIM = 128
A_WIDTH = N_HEADS * HEAD_DIM
B_WIDTH = N_HEADS * HEAD_DIM
POOL_WINDOWS = (2, 4, 8, 16)
D_FF = 2816
EPS = 1e-6

HALO = 16
TOKEN_TILE = 1024
FF_CHUNK = 256
VPU_SLICE = 64
DFT_SPLIT = 128
S2_BLOCK = 8
MAX_S2_STEP = 2
K12_FIXED_VMEM = 14 * 1024 * 1024
K1_PAIRS = 8
PAIR_BLOCKS = 2
VMEM_LIMIT_V7X = 58 * 1024 * 1024


def _const_spec(shape):
    zeros = (0,) * len(shape)
    return pl.BlockSpec(shape, lambda *_: zeros, pipeline_mode=pl.Buffered(1))


def _rmsnorm(x, g):
    r = jax.lax.rsqrt(jnp.mean(x * x, axis=-1, keepdims=True) + EPS)
    return x * r * g


def _dot(a, b):
    return jnp.dot(a, b, preferred_element_type=F32)


def _split_bf16(a):
    hi = a.astype(BF16)
    lo = (a - hi.astype(F32)).astype(BF16)
    return hi, lo


def _dot_f32(a, b):
    ah, al = _split_bf16(a)
    bh, bl = _split_bf16(b)
    return _dot(ah, bh) + (_dot(ah, bl) + _dot(al, bh))


def _prep_kernel(cs_ref, wf_ref, wp_ref, ps_ref, wout_ref, w2_ref, wpo_ref):
    for h in range(N_HEADS):
        w = wf_ref[h]
        w2_ref[h, :, :HEAD_DIM] = _dot_f32(cs_ref[0], w).astype(BF16)
        w2_ref[h, :, HEAD_DIM:] = (-_dot_f32(cs_ref[1], w)).astype(BF16)
    for g in range(N_HEADS):
        lanes = slice(g * HEAD_DIM, (g + 1) * HEAD_DIM)
        rows = slice(A_WIDTH + g * HEAD_DIM, A_WIDTH + (g + 1) * HEAD_DIM)
        wpo_ref[lanes, :] = _dot_f32(wp_ref[g] * ps_ref[:, lanes], wout_ref[rows, :]).astype(BF16)


def _prep(w_fourier, w_pool, pool_scale, w_out):
    n = np.arange(HEAD_DIM)
    ang = 2.0 * np.pi * np.outer(n, n) / HEAD_DIM
    cs = np.stack([np.cos(ang), np.sin(ang)]) / np.sqrt(HEAD_DIM)
    return pl.pallas_call(
        _prep_kernel,
        out_shape=(jax.ShapeDtypeStruct((N_HEADS, HEAD_DIM, 2 * HEAD_DIM), BF16),
                   jax.ShapeDtypeStruct((B_WIDTH, D_MODEL), BF16)),
        name="prep_weights",
    )(jnp.asarray(cs, F32), w_fourier, w_pool, pool_scale, w_out)


def _stage1_group(x_ref, g1_ref, win_ref, w2_ref, f1_ref, gs_ref, ab_ref, sb, jg, n1):
    rows = n1 * S2_BLOCK
    half = n1 // 2
    x = x_ref[0, :, sb * S2_BLOCK:(sb + 1) * S2_BLOCK, :].reshape(rows, D_MODEL)
    h = _rmsnorm(x, g1_ref[...]).astype(BF16)
    za = _dot(h, win_ref[...]).astype(BF16)
    for hd in range(N_HEADS):
        ab = _dot(za[:, hd * HEAD_DIM:(hd + 1) * HEAD_DIM], w2_ref[hd])
        buf = ab_ref.at[hd % 2]
        buf[0] = ab[:, :HEAD_DIM]
        buf[1] = ab[:, HEAD_DIM:]
        xx = jnp.concatenate(
            [jnp.concatenate([buf.at[ri][pl.ds(j, n1, stride=S2_BLOCK), :]
                              for j in range(S2_BLOCK)], axis=1) for ri in range(2)],
            axis=0).astype(BF16)
        g = _dot(f1_ref[...], xx)
        gp = pltpu.bitcast(g.astype(BF16), jnp.uint32)
        for ri in range(2):
            out = gs_ref.at[hd, ri, jg].reshape(half * S2_BLOCK, HEAD_DIM)
            for j in range(S2_BLOCK):
                out[pl.ds(j, half, stride=S2_BLOCK), :] = (
                    gp[ri * half:(ri + 1) * half, j * HEAD_DIM:(j + 1) * HEAD_DIM])


def _stage2_pairs(gs_ref, t0_ref, tw_ref, f3_ref, y_ref, jb, n2):
    pairs = PAIR_BLOCKS * K1_PAIRS
    tr = t0_ref[0, 0]
    ts = t0_ref[0, 1]
    for p in range(pairs):
        if p:
            tr, ts = tr * tw_ref[0] - ts * tw_ref[1], tr * tw_ref[1] + ts * tw_ref[0]
        pr = []
        pi = []
        for hd in range(N_HEADS):
            wr = gs_ref[hd, 0, :, jb * pairs + p].reshape(n2, HEAD_DIM)
            wi = gs_ref[hd, 1, :, jb * pairs + p].reshape(n2, HEAD_DIM)
            gr = pltpu.bitcast(wr, BF16).astype(F32)
            gi = pltpu.bitcast(wi, BF16).astype(F32)
            pr.append(gr * tr + gi * ts)
            pi.append(gi * tr - gr * ts)
        x = jnp.concatenate([jnp.concatenate(pr, axis=1), jnp.concatenate(pi, axis=1)],
                            axis=0).astype(BF16)
        r = _dot(f3_ref[...], x)
        ru = pltpu.bitcast(r.astype(BF16), jnp.uint32)
        for hd in range(N_HEADS):
            out = y_ref.at[0, hd, p // K1_PAIRS].reshape(n2 * K1_PAIRS, HEAD_DIM)
            out[pl.ds(p % K1_PAIRS, n2, stride=K1_PAIRS), :] = ru[:, hd * HEAD_DIM:(hd + 1) * HEAD_DIM]


def _k12_kernel(x_ref, g1_ref, win_ref, w2_ref, f1_ref, t0_ref, tw_ref, f3_ref, y_ref, ab_ref, gs_ref,
                *, n1, n2, s2_step):
    j = pl.program_id(1)
    n_s1 = n2 // (s2_step * S2_BLOCK)

    @pl.when(j < n_s1)
    def _():
        for sb in range(s2_step):
            _stage1_group(x_ref, g1_ref, win_ref, w2_ref, f1_ref, gs_ref, ab_ref, sb, j * s2_step + sb, n1)

    @pl.when(j >= n_s1)
    def _():
        _stage2_pairs(gs_ref, t0_ref, tw_ref, f3_ref, y_ref, j - n_s1, n2)


def _dft_tables(s, n1):
    n2 = s // n1
    k1 = np.arange(n1)
    a1 = 2.0 * np.pi * np.outer(k1, k1) / n1
    c1, s1 = np.cos(a1), np.sin(a1)
    f1 = np.block([[c1, s1], [-s1, c1]]) / np.sqrt(n1)
    at = 2.0 * np.pi * np.outer(k1, np.arange(n2)) / s
    k2 = np.arange(n2)
    a2 = 2.0 * np.pi * np.outer(k2, k2) / n2
    f3 = np.concatenate([np.cos(a2), np.sin(a2)], axis=1) / np.sqrt(n2)
    f3p = np.einsum("krs,ef->kersf", f3.reshape(n2, 2, n2), np.eye(2)).reshape(2 * n2, 4 * n2)
    lanes = np.ones((1, 1, HEAD_DIM))
    atp = at.reshape(n1 // 2, 2, n2).transpose(0, 2, 1).reshape(n1 // 2, 2 * n2)
    at0 = atp[::PAIR_BLOCKS * K1_PAIRS]
    t0 = np.stack([np.cos(at0), np.sin(at0)], axis=1)[:, :, :, None] * lanes
    a_next = np.repeat(2.0 * np.pi * 2.0 * np.arange(n2) / s, 2)
    tw = np.stack([np.cos(a_next), np.sin(a_next)])[:, :, None] * lanes
    return (jnp.asarray(f1, F32).astype(BF16), jnp.asarray(t0, F32), jnp.asarray(tw, F32),
            jnp.asarray(f3p, F32).astype(BF16))


def _s2_step(s):
    spectrum = s * A_WIDTH * 4
    x_group = 2 * DFT_SPLIT * S2_BLOCK * D_MODEL * 4
    return max(1, min(MAX_S2_STEP, (VMEM_LIMIT_V7X - spectrum - K12_FIXED_VMEM) // x_group))


def _k12(x, g1, w_in_a, w2):
    b, s, _ = x.shape
    n1 = DFT_SPLIT
    n2 = s // n1
    half = n1 // 2
    nj = n2 // S2_BLOCK
    f1, t0, tw, f3 = _dft_tables(s, n1)
    s2_step = _s2_step(s)
    n_s1 = n2 // (s2_step * S2_BLOCK)
    n_s2 = half // (PAIR_BLOCKS * K1_PAIRS)
    return pl.pallas_call(
        functools.partial(_k12_kernel, n1=n1, n2=n2, s2_step=s2_step),
        grid=(b, n_s1 + n_s2),
        in_specs=[
            pl.BlockSpec((1, n1, s2_step * S2_BLOCK, D_MODEL),
                         lambda bi, j: (bi, 0, jnp.minimum(j, n_s1 - 1), 0)),
            _const_spec((1, D_MODEL)),
            _const_spec((D_MODEL, A_WIDTH)),
            _const_spec((N_HEADS, HEAD_DIM, 2 * HEAD_DIM)),
            _const_spec((2 * n1, 2 * n1)),
            pl.BlockSpec((1, 2, 2 * n2, HEAD_DIM), lambda bi, j: (jnp.maximum(j - n_s1, 0), 0, 0, 0)),
            _const_spec((2, 2 * n2, HEAD_DIM)),
            _const_spec((2 * n2, 4 * n2)),
        ],
        out_specs=pl.BlockSpec((1, N_HEADS, PAIR_BLOCKS, n2, K1_PAIRS, HEAD_DIM),
                               lambda bi, j: (bi, 0, jnp.maximum(j - n_s1, 0), 0, 0, 0)),
        out_shape=jax.ShapeDtypeStruct((b, N_HEADS, half // K1_PAIRS, n2, K1_PAIRS, HEAD_DIM), jnp.uint32),
        scratch_shapes=[
            pltpu.VMEM((2, 2, n1 * S2_BLOCK, HEAD_DIM), F32),
            pltpu.VMEM((N_HEADS, 2, nj, half, S2_BLOCK, HEAD_DIM), jnp.uint32),
        ],
        compiler_params=pltpu.CompilerParams(
            dimension_semantics=("parallel", "arbitrary"), vmem_limit_bytes=VMEM_LIMIT_V7X),
        name="k12_in_proj_dft",
    )(x.reshape(b, n1, n2, D_MODEL), g1, w_in_a, w2, f1, t0, tw, f3)


def _k3_kernel(x_ref, xp_ref, xn_ref, ya_ref, g1_ref, winb_ref, wouta_ref, wpo_ref, g2_ref,
               wg_ref, wu_ref, wd_ref, gf_ref, o_ref, hs_ref, zs_ref, act_ref, d_ref, h2_ref, *, seq_len):
    tm = x_ref.shape[1]
    th = tm // 2
    i = pl.program_id(1)
    last = pl.num_programs(1) - 1
    x = x_ref[0]
    g1 = g1_ref[...]
    g2 = g2_ref[...]
    gf = gf_ref[...]
    half_a = slice(0, th)
    half_b = slice(th, tm)

    ya = jnp.concatenate(
        [pltpu.bitcast(jnp.swapaxes(ya_ref[0, hd], 0, 1).reshape(tm // 2, HEAD_DIM), BF16)
         for hd in range(N_HEADS)], axis=-1)
    x1p = x + _dot(ya, wouta_ref[...])

    hs_ref[:HALO] = _rmsnorm(xp_ref[0], g1).astype(BF16)
    hs_ref[HALO:HALO + tm] = _rmsnorm(x, g1).astype(BF16)
    hs_ref[HALO + tm:] = _rmsnorm(xn_ref[0], g1).astype(BF16)
    zb = _dot(hs_ref[...], winb_ref[...])
    row = jax.lax.broadcasted_iota(jnp.int32, (tm + 2 * HALO, 1), 0)
    inside = ((row >= HALO) | (i > 0)) & ((row < HALO + tm) | (i < last))
    zb = jnp.where(inside, zb, 0.0)
    for gi in range(N_HEADS):
        zs_ref[gi] = zb[:, gi * HEAD_DIM:(gi + 1) * HEAD_DIM]

    def pool_rows(r0):
        n = VPU_SLICE
        t = i * tm + r0 + jax.lax.broadcasted_iota(jnp.int32, (n, 1), 0)
        d = []
        for gi, k in enumerate(POOL_WINDOWS):
            sums = zs_ref[gi, pl.ds(HALO + r0 - k // 2, n), :]
            for j in range(1, k):
                sums = sums + zs_ref[gi, pl.ds(HALO + r0 - k // 2 + j, n), :]
            cnt = jnp.minimum(t + k // 2, seq_len) - jnp.maximum(t - k // 2, 0)
            d.append((sums / cnt.astype(F32) - zs_ref[gi, pl.ds(HALO + r0, n), :]).astype(BF16))
        d_ref[pl.ds(r0, n), :] = jnp.concatenate(d, axis=-1)

    def ffn_chunk(rows, c):
        cols = slice(c * FF_CHUNK, (c + 1) * FF_CHUNK)
        h2 = h2_ref[rows, :]
        gate = _dot(h2, wg_ref[:, cols])
        up = _dot(h2, wu_ref[:, cols])
        act_ref[rows, cols] = (gate * jax.nn.sigmoid(gate) * up).astype(BF16)

    n_chunks = D_FF // FF_CHUNK
    n_slices = th // VPU_SLICE

    for s in range(n_slices):
        pool_rows(s * VPU_SLICE)
    x1a = x1p[half_a] + _dot(d_ref[half_a, :], wpo_ref[...])
    h2_ref[half_a, :] = _rmsnorm(x1a, g2).astype(BF16)

    box = {}

    def proj_b():
        box["x1b"] = x1p[half_b] + _dot(d_ref[half_b, :], wpo_ref[...])

    def norm2_b(s):
        r = slice(s * VPU_SLICE, (s + 1) * VPU_SLICE)
        h2_ref[pl.ds(th + s * VPU_SLICE, VPU_SLICE), :] = _rmsnorm(box["x1b"][r], g2).astype(BF16)

    tasks = [functools.partial(pool_rows, th + s * VPU_SLICE) for s in range(n_slices)]
    tasks += [proj_b] + [functools.partial(norm2_b, s) for s in range(n_slices)]
    per_chunk = -(-len(tasks) // n_chunks)
    for c in range(n_chunks):
        ffn_chunk(half_a, c)
        for task in tasks[c * per_chunk:(c + 1) * per_chunk]:
            task()
    x2a = x1a + _dot(act_ref[half_a, :], wd_ref[...])

    def final_a(s):
        r = slice(s * VPU_SLICE, (s + 1) * VPU_SLICE)
        o_ref[0, r, :] = _rmsnorm(x2a[r], gf)

    tasks = [functools.partial(final_a, s) for s in range(n_slices)]
    per_chunk = -(-len(tasks) // n_chunks)
    for c in range(n_chunks):
        ffn_chunk(half_b, c)
        for task in tasks[c * per_chunk:(c + 1) * per_chunk]:
            task()
    x2b = box["x1b"] + _dot(act_ref[half_b, :], wd_ref[...])
    o_ref[0, half_b, :] = _rmsnorm(x2b, gf)


def _k3(x, ya, g1, w_in_b, w_out_a, w_po, g2, w_gate, w_up, w_down, gf):
    b, s, _ = x.shape
    tm = TOKEN_TILE
    hb = tm // HALO
    n_hb = s // HALO
    return pl.pallas_call(
        functools.partial(_k3_kernel, seq_len=s),
        grid=(b, s // tm),
        in_specs=[
            pl.BlockSpec((1, tm, D_MODEL), lambda bi, i: (bi, i, 0)),
            pl.BlockSpec((1, HALO, D_MODEL), lambda bi, i: (bi, jnp.maximum(i * hb - 1, 0), 0)),
            pl.BlockSpec((1, HALO, D_MODEL), lambda bi, i: (bi, jnp.minimum((i + 1) * hb, n_hb - 1), 0)),
            pl.BlockSpec((1, N_HEADS, DFT_SPLIT // (2 * K1_PAIRS), tm // DFT_SPLIT, K1_PAIRS, HEAD_DIM),
                         lambda bi, i: (bi, 0, 0, i, 0, 0)),
            _const_spec((1, D_MODEL)),
            _const_spec((D_MODEL, B_WIDTH)),
            _const_spec((A_WIDTH, D_MODEL)),
            _const_spec((B_WIDTH, D_MODEL)),
            _const_spec((1, D_MODEL)),
            _const_spec((D_MODEL, D_FF)),
            _const_spec((D_MODEL, D_FF)),
            _const_spec((D_FF, D_MODEL)),
            _const_spec((1, D_MODEL)),
        ],
        out_specs=pl.BlockSpec((1, tm, D_MODEL), lambda bi, i: (bi, i, 0)),
        out_shape=jax.ShapeDtypeStruct((b, s, D_MODEL), F32),
        scratch_shapes=[
            pltpu.VMEM((tm + 2 * HALO, D_MODEL), BF16),
            pltpu.VMEM((N_HEADS, tm + 2 * HALO, HEAD_DIM), F32),
            pltpu.VMEM((tm, D_FF), BF16),
            pltpu.VMEM((tm, B_WIDTH), BF16),
            pltpu.VMEM((tm, D_MODEL), BF16),
        ],
        compiler_params=pltpu.CompilerParams(
            dimension_semantics=("parallel", "parallel"), vmem_limit_bytes=VMEM_LIMIT_V7X),
        name="k3_pool_out_ffn",
    )(x, x, x, ya, g1, w_in_b, w_out_a, w_po, g2, w_gate, w_up, w_down, gf)


def kernel(x_prompt, x_sample, norm1_g, w_in, w_fourier, w_pool, pool_scale, w_out, norm2_g,
           w_gate, w_up, w_down, normf_g):
    assert norm1_g.shape[0] == 1, "single-layer block"
    g1 = norm1_g[0][None]
    g2 = norm2_g[0][None]
    gf = normf_g[None]
    w_in_a = w_in[0, :, :A_WIDTH].astype(BF16)
    w_in_b = w_in[0, :, A_WIDTH:].astype(BF16)
    w_o_a = w_out[0, :A_WIDTH].astype(BF16)
    w_g = w_gate[0].astype(BF16)
    w_u = w_up[0].astype(BF16)
    w_d = w_down[0].astype(BF16)
    w2, w_po = _prep(w_fourier[0], w_pool[0], pool_scale[0][None], w_out[0])

    def trunk(x):
        ya = _k12(x, g1, w_in_a, w2)
        return _k3(x, ya, g1, w_in_b, w_o_a, w_po, g2, w_g, w_u, w_d, gf)

    y_sample = trunk(x_sample)
    y_prompt = trunk(x_prompt)
    return y_prompt, y_sample
```

```python
import functools

import jax
import jax.numpy as jnp
import numpy as np
from jax.experimental import pallas as pl
from jax.experimental.pallas import tpu as pltpu

F32 = jnp.float32
BF16 = jnp.bfloat16

D_MODEL = 1024
N_HEADS = 4
HEAD_DIM = 128
A_WIDTH = N_HEADS * HEAD_DIM
B_WIDTH = N_HEADS * HEAD_DIM
POOL_WINDOWS = (2, 4, 8, 16)
D_FF = 2816
EPS = 1e-6

HALO = 16
TOKEN_TILE = 1024
FF_CHUNK = 256
VPU_SLICE = 64
DFT_SPLIT = 128
S2_BLOCK = 8
MAX_S2_STEP = 2
K12_FIXED_VMEM = 14 * 1024 * 1024
K1_PAIRS = 8
PAIR_BLOCKS = 2
VMEM_LIMIT_V7X = 58 * 1024 * 1024


def _const_spec(shape):
    zeros = (0,) * len(shape)
    return pl.BlockSpec(shape, lambda *_: zeros, pipeline_mode=pl.Buffered(1))


def _rmsnorm(x, g):
    r = jax.lax.rsqrt(jnp.mean(x * x, axis=-1, keepdims=True) + EPS)
    return x * r * g


def _dot(a, b):
    return jnp.dot(a, b, preferred_element_type=F32)


def _split_bf16(a):
    hi = a.astype(BF16)
    lo = (a - hi.astype(F32)).astype(BF16)
    return hi, lo


def _dot_f32(a, b):
    ah, al = _split_bf16(a)
    bh, bl = _split_bf16(b)
    return _dot(ah, bh) + (_dot(ah, bl) + _dot(al, bh))


def _prep_kernel(cs_ref, wf_ref, wp_ref, ps_ref, wout_ref, w2_ref, wpo_ref):
    for h in range(N_HEADS):
        w = wf_ref[h]
        w2_ref[h, :, :HEAD_DIM] = _dot_f32(cs_ref[0], w).astype(BF16)
        w2_ref[h, :, HEAD_DIM:] = (-_dot_f32(cs_ref[1], w)).astype(BF16)
    for g in range(N_HEADS):
        lanes = slice(g * HEAD_DIM, (g + 1) * HEAD_DIM)
        rows = slice(A_WIDTH + g * HEAD_DIM, A_WIDTH + (g + 1) * HEAD_DIM)
        wpo_ref[lanes, :] = _dot_f32(wp_ref[g] * ps_ref[:, lanes], wout_ref[rows, :]).astype(BF16)


def _prep(w_fourier, w_pool, pool_scale, w_out):
    n = np.arange(HEAD_DIM)
    ang = 2.0 * np.pi * np.outer(n, n) / HEAD_DIM
    cs = np.stack([np.cos(ang), np.sin(ang)]) / np.sqrt(HEAD_DIM)
    return pl.pallas_call(
        _prep_kernel,
        out_shape=(jax.ShapeDtypeStruct((N_HEADS, HEAD_DIM, 2 * HEAD_DIM), BF16),
                   jax.ShapeDtypeStruct((B_WIDTH, D_MODEL), BF16)),
        name="prep_weights",
    )(jnp.asarray(cs, F32), w_fourier, w_pool, pool_scale, w_out)


def _stage1_group(x_ref, g1_ref, win_ref, w2_ref, f1_ref, gs_ref, ab_ref, sb, jg, n1):
    rows = n1 * S2_BLOCK
    half = n1 // 2
    x = x_ref[0, :, sb * S2_BLOCK:(sb + 1) * S2_BLOCK, :].reshape(rows, D_MODEL)
    h = _rmsnorm(x, g1_ref[...]).astype(BF16)
    za = _dot(h, win_ref[...]).astype(BF16)
    for hd in range(N_HEADS):
        ab = _dot(za[:, hd * HEAD_DIM:(hd + 1) * HEAD_DIM], w2_ref[hd])
        buf = ab_ref.at[hd % 2]
        buf[0] = ab[:, :HEAD_DIM]
        buf[1] = ab[:, HEAD_DIM:]
        xx = jnp.concatenate(
            [jnp.concatenate([buf.at[ri][pl.ds(j, n1, stride=S2_BLOCK), :]
                              for j in range(S2_BLOCK)], axis=1) for ri in range(2)],
            axis=0).astype(BF16)
        g = _dot(f1_ref[...], xx)
        gp = pltpu.bitcast(g.astype(BF16), jnp.uint32)
        for ri in range(2):
            out = gs_ref.at[hd, ri, jg].reshape(half * S2_BLOCK, HEAD_DIM)
            for j in range(S2_BLOCK):
                out[pl.ds(j, half, stride=S2_BLOCK), :] = (
                    gp[ri * half:(ri + 1) * half, j * HEAD_DIM:(j + 1) * HEAD_DIM])


def _stage2_pairs(gs_ref, t0_ref, tw_ref, f3_ref, y_ref, jb, n2):
    pairs = PAIR_BLOCKS * K1_PAIRS
    tr = t0_ref[0, 0]
    ts = t0_ref[0, 1]
    for p in range(pairs):
        if p:
            tr, ts = tr * tw_ref[0] - ts * tw_ref[1], tr * tw_ref[1] + ts * tw_ref[0]
        pr = []
        pi = []
        for hd in range(N_HEADS):
            wr = gs_ref[hd, 0, :, jb * pairs + p].reshape(n2, HEAD_DIM)
            wi = gs_ref[hd, 1, :, jb * pairs + p].reshape(n2, HEAD_DIM)
            gr = pltpu.bitcast(wr, BF16).astype(F32)
            gi = pltpu.bitcast(wi, BF16).astype(F32)
            pr.append(gr * tr + gi * ts)
            pi.append(gi * tr - gr * ts)
        x = jnp.concatenate([jnp.concatenate(pr, axis=1), jnp.concatenate(pi, axis=1)],
                            axis=0).astype(BF16)
        r = _dot(f3_ref[...], x)
        ru = pltpu.bitcast(r.astype(BF16), jnp.uint32)
        for hd in range(N_HEADS):
            out = y_ref.at[0, hd, p // K1_PAIRS].reshape(n2 * K1_PAIRS, HEAD_DIM)
            out[pl.ds(p % K1_PAIRS, n2, stride=K1_PAIRS), :] = ru[:, hd * HEAD_DIM:(hd + 1) * HEAD_DIM]


def _k12_kernel(x_ref, g1_ref, win_ref, w2_ref, f1_ref, t0_ref, tw_ref, f3_ref, y_ref, ab_ref, gs_ref,
                *, n1, n2, s2_step):
    j = pl.program_id(1)
    n_s1 = n2 // (s2_step * S2_BLOCK)

    @pl.when(j < n_s1)
    def _():
        for sb in range(s2_step):
            _stage1_group(x_ref, g1_ref, win_ref, w2_ref, f1_ref, gs_ref, ab_ref, sb, j * s2_step + sb, n1)

    @pl.when(j >= n_s1)
    def _():
        _stage2_pairs(gs_ref, t0_ref, tw_ref, f3_ref, y_ref, j - n_s1, n2)


def _dft_tables(s, n1):
    n2 = s // n1
    k1 = np.arange(n1)
    a1 = 2.0 * np.pi * np.outer(k1, k1) / n1
    c1, s1 = np.cos(a1), np.sin(a1)
    f1 = np.block([[c1, s1], [-s1, c1]]) / np.sqrt(n1)
    at = 2.0 * np.pi * np.outer(k1, np.arange(n2)) / s
    k2 = np.arange(n2)
    a2 = 2.0 * np.pi * np.outer(k2, k2) / n2
    f3 = np.concatenate([np.cos(a2), np.sin(a2)], axis=1) / np.sqrt(n2)
    f3p = np.einsum("krs,ef->kersf", f3.reshape(n2, 2, n2), np.eye(2)).reshape(2 * n2, 4 * n2)
    lanes = np.ones((1, 1, HEAD_DIM))
    atp = at.reshape(n1 // 2, 2, n2).transpose(0, 2, 1).reshape(n1 // 2, 2 * n2)
    at0 = atp[::PAIR_BLOCKS * K1_PAIRS]
    t0 = np.stack([np.cos(at0), np.sin(at0)], axis=1)[:, :, :, None] * lanes
    a_next = np.repeat(2.0 * np.pi * 2.0 * np.arange(n2) / s, 2)
    tw = np.stack([np.cos(a_next), np.sin(a_next)])[:, :, None] * lanes
    return (jnp.asarray(f1, F32).astype(BF16), jnp.asarray(t0, F32), jnp.asarray(tw, F32),
            jnp.asarray(f3p, F32).astype(BF16))


def _s2_step(s):
    spectrum = s * A_WIDTH * 4
    x_group = 2 * DFT_SPLIT * S2_BLOCK * D_MODEL * 4
    return max(1, min(MAX_S2_STEP, (VMEM_LIMIT_V7X - spectrum - K12_FIXED_VMEM) // x_group))


def _k12(x, g1, w_in_a, w2):
    b, s, _ = x.shape
    n1 = DFT_SPLIT
    n2 = s // n1
    half = n1 // 2
    nj = n2 // S2_BLOCK
    f1, t0, tw, f3 = _dft_tables(s, n1)
    s2_step = _s2_step(s)
    n_s1 = n2 // (s2_step * S2_BLOCK)
    n_s2 = half // (PAIR_BLOCKS * K1_PAIRS)
    return pl.pallas_call(
        functools.partial(_k12_kernel, n1=n1, n2=n2, s2_step=s2_step),
        grid=(b, n_s1 + n_s2),
        in_specs=[
            pl.BlockSpec((1, n1, s2_step * S2_BLOCK, D_MODEL),
                         lambda bi, j: (bi, 0, jnp.minimum(j, n_s1 - 1), 0)),
            _const_spec((1, D_MODEL)),
            _const_spec((D_MODEL, A_WIDTH)),
            _const_spec((N_HEADS, HEAD_DIM, 2 * HEAD_DIM)),
            _const_spec((2 * n1, 2 * n1)),
            pl.BlockSpec((1, 2, 2 * n2, HEAD_DIM), lambda bi, j: (jnp.maximum(j - n_s1, 0), 0, 0, 0)),
            _const_spec((2, 2 * n2, HEAD_DIM)),
            _const_spec((2 * n2, 4 * n2)),
        ],
        out_specs=pl.BlockSpec((1, N_HEADS, PAIR_BLOCKS, n2, K1_PAIRS, HEAD_DIM),
                               lambda bi, j: (bi, 0, jnp.maximum(j - n_s1, 0), 0, 0, 0)),
        out_shape=jax.ShapeDtypeStruct((b, N_HEADS, half // K1_PAIRS, n2, K1_PAIRS, HEAD_DIM), jnp.uint32),
        scratch_shapes=[
            pltpu.VMEM((2, 2, n1 * S2_BLOCK, HEAD_DIM), F32),
            pltpu.VMEM((N_HEADS, 2, nj, half, S2_BLOCK, HEAD_DIM), jnp.uint32),
        ],
        compiler_params=pltpu.CompilerParams(
            dimension_semantics=("parallel", "arbitrary"), vmem_limit_bytes=VMEM_LIMIT_V7X),
        name="k12_in_proj_dft",
    )(x.reshape(b, n1, n2, D_MODEL), g1, w_in_a, w2, f1, t0, tw, f3)


def _k3_kernel(x_ref, xp_ref, xn_ref, ya_ref, g1_ref, winb_ref, wouta_ref, wpo_ref, g2_ref,
               wg_ref, wu_ref, wd_ref, gf_ref, o_ref, hs_ref, zs_ref, act_ref, d_ref, h2_ref, *, seq_len):
    tm = x_ref.shape[1]
    th = tm // 2
    i = pl.program_id(1)
    last = pl.num_programs(1) - 1
    x = x_ref[0]
    g1 = g1_ref[...]
    g2 = g2_ref[...]
    gf = gf_ref[...]
    half_a = slice(0, th)
    half_b = slice(th, tm)

    ya = jnp.concatenate(
        [pltpu.bitcast(jnp.swapaxes(ya_ref[0, hd], 0, 1).reshape(tm // 2, HEAD_DIM), BF16)
         for hd in range(N_HEADS)], axis=-1)
    x1p = x + _dot(ya, wouta_ref[...])

    hs_ref[:HALO] = _rmsnorm(xp_ref[0], g1).astype(BF16)
    hs_ref[HALO:HALO + tm] = _rmsnorm(x, g1).astype(BF16)
    hs_ref[HALO + tm:] = _rmsnorm(xn_ref[0], g1).astype(BF16)
    zb = _dot(hs_ref[...], winb_ref[...])
    row = jax.lax.broadcasted_iota(jnp.int32, (tm + 2 * HALO, 1), 0)
    inside = ((row >= HALO) | (i > 0)) & ((row < HALO + tm) | (i < last))
    zb = jnp.where(inside, zb, 0.0)
    for gi in range(N_HEADS):
        zs_ref[gi] = zb[:, gi * HEAD_DIM:(gi + 1) * HEAD_DIM]

    def pool_rows(r0):
        n = VPU_SLICE
        t = i * tm + r0 + jax.lax.broadcasted_iota(jnp.int32, (n, 1), 0)
        d = []
        for gi, k in enumerate(POOL_WINDOWS):
            sums = zs_ref[gi, pl.ds(HALO + r0 - k // 2, n), :]
            for j in range(1, k):
                sums = sums + zs_ref[gi, pl.ds(HALO + r0 - k // 2 + j, n), :]
            cnt = jnp.minimum(t + k // 2, seq_len) - jnp.maximum(t - k // 2, 0)
            d.append((sums / cnt.astype(F32) - zs_ref[gi, pl.ds(HALO + r0, n), :]).astype(BF16))
        d_ref[pl.ds(r0, n), :] = jnp.concatenate(d, axis=-1)

    def ffn_chunk(rows, c):
        cols = slice(c * FF_CHUNK, (c + 1) * FF_CHUNK)
        h2 = h2_ref[rows, :]
        gate = _dot(h2, wg_ref[:, cols])
        up = _dot(h2, wu_ref[:, cols])
        act_ref[rows, cols] = (gate * jax.nn.sigmoid(gate) * up).astype(BF16)

    n_chunks = D_FF // FF_CHUNK
    n_slices = th // VPU_SLICE

    for s in range(n_slices):
        pool_rows(s * VPU_SLICE)
    x1a = x1p[half_a] + _dot(d_ref[half_a, :], wpo_ref[...])
    h2_ref[half_a, :] = _rmsnorm(x1a, g2).astype(BF16)

    box = {}

    def proj_b():
        box["x1b"] = x1p[half_b] + _dot(d_ref[half_b, :], wpo_ref[...])

    def norm2_b(s):
        r = slice(s * VPU_SLICE, (s + 1) * VPU_SLICE)
        h2_ref[pl.ds(th + s * VPU_SLICE, VPU_SLICE), :] = _rmsnorm(box["x1b"][r], g2).astype(BF16)

    tasks = [functools.partial(pool_rows, th + s * VPU_SLICE) for s in range(n_slices)]
    tasks += [proj_b] + [functools.partial(norm2_b, s) for s in range(n_slices)]
    per_chunk = -(-len(tasks) // n_chunks)
    for c in range(n_chunks):
        ffn_chunk(half_a, c)
        for task in tasks[c * per_chunk:(c + 1) * per_chunk]:
            task()
    x2a = x1a + _dot(act_ref[half_a, :], wd_ref[...])

    def final_a(s):
        r = slice(s * VPU_SLICE, (s + 1) * VPU_SLICE)
        o_ref[0, r, :] = _rmsnorm(x2a[r], gf)

    tasks = [functools.partial(final_a, s) for s in range(n_slices)]
    per_chunk = -(-len(tasks) // n_chunks)
    for c in range(n_chunks):
        ffn_chunk(half_b, c)
        for task in tasks[c * per_chunk:(c + 1) * per_chunk]:
            task()
    x2b = box["x1b"] + _dot(act_ref[half_b, :], wd_ref[...])
    o_ref[0, half_b, :] = _rmsnorm(x2b, gf)


def _k3(x, ya, g1, w_in_b, w_out_a, w_po, g2, w_gate, w_up, w_down, gf):
    b, s, _ = x.shape
    tm = TOKEN_TILE
    hb = tm // HALO
    n_hb = s // HALO
    return pl.pallas_call(
        functools.partial(_k3_kernel, seq_len=s),
        grid=(b, s // tm),
        in_specs=[
            pl.BlockSpec((1, tm, D_MODEL), lambda bi, i: (bi, i, 0)),
            pl.BlockSpec((1, HALO, D_MODEL), lambda bi, i: (bi, jnp.maximum(i * hb - 1, 0), 0)),
            pl.BlockSpec((1, HALO, D_MODEL), lambda bi, i: (bi, jnp.minimum((i + 1) * hb, n_hb - 1), 0)),
            pl.BlockSpec((1, N_HEADS, DFT_SPLIT // (2 * K1_PAIRS), tm // DFT_SPLIT, K1_PAIRS, HEAD_DIM),
                         lambda bi, i: (bi, 0, 0, i, 0, 0)),
            _const_spec((1, D_MODEL)),
            _const_spec((D_MODEL, B_WIDTH)),
            _const_spec((A_WIDTH, D_MODEL)),
            _const_spec((B_WIDTH, D_MODEL)),
            _const_spec((1, D_MODEL)),
            _const_spec((D_MODEL, D_FF)),
            _const_spec((D_MODEL, D_FF)),
            _const_spec((D_FF, D_MODEL)),
            _const_spec((1, D_MODEL)),
        ],
        out_specs=pl.BlockSpec((1, tm, D_MODEL), lambda bi, i: (bi, i, 0)),
        out_shape=jax.ShapeDtypeStruct((b, s, D_MODEL), F32),
        scratch_shapes=[
            pltpu.VMEM((tm + 2 * HALO, D_MODEL), BF16),
            pltpu.VMEM((N_HEADS, tm + 2 * HALO, HEAD_DIM), F32),
            pltpu.VMEM((tm, D_FF), BF16),
            pltpu.VMEM((tm, B_WIDTH), BF16),
            pltpu.VMEM((tm, D_MODEL), BF16),
        ],
        compiler_params=pltpu.CompilerParams(
            dimension_semantics=("parallel", "parallel"), vmem_limit_bytes=VMEM_LIMIT_V7X),
        name="k3_pool_out_ffn",
    )(x, x, x, ya, g1, w_in_b, w_out_a, w_po, g2, w_gate, w_up, w_down, gf)


def kernel(x_prompt, x_sample, norm1_g, w_in, w_fourier, w_pool, pool_scale, w_out, norm2_g,
           w_gate, w_up, w_down, normf_g):
    assert norm1_g.shape[0] == 1, "single-layer block"
    g1 = norm1_g[0][None]
    g2 = norm2_g[0][None]
    gf = normf_g[None]
    w_in_a = w_in[0, :, :A_WIDTH].astype(BF16)
    w_in_b = w_in[0, :, A_WIDTH:].astype(BF16)
    w_o_a = w_out[0, :A_WIDTH].astype(BF16)
    w_g = w_gate[0].astype(BF16)
    w_u = w_up[0].astype(BF16)
    w_d = w_down[0].astype(BF16)
    w2, w_po = _prep(w_fourier[0], w_pool[0], pool_scale[0][None], w_out[0])

    def trunk(x):
        ya = _k12(x, g1, w_in_a, w2)
        return _k3(x, ya, g1, w_in_b, w_o_a, w_po, g2, w_g, w_u, w_d, gf)

    y_prompt = trunk(x_prompt)
    y_prompt, x_sample = jax.lax.optimization_barrier((y_prompt, x_sample))
    return y_prompt, trunk(x_sample)
```

```python
import functools

import jax
import jax.numpy as jnp
import numpy as np
from jax.experimental import pallas as pl
from jax.experimental.pallas import tpu as pltpu

F32 = jnp.float32
BF16 = jnp.bfloat16

D_MODEL = 1024
N_HEADS = 4
HEAD_DIM = 128
A_WIDTH = N_HEADS * HEAD_DIM
B_WIDTH = N_HEADS * HEAD_DIM
POOL_WINDOWS = (2, 4, 8, 16)
D_FF = 2816
EPS = 1e-6

HALO = 16
TOKEN_TILE = 1024
FF_CHUNK = 256
VPU_SLICE = 64
PROJ_SPLIT = 2
DFT_SPLIT = 128
S2_BLOCK = 8
MAX_S2_STEP = 2
K12_FIXED_VMEM = 14 * 1024 * 1024
K1_PAIRS = 8
PAIR_BLOCKS = 2
VMEM_LIMIT_V7X = 58 * 1024 * 1024


def _const_spec(shape):
    zeros = (0,) * len(shape)
    return pl.BlockSpec(shape, lambda *_: zeros, pipeline_mode=pl.Buffered(1))


def _rmsnorm(x, g):
    r = jax.lax.rsqrt(jnp.mean(x * x, axis=-1, keepdims=True) + EPS)
    return x * r * g


def _dot(a, b):
    return jnp.dot(a, b, preferred_element_type=F32)


def _split_bf16(a):
    hi = a.astype(BF16)
    lo = (a - hi.astype(F32)).astype(BF16)
    return hi, lo


def _dot_f32(a, b):
    ah, al = _split_bf16(a)
    bh, bl = _split_bf16(b)
    return _dot(ah, bh) + (_dot(ah, bl) + _dot(al, bh))


def _prep_kernel(cs_ref, wf_ref, wp_ref, ps_ref, wout_ref, w2_ref, wpo_ref):
    for h in range(N_HEADS):
        w = wf_ref[h]
        w2_ref[h, :, :HEAD_DIM] = _dot_f32(cs_ref[0], w).astype(BF16)
        w2_ref[h, :, HEAD_DIM:] = (-_dot_f32(cs_ref[1], w)).astype(BF16)
    for g in range(N_HEADS):
        lanes = slice(g * HEAD_DIM, (g + 1) * HEAD_DIM)
        rows = slice(A_WIDTH + g * HEAD_DIM, A_WIDTH + (g + 1) * HEAD_DIM)
        wpo_ref[lanes, :] = _dot_f32(wp_ref[g] * ps_ref[:, lanes], wout_ref[rows, :]).astype(BF16)


def _prep(w_fourier, w_pool, pool_scale, w_out):
    n = np.arange(HEAD_DIM)
    ang = 2.0 * np.pi * np.outer(n, n) / HEAD_DIM
    cs = np.stack([np.cos(ang), np.sin(ang)]) / np.sqrt(HEAD_DIM)
    return pl.pallas_call(
        _prep_kernel,
        out_shape=(jax.ShapeDtypeStruct((N_HEADS, HEAD_DIM, 2 * HEAD_DIM), BF16),
                   jax.ShapeDtypeStruct((B_WIDTH, D_MODEL), BF16)),
        name="prep_weights",
    )(jnp.asarray(cs, F32), w_fourier, w_pool, pool_scale, w_out)


def _stage1_group(x_ref, g1_ref, win_ref, w2_ref, f1_ref, gs_ref, ab_ref, sb, jg, n1):
    rows = n1 * S2_BLOCK
    half = n1 // 2
    x = x_ref[0, :, sb * S2_BLOCK:(sb + 1) * S2_BLOCK, :].reshape(rows, D_MODEL)
    h = _rmsnorm(x, g1_ref[...]).astype(BF16)
    za = _dot(h, win_ref[...]).astype(BF16)
    for hd in range(N_HEADS):
        ab = _dot(za[:, hd * HEAD_DIM:(hd + 1) * HEAD_DIM], w2_ref[hd])
        buf = ab_ref.at[hd % 2]
        buf[0] = ab[:, :HEAD_DIM]
        buf[1] = ab[:, HEAD_DIM:]
        xx = jnp.concatenate(
            [jnp.concatenate([buf.at[ri][pl.ds(j, n1, stride=S2_BLOCK), :]
                              for j in range(S2_BLOCK)], axis=1) for ri in range(2)],
            axis=0).astype(BF16)
        g = _dot(f1_ref[...], xx)
        gp = pltpu.bitcast(g.astype(BF16), jnp.uint32)
        for ri in range(2):
            out = gs_ref.at[hd, ri, jg].reshape(half * S2_BLOCK, HEAD_DIM)
            for j in range(S2_BLOCK):
                out[pl.ds(j, half, stride=S2_BLOCK), :] = (
                    gp[ri * half:(ri + 1) * half, j * HEAD_DIM:(j + 1) * HEAD_DIM])


def _stage2_pairs(gs_ref, t0_ref, tw_ref, f3_ref, y_ref, jb, n2):
    pairs = PAIR_BLOCKS * K1_PAIRS
    tr = t0_ref[0, 0]
    ts = t0_ref[0, 1]
    for p in range(pairs):
        if p:
            tr, ts = tr * tw_ref[0] - ts * tw_ref[1], tr * tw_ref[1] + ts * tw_ref[0]
        pr = []
        pi = []
        for hd in range(N_HEADS):
            wr = gs_ref[hd, 0, :, jb * pairs + p].reshape(n2, HEAD_DIM)
            wi = gs_ref[hd, 1, :, jb * pairs + p].reshape(n2, HEAD_DIM)
            gr = pltpu.bitcast(wr, BF16).astype(F32)
            gi = pltpu.bitcast(wi, BF16).astype(F32)
            pr.append(gr * tr + gi * ts)
            pi.append(gi * tr - gr * ts)
        x = jnp.concatenate([jnp.concatenate(pr, axis=1), jnp.concatenate(pi, axis=1)],
                            axis=0).astype(BF16)
        r = _dot(f3_ref[...], x)
        ru = pltpu.bitcast(r.astype(BF16), jnp.uint32)
        for hd in range(N_HEADS):
            out = y_ref.at[0, hd, p // K1_PAIRS].reshape(n2 * K1_PAIRS, HEAD_DIM)
            out[pl.ds(p % K1_PAIRS, n2, stride=K1_PAIRS), :] = ru[:, hd * HEAD_DIM:(hd + 1) * HEAD_DIM]


def _k12_kernel(x_ref, g1_ref, win_ref, w2_ref, f1_ref, t0_ref, tw_ref, f3_ref, y_ref, ab_ref, gs_ref,
                *, n1, n2, s2_step):
    j = pl.program_id(1)
    n_s1 = n2 // (s2_step * S2_BLOCK)

    @pl.when(j < n_s1)
    def _():
        for sb in range(s2_step):
            _stage1_group(x_ref, g1_ref, win_ref, w2_ref, f1_ref, gs_ref, ab_ref, sb, j * s2_step + sb, n1)

    @pl.when(j >= n_s1)
    def _():
        _stage2_pairs(gs_ref, t0_ref, tw_ref, f3_ref, y_ref, j - n_s1, n2)


def _dft_tables(s, n1):
    n2 = s // n1
    k1 = np.arange(n1)
    a1 = 2.0 * np.pi * np.outer(k1, k1) / n1
    c1, s1 = np.cos(a1), np.sin(a1)
    f1 = np.block([[c1, s1], [-s1, c1]]) / np.sqrt(n1)
    at = 2.0 * np.pi * np.outer(k1, np.arange(n2)) / s
    k2 = np.arange(n2)
    a2 = 2.0 * np.pi * np.outer(k2, k2) / n2
    f3 = np.concatenate([np.cos(a2), np.sin(a2)], axis=1) / np.sqrt(n2)
    f3p = np.einsum("krs,ef->kersf", f3.reshape(n2, 2, n2), np.eye(2)).reshape(2 * n2, 4 * n2)
    lanes = np.ones((1, 1, HEAD_DIM))
    atp = at.reshape(n1 // 2, 2, n2).transpose(0, 2, 1).reshape(n1 // 2, 2 * n2)
    at0 = atp[::PAIR_BLOCKS * K1_PAIRS]
    t0 = np.stack([np.cos(at0), np.sin(at0)], axis=1)[:, :, :, None] * lanes
    a_next = np.repeat(2.0 * np.pi * 2.0 * np.arange(n2) / s, 2)
    tw = np.stack([np.cos(a_next), np.sin(a_next)])[:, :, None] * lanes
    return (jnp.asarray(f1, F32).astype(BF16), jnp.asarray(t0, F32), jnp.asarray(tw, F32),
            jnp.asarray(f3p, F32).astype(BF16))


def _s2_step(s):
    spectrum = s * A_WIDTH * 4
    x_group = 2 * DFT_SPLIT * S2_BLOCK * D_MODEL * 4
    return max(1, min(MAX_S2_STEP, (VMEM_LIMIT_V7X - spectrum - K12_FIXED_VMEM) // x_group))


def _k12(x, g1, w_in_a, w2):
    b, s, _ = x.shape
    n1 = DFT_SPLIT
    n2 = s // n1
    half = n1 // 2
    nj = n2 // S2_BLOCK
    f1, t0, tw, f3 = _dft_tables(s, n1)
    s2_step = _s2_step(s)
    n_s1 = n2 // (s2_step * S2_BLOCK)
    n_s2 = half // (PAIR_BLOCKS * K1_PAIRS)
    return pl.pallas_call(
        functools.partial(_k12_kernel, n1=n1, n2=n2, s2_step=s2_step),
        grid=(b, n_s1 + n_s2),
        in_specs=[
            pl.BlockSpec((1, n1, s2_step * S2_BLOCK, D_MODEL),
                         lambda bi, j: (bi, 0, jnp.minimum(j, n_s1 - 1), 0)),
            _const_spec((1, D_MODEL)),
            _const_spec((D_MODEL, A_WIDTH)),
            _const_spec((N_HEADS, HEAD_DIM, 2 * HEAD_DIM)),
            _const_spec((2 * n1, 2 * n1)),
            pl.BlockSpec((1, 2, 2 * n2, HEAD_DIM), lambda bi, j: (jnp.maximum(j - n_s1, 0), 0, 0, 0)),
            _const_spec((2, 2 * n2, HEAD_DIM)),
            _const_spec((2 * n2, 4 * n2)),
        ],
        out_specs=pl.BlockSpec((1, N_HEADS, PAIR_BLOCKS, n2, K1_PAIRS, HEAD_DIM),
                               lambda bi, j: (bi, 0, jnp.maximum(j - n_s1, 0), 0, 0, 0)),
        out_shape=jax.ShapeDtypeStruct((b, N_HEADS, half // K1_PAIRS, n2, K1_PAIRS, HEAD_DIM), jnp.uint32),
        scratch_shapes=[
            pltpu.VMEM((2, 2, n1 * S2_BLOCK, HEAD_DIM), F32),
            pltpu.VMEM((N_HEADS, 2, nj, half, S2_BLOCK, HEAD_DIM), jnp.uint32),
        ],
        compiler_params=pltpu.CompilerParams(
            dimension_semantics=("parallel", "arbitrary"), vmem_limit_bytes=VMEM_LIMIT_V7X),
        name="k12_in_proj_dft",
    )(x.reshape(b, n1, n2, D_MODEL), g1, w_in_a, w2, f1, t0, tw, f3)


def _k3_kernel(x_ref, xp_ref, xn_ref, ya_ref, g1_ref, winb_ref, wouta_ref, wpo_ref, g2_ref,
               wg_ref, wu_ref, wd_ref, gf_ref, o_ref, hs_ref, zs_ref, act_ref, d_ref, h2_ref, *, seq_len):
    tm = x_ref.shape[1]
    th = tm // 2
    i = pl.program_id(1)
    last = pl.num_programs(1) - 1
    x = x_ref[0]
    g1 = g1_ref[...]
    g2 = g2_ref[...]
    gf = gf_ref[...]
    half_a = slice(0, th)
    half_b = slice(th, tm)

    ya = jnp.concatenate(
        [pltpu.bitcast(jnp.swapaxes(ya_ref[0, hd], 0, 1).reshape(tm // 2, HEAD_DIM), BF16)
         for hd in range(N_HEADS)], axis=-1)
    x1pb = x[half_b] + _dot(ya[half_b], wouta_ref[...])

    hs_ref[:HALO] = _rmsnorm(xp_ref[0], g1).astype(BF16)
    hs_ref[HALO:HALO + tm] = _rmsnorm(x, g1).astype(BF16)
    hs_ref[HALO + tm:] = _rmsnorm(xn_ref[0], g1).astype(BF16)
    zb = _dot(hs_ref[...], winb_ref[...])
    row = jax.lax.broadcasted_iota(jnp.int32, (tm + 2 * HALO, 1), 0)
    inside = ((row >= HALO) | (i > 0)) & ((row < HALO + tm) | (i < last))
    zb = jnp.where(inside, zb, 0.0)
    for gi in range(N_HEADS):
        zs_ref[gi] = zb[:, gi * HEAD_DIM:(gi + 1) * HEAD_DIM]

    def pool_rows(r0):
        n = VPU_SLICE
        t = i * tm + r0 + jax.lax.broadcasted_iota(jnp.int32, (n, 1), 0)
        d = []
        for gi, k in enumerate(POOL_WINDOWS):
            sums = zs_ref[gi, pl.ds(HALO + r0 - k // 2, n), :]
            for j in range(1, k):
                sums = sums + zs_ref[gi, pl.ds(HALO + r0 - k // 2 + j, n), :]
            cnt = jnp.minimum(t + k // 2, seq_len) - jnp.maximum(t - k // 2, 0)
            d.append((sums / cnt.astype(F32) - zs_ref[gi, pl.ds(HALO + r0, n), :]).astype(BF16))
        d_ref[pl.ds(r0, n), :] = jnp.concatenate(d, axis=-1)

    def ffn_chunk(rows, c):
        cols = slice(c * FF_CHUNK, (c + 1) * FF_CHUNK)
        h2 = h2_ref[rows, :]
        gate = _dot(h2, wg_ref[:, cols])
        up = _dot(h2, wu_ref[:, cols])
        act_ref[rows, cols] = (gate * jax.nn.sigmoid(gate) * up).astype(BF16)

    n_chunks = D_FF // FF_CHUNK
    n_slices = th // VPU_SLICE

    x1pa = []
    for q in range(PROJ_SPLIT):
        rows = slice(q * th // PROJ_SPLIT, (q + 1) * th // PROJ_SPLIT)
        x1pa.append(x[rows] + _dot(ya[rows], wouta_ref[...]))
        for s in range(q * n_slices // PROJ_SPLIT, (q + 1) * n_slices // PROJ_SPLIT):
            pool_rows(s * VPU_SLICE)
    x1a = jnp.concatenate(x1pa, axis=0) + _dot(d_ref[half_a, :], wpo_ref[...])
    h2_ref[half_a, :] = _rmsnorm(x1a, g2).astype(BF16)

    box = {}

    def proj_b():
        box["x1b"] = x1pb + _dot(d_ref[half_b, :], wpo_ref[...])

    def norm2_b(s):
        r = slice(s * VPU_SLICE, (s + 1) * VPU_SLICE)
        h2_ref[pl.ds(th + s * VPU_SLICE, VPU_SLICE), :] = _rmsnorm(box["x1b"][r], g2).astype(BF16)

    tasks = [functools.partial(pool_rows, th + s * VPU_SLICE) for s in range(n_slices)]
    tasks += [proj_b] + [functools.partial(norm2_b, s) for s in range(n_slices)]
    per_chunk = -(-len(tasks) // n_chunks)
    for c in range(n_chunks):
        ffn_chunk(half_a, c)
        for task in tasks[c * per_chunk:(c + 1) * per_chunk]:
            task()
    x2a = x1a + _dot(act_ref[half_a, :], wd_ref[...])

    def final_a(s):
        r = slice(s * VPU_SLICE, (s + 1) * VPU_SLICE)
        o_ref[0, r, :] = _rmsnorm(x2a[r], gf)

    tasks = [functools.partial(final_a, s) for s in range(n_slices)]
    per_chunk = -(-len(tasks) // n_chunks)
    for c in range(n_chunks):
        ffn_chunk(half_b, c)
        for task in tasks[c * per_chunk:(c + 1) * per_chunk]:
            task()
    x2b = box["x1b"] + _dot(act_ref[half_b, :], wd_ref[...])
    o_ref[0, half_b, :] = _rmsnorm(x2b, gf)


def _k3(x, ya, g1, w_in_b, w_out_a, w_po, g2, w_gate, w_up, w_down, gf):
    b, s, _ = x.shape
    tm = TOKEN_TILE
    hb = tm // HALO
    n_hb = s // HALO
    return pl.pallas_call(
        functools.partial(_k3_kernel, seq_len=s),
        grid=(b, s // tm),
        in_specs=[
            pl.BlockSpec((1, tm, D_MODEL), lambda bi, i: (bi, i, 0)),
            pl.BlockSpec((1, HALO, D_MODEL), lambda bi, i: (bi, jnp.maximum(i * hb - 1, 0), 0)),
            pl.BlockSpec((1, HALO, D_MODEL), lambda bi, i: (bi, jnp.minimum((i + 1) * hb, n_hb - 1), 0)),
            pl.BlockSpec((1, N_HEADS, DFT_SPLIT // (2 * K1_PAIRS), tm // DFT_SPLIT, K1_PAIRS, HEAD_DIM),
                         lambda bi, i: (bi, 0, 0, i, 0, 0)),
            _const_spec((1, D_MODEL)),
            _const_spec((D_MODEL, B_WIDTH)),
            _const_spec((A_WIDTH, D_MODEL)),
            _const_spec((B_WIDTH, D_MODEL)),
            _const_spec((1, D_MODEL)),
            _const_spec((D_MODEL, D_FF)),
            _const_spec((D_MODEL, D_FF)),
            _const_spec((D_FF, D_MODEL)),
            _const_spec((1, D_MODEL)),
        ],
        out_specs=pl.BlockSpec((1, tm, D_MODEL), lambda bi, i: (bi, i, 0)),
        out_shape=jax.ShapeDtypeStruct((b, s, D_MODEL), F32),
        scratch_shapes=[
            pltpu.VMEM((tm + 2 * HALO, D_MODEL), BF16),
            pltpu.VMEM((N_HEADS, tm + 2 * HALO, HEAD_DIM), F32),
            pltpu.VMEM((tm, D_FF), BF16),
            pltpu.VMEM((tm, B_WIDTH), BF16),
            pltpu.VMEM((tm, D_MODEL), BF16),
        ],
        compiler_params=pltpu.CompilerParams(
            dimension_semantics=("parallel", "parallel"), vmem_limit_bytes=VMEM_LIMIT_V7X),
        name="k3_pool_out_ffn",
    )(x, x, x, ya, g1, w_in_b, w_out_a, w_po, g2, w_gate, w_up, w_down, gf)


def kernel(x_prompt, x_sample, norm1_g, w_in, w_fourier, w_pool, pool_scale, w_out, norm2_g,
           w_gate, w_up, w_down, normf_g):
    assert norm1_g.shape[0] == 1, "single-layer block"
    g1 = norm1_g[0][None]
    g2 = norm2_g[0][None]
    gf = normf_g[None]
    w_in_a = w_in[0, :, :A_WIDTH].astype(BF16)
    w_in_b = w_in[0, :, A_WIDTH:].astype(BF16)
    w_o_a = w_out[0, :A_WIDTH].astype(BF16)
    w_g = w_gate[0].astype(BF16)
    w_u = w_up[0].astype(BF16)
    w_d = w_down[0].astype(BF16)
    w2, w_po = _prep(w_fourier[0], w_pool[0], pool_scale[0][None], w_out[0])

    def trunk(x):
        ya = _k12(x, g1, w_in_a, w2)
        return _k3(x, ya, g1, w_in_b, w_o_a, w_po, g2, w_g, w_u, w_d, gf)

    y_prompt = trunk(x_prompt)
    y_prompt, x_sample = jax.lax.optimization_barrier((y_prompt, x_sample))
    return y_prompt, trunk(x_sample)
```

```python
import functools

import jax
import jax.numpy as jnp
import numpy as np
from jax.experimental import pallas as pl
from jax.experimental.pallas import tpu as pltpu

F32 = jnp.float32
BF16 = jnp.bfloat16

D_MODEL = 1024
N_HEADS = 4
HEAD_DIM = 128
A_WIDTH = N_HEADS * HEAD_DIM
B_WIDTH = N_HEADS * HEAD_DIM
POOL_WINDOWS = (2, 4, 8, 16)
D_FF = 2816
EPS = 1e-6

HALO = 16
TOKEN_TILE = 1024
FF_CHUNK = 256
VPU_SLICE = 64
PROJ_SPLIT = 2
DFT_SPLIT = 128
S2_BLOCK = 8
MAX_S2_STEP = 2
K1_PAIRS = 8
MAX_PAIR_BLOCKS = 2
K12_FIXED_VMEM = 7 * 1024 * 1024
VMEM_LIMIT_V7X = 58 * 1024 * 1024


def _const_spec(shape):
    zeros = (0,) * len(shape)
    return pl.BlockSpec(shape, lambda *_: zeros, pipeline_mode=pl.Buffered(1))


def _rmsnorm(x, g):
    r = jax.lax.rsqrt(jnp.mean(x * x, axis=-1, keepdims=True) + EPS)
    return x * r * g


def _dot(a, b):
    return jnp.dot(a, b, preferred_element_type=F32)


def _split_bf16(a):
    hi = a.astype(BF16)
    lo = (a - hi.astype(F32)).astype(BF16)
    return hi, lo


def _dot_f32(a, b):
    ah, al = _split_bf16(a)
    bh, bl = _split_bf16(b)
    return _dot(ah, bh) + (_dot(ah, bl) + _dot(al, bh))


def _prep_kernel(cs_ref, wf_ref, wp_ref, ps_ref, wout_ref, w2_ref, wpo_ref):
    for h in range(N_HEADS):
        w = wf_ref[h]
        w2_ref[h, :, :HEAD_DIM] = _dot_f32(cs_ref[0], w).astype(BF16)
        w2_ref[h, :, HEAD_DIM:] = (-_dot_f32(cs_ref[1], w)).astype(BF16)
    for g in range(N_HEADS):
        lanes = slice(g * HEAD_DIM, (g + 1) * HEAD_DIM)
        rows = slice(A_WIDTH + g * HEAD_DIM, A_WIDTH + (g + 1) * HEAD_DIM)
        wpo_ref[lanes, :] = _dot_f32(wp_ref[g] * ps_ref[:, lanes], wout_ref[rows, :]).astype(BF16)


def _prep(w_fourier, w_pool, pool_scale, w_out):
    n = np.arange(HEAD_DIM)
    ang = 2.0 * np.pi * np.outer(n, n) / HEAD_DIM
    cs = np.stack([np.cos(ang), np.sin(ang)]) / np.sqrt(HEAD_DIM)
    return pl.pallas_call(
        _prep_kernel,
        out_shape=(jax.ShapeDtypeStruct((N_HEADS, HEAD_DIM, 2 * HEAD_DIM), BF16),
                   jax.ShapeDtypeStruct((B_WIDTH, D_MODEL), BF16)),
        name="prep_weights",
    )(jnp.asarray(cs, F32), w_fourier, w_pool, pool_scale, w_out)


def _stage1_group(x_ref, g1_ref, win_ref, w2_ref, f1_ref, gs_ref, ab_ref, sb, jg, n1):
    rows = n1 * S2_BLOCK
    half = n1 // 2
    x = x_ref[0, :, sb * S2_BLOCK:(sb + 1) * S2_BLOCK, :].reshape(rows, D_MODEL)
    h = _rmsnorm(x, g1_ref[...]).astype(BF16)
    za = _dot(h, win_ref[...]).astype(BF16)
    for hd in range(N_HEADS):
        ab = _dot(za[:, hd * HEAD_DIM:(hd + 1) * HEAD_DIM], w2_ref[hd])
        buf = ab_ref.at[hd % 2]
        buf[0] = ab[:, :HEAD_DIM]
        buf[1] = ab[:, HEAD_DIM:]
        xx = jnp.concatenate(
            [jnp.concatenate([buf.at[ri][pl.ds(j, n1, stride=S2_BLOCK), :]
                              for j in range(S2_BLOCK)], axis=1) for ri in range(2)],
            axis=0).astype(BF16)
        g = _dot(f1_ref[...], xx)
        gp = pltpu.bitcast(g.astype(BF16), jnp.uint32)
        for ri in range(2):
            out = gs_ref.at[hd, ri, jg].reshape(half * S2_BLOCK, HEAD_DIM)
            for j in range(S2_BLOCK):
                out[pl.ds(j, half, stride=S2_BLOCK), :] = (
                    gp[ri * half:(ri + 1) * half, j * HEAD_DIM:(j + 1) * HEAD_DIM])


def _stage2_pairs(gs_ref, t0_ref, tw_ref, f3_ref, y_ref, jb, n2, pair_blocks):
    pairs = pair_blocks * K1_PAIRS
    tr = t0_ref[0, 0]
    ts = t0_ref[0, 1]
    for p in range(pairs):
        if p:
            tr, ts = tr * tw_ref[0] - ts * tw_ref[1], tr * tw_ref[1] + ts * tw_ref[0]
        pr = []
        pi = []
        for hd in range(N_HEADS):
            wr = gs_ref[hd, 0, :, jb * pairs + p].reshape(n2, HEAD_DIM)
            wi = gs_ref[hd, 1, :, jb * pairs + p].reshape(n2, HEAD_DIM)
            gr = pltpu.bitcast(wr, BF16).astype(F32)
            gi = pltpu.bitcast(wi, BF16).astype(F32)
            pr.append(gr * tr + gi * ts)
            pi.append(gi * tr - gr * ts)
        x = jnp.concatenate([jnp.concatenate(pr, axis=1), jnp.concatenate(pi, axis=1)],
                            axis=0).astype(BF16)
        r = _dot(f3_ref[...], x)
        ru = pltpu.bitcast(r.astype(BF16), jnp.uint32)
        for hd in range(N_HEADS):
            out = y_ref.at[0, hd, p // K1_PAIRS].reshape(n2 * K1_PAIRS, HEAD_DIM)
            out[pl.ds(p % K1_PAIRS, n2, stride=K1_PAIRS), :] = ru[:, hd * HEAD_DIM:(hd + 1) * HEAD_DIM]


def _k12_kernel(x_ref, g1_ref, win_ref, w2_ref, f1_ref, t0_ref, tw_ref, f3_ref, *refs,
                n1, n2, s2_step, pair_blocks, n_cast):
    cast_in, y_ref, cast_out = refs[:n_cast], refs[n_cast], refs[n_cast + 1:2 * n_cast + 1]
    ab_ref, gs_ref = refs[2 * n_cast + 1:]
    j = pl.program_id(1)
    n_s1 = n2 // (s2_step * S2_BLOCK)

    @pl.when(j < n_s1)
    def _():
        for sb in range(s2_step):
            _stage1_group(x_ref, g1_ref, win_ref, w2_ref, f1_ref, gs_ref, ab_ref, sb, j * s2_step + sb, n1)
        for src, dst in zip(cast_in, cast_out):
            dst[...] = src[...].astype(BF16)

    @pl.when(j >= n_s1)
    def _():
        _stage2_pairs(gs_ref, t0_ref, tw_ref, f3_ref, y_ref, j - n_s1, n2, pair_blocks)


def _dft_tables(s, n1, pair_blocks):
    n2 = s // n1
    k1 = np.arange(n1)
    a1 = 2.0 * np.pi * np.outer(k1, k1) / n1
    c1, s1 = np.cos(a1), np.sin(a1)
    f1 = np.block([[c1, s1], [-s1, c1]]) / np.sqrt(n1)
    at = 2.0 * np.pi * np.outer(k1, np.arange(n2)) / s
    k2 = np.arange(n2)
    a2 = 2.0 * np.pi * np.outer(k2, k2) / n2
    f3 = np.concatenate([np.cos(a2), np.sin(a2)], axis=1) / np.sqrt(n2)
    f3p = np.einsum("krs,ef->kersf", f3.reshape(n2, 2, n2), np.eye(2)).reshape(2 * n2, 4 * n2)
    lanes = np.ones((1, 1, HEAD_DIM))
    atp = at.reshape(n1 // 2, 2, n2).transpose(0, 2, 1).reshape(n1 // 2, 2 * n2)
    at0 = atp[::pair_blocks * K1_PAIRS]
    t0 = np.stack([np.cos(at0), np.sin(at0)], axis=1)[:, :, :, None] * lanes
    a_next = np.repeat(2.0 * np.pi * 2.0 * np.arange(n2) / s, 2)
    tw = np.stack([np.cos(a_next), np.sin(a_next)])[:, :, None] * lanes
    return (jnp.asarray(f1, F32).astype(BF16), jnp.asarray(t0, F32), jnp.asarray(tw, F32),
            jnp.asarray(f3p, F32).astype(BF16))


def _k12_plan(s, cast):
    n2 = s // DFT_SPLIT
    spectrum = s * A_WIDTH * 4
    x_group = 2 * DFT_SPLIT * S2_BLOCK * D_MODEL * 4
    y_block = 2 * N_HEADS * n2 * K1_PAIRS * HEAD_DIM * 4
    for s2_step in range(MAX_S2_STEP, 0, -1):
        n_s1 = n2 // (s2_step * S2_BLOCK)
        cast_bytes = sum(2 * (4 + 2) * w.size // n_s1 for w in cast)
        for pair_blocks in range(MAX_PAIR_BLOCKS, 0, -1):
            need = spectrum + K12_FIXED_VMEM + s2_step * x_group + pair_blocks * y_block + cast_bytes
            if need <= VMEM_LIMIT_V7X:
                return s2_step, pair_blocks
    raise ValueError(f"k12 does not fit VMEM for sequence length {s}")


def _k12(x, g1, w_in_a, w2, cast=()):
    b, s, _ = x.shape
    n1 = DFT_SPLIT
    n2 = s // n1
    half = n1 // 2
    nj = n2 // S2_BLOCK
    s2_step, pair_blocks = _k12_plan(s, cast)
    f1, t0, tw, f3 = _dft_tables(s, n1, pair_blocks)
    n_s1 = n2 // (s2_step * S2_BLOCK)
    n_s2 = half // (pair_blocks * K1_PAIRS)
    cast_rows = [w.shape[0] // n_s1 for w in cast]
    assert all(r * n_s1 == w.shape[0] and r % 16 == 0 for r, w in zip(cast_rows, cast)), cast_rows

    def cast_spec(r, w):
        return pl.BlockSpec((r, w.shape[1]), lambda bi, j: (jnp.minimum(j, n_s1 - 1), 0))

    y_spec = pl.BlockSpec((1, N_HEADS, pair_blocks, n2, K1_PAIRS, HEAD_DIM),
                          lambda bi, j: (bi, 0, jnp.maximum(j - n_s1, 0), 0, 0, 0))
    y_shape = jax.ShapeDtypeStruct((b, N_HEADS, half // K1_PAIRS, n2, K1_PAIRS, HEAD_DIM), jnp.uint32)
    return pl.pallas_call(
        functools.partial(_k12_kernel, n1=n1, n2=n2, s2_step=s2_step, pair_blocks=pair_blocks,
                          n_cast=len(cast)),
        grid=(b, n_s1 + n_s2),
        in_specs=[
            pl.BlockSpec((1, n1, s2_step * S2_BLOCK, D_MODEL),
                         lambda bi, j: (bi, 0, jnp.minimum(j, n_s1 - 1), 0)),
            _const_spec((1, D_MODEL)),
            _const_spec((D_MODEL, A_WIDTH)),
            _const_spec((N_HEADS, HEAD_DIM, 2 * HEAD_DIM)),
            _const_spec((2 * n1, 2 * n1)),
            pl.BlockSpec((1, 2, 2 * n2, HEAD_DIM), lambda bi, j: (jnp.maximum(j - n_s1, 0), 0, 0, 0)),
            _const_spec((2, 2 * n2, HEAD_DIM)),
            _const_spec((2 * n2, 4 * n2)),
        ] + [cast_spec(r, w) for r, w in zip(cast_rows, cast)],
        out_specs=[y_spec] + [cast_spec(r, w) for r, w in zip(cast_rows, cast)],
        out_shape=[y_shape] + [jax.ShapeDtypeStruct(w.shape, BF16) for w in cast],
        scratch_shapes=[
            pltpu.VMEM((2, 2, n1 * S2_BLOCK, HEAD_DIM), F32),
            pltpu.VMEM((N_HEADS, 2, nj, half, S2_BLOCK, HEAD_DIM), jnp.uint32),
        ],
        compiler_params=pltpu.CompilerParams(
            dimension_semantics=("parallel", "arbitrary"), vmem_limit_bytes=VMEM_LIMIT_V7X),
        name="k12_in_proj_dft",
    )(x.reshape(b, n1, n2, D_MODEL), g1, w_in_a, w2, f1, t0, tw, f3, *cast)


def _k3_kernel(x_ref, xp_ref, xn_ref, ya_ref, g1_ref, winb_ref, wouta_ref, wpo_ref, g2_ref,
               wg_ref, wu_ref, wd_ref, gf_ref, o_ref, hs_ref, zs_ref, act_ref, d_ref, h2_ref, *, seq_len):
    tm = x_ref.shape[1]
    th = tm // 2
    i = pl.program_id(1)
    last = pl.num_programs(1) - 1
    x = x_ref[0]
    g1 = g1_ref[...]
    g2 = g2_ref[...]
    gf = gf_ref[...]
    half_a = slice(0, th)
    half_b = slice(th, tm)

    ya = jnp.concatenate(
        [pltpu.bitcast(jnp.swapaxes(ya_ref[0, hd], 0, 1).reshape(tm // 2, HEAD_DIM), BF16)
         for hd in range(N_HEADS)], axis=-1)
    x1pb = x[half_b] + _dot(ya[half_b], wouta_ref[...])

    hs_ref[:HALO] = _rmsnorm(xp_ref[0], g1).astype(BF16)
    hs_ref[HALO:HALO + tm] = _rmsnorm(x, g1).astype(BF16)
    hs_ref[HALO + tm:] = _rmsnorm(xn_ref[0], g1).astype(BF16)
    zb = _dot(hs_ref[...], winb_ref[...])
    row = jax.lax.broadcasted_iota(jnp.int32, (tm + 2 * HALO, 1), 0)
    inside = ((row >= HALO) | (i > 0)) & ((row < HALO + tm) | (i < last))
    zb = jnp.where(inside, zb, 0.0)
    for gi in range(N_HEADS):
        zs_ref[gi] = zb[:, gi * HEAD_DIM:(gi + 1) * HEAD_DIM]

    def pool_rows(r0):
        n = VPU_SLICE
        t = i * tm + r0 + jax.lax.broadcasted_iota(jnp.int32, (n, 1), 0)
        d = []
        for gi, k in enumerate(POOL_WINDOWS):
            sums = zs_ref[gi, pl.ds(HALO + r0 - k // 2, n), :]
            for j in range(1, k):
                sums = sums + zs_ref[gi, pl.ds(HALO + r0 - k // 2 + j, n), :]
            cnt = jnp.minimum(t + k // 2, seq_len) - jnp.maximum(t - k // 2, 0)
            d.append((sums / cnt.astype(F32) - zs_ref[gi, pl.ds(HALO + r0, n), :]).astype(BF16))
        d_ref[pl.ds(r0, n), :] = jnp.concatenate(d, axis=-1)

    def ffn_chunk(rows, c):
        cols = slice(c * FF_CHUNK, (c + 1) * FF_CHUNK)
        h2 = h2_ref[rows, :]
        gate = _dot(h2, wg_ref[:, cols])
        up = _dot(h2, wu_ref[:, cols])
        act_ref[rows, cols] = (gate * jax.nn.sigmoid(gate) * up).astype(BF16)

    n_chunks = D_FF // FF_CHUNK
    n_slices = th // VPU_SLICE

    x1pa = []
    for q in range(PROJ_SPLIT):
        rows = slice(q * th // PROJ_SPLIT, (q + 1) * th // PROJ_SPLIT)
        x1pa.append(x[rows] + _dot(ya[rows], wouta_ref[...]))
        for s in range(q * n_slices // PROJ_SPLIT, (q + 1) * n_slices // PROJ_SPLIT):
            pool_rows(s * VPU_SLICE)
    x1a = jnp.concatenate(x1pa, axis=0) + _dot(d_ref[half_a, :], wpo_ref[...])
    h2_ref[half_a, :] = _rmsnorm(x1a, g2).astype(BF16)

    box = {}

    def proj_b():
        box["x1b"] = x1pb + _dot(d_ref[half_b, :], wpo_ref[...])

    def norm2_b(s):
        r = slice(s * VPU_SLICE, (s + 1) * VPU_SLICE)
        h2_ref[pl.ds(th + s * VPU_SLICE, VPU_SLICE), :] = _rmsnorm(box["x1b"][r], g2).astype(BF16)

    tasks = [functools.partial(pool_rows, th + s * VPU_SLICE) for s in range(n_slices)]
    tasks += [proj_b] + [functools.partial(norm2_b, s) for s in range(n_slices)]
    per_chunk = -(-len(tasks) // n_chunks)
    for c in range(n_chunks):
        ffn_chunk(half_a, c)
        for task in tasks[c * per_chunk:(c + 1) * per_chunk]:
            task()
    x2a = x1a + _dot(act_ref[half_a, :], wd_ref[...])

    def final_a(s):
        r = slice(s * VPU_SLICE, (s + 1) * VPU_SLICE)
        o_ref[0, r, :] = _rmsnorm(x2a[r], gf)

    tasks = [functools.partial(final_a, s) for s in range(n_slices)]
    per_chunk = -(-len(tasks) // n_chunks)
    for c in range(n_chunks):
        ffn_chunk(half_b, c)
        for task in tasks[c * per_chunk:(c + 1) * per_chunk]:
            task()
    x2b = box["x1b"] + _dot(act_ref[half_b, :], wd_ref[...])
    o_ref[0, half_b, :] = _rmsnorm(x2b, gf)


def _k3(x, ya, g1, w_in_b, w_out_a, w_po, g2, w_gate, w_up, w_down, gf):
    b, s, _ = x.shape
    tm = TOKEN_TILE
    hb = tm // HALO
    n_hb = s // HALO
    return pl.pallas_call(
        functools.partial(_k3_kernel, seq_len=s),
        grid=(b, s // tm),
        in_specs=[
            pl.BlockSpec((1, tm, D_MODEL), lambda bi, i: (bi, i, 0)),
            pl.BlockSpec((1, HALO, D_MODEL), lambda bi, i: (bi, jnp.maximum(i * hb - 1, 0), 0)),
            pl.BlockSpec((1, HALO, D_MODEL), lambda bi, i: (bi, jnp.minimum((i + 1) * hb, n_hb - 1), 0)),
            pl.BlockSpec((1, N_HEADS, DFT_SPLIT // (2 * K1_PAIRS), tm // DFT_SPLIT, K1_PAIRS, HEAD_DIM),
                         lambda bi, i: (bi, 0, 0, i, 0, 0)),
            _const_spec((1, D_MODEL)),
            _const_spec((D_MODEL, B_WIDTH)),
            _const_spec((A_WIDTH, D_MODEL)),
            _const_spec((B_WIDTH, D_MODEL)),
            _const_spec((1, D_MODEL)),
            _const_spec((D_MODEL, D_FF)),
            _const_spec((D_MODEL, D_FF)),
            _const_spec((D_FF, D_MODEL)),
            _const_spec((1, D_MODEL)),
        ],
        out_specs=pl.BlockSpec((1, tm, D_MODEL), lambda bi, i: (bi, i, 0)),
        out_shape=jax.ShapeDtypeStruct((b, s, D_MODEL), F32),
        scratch_shapes=[
            pltpu.VMEM((tm + 2 * HALO, D_MODEL), BF16),
            pltpu.VMEM((N_HEADS, tm + 2 * HALO, HEAD_DIM), F32),
            pltpu.VMEM((tm, D_FF), BF16),
            pltpu.VMEM((tm, B_WIDTH), BF16),
            pltpu.VMEM((tm, D_MODEL), BF16),
        ],
        compiler_params=pltpu.CompilerParams(
            dimension_semantics=("parallel", "parallel"), vmem_limit_bytes=VMEM_LIMIT_V7X),
        name="k3_pool_out_ffn",
    )(x, x, x, ya, g1, w_in_b, w_out_a, w_po, g2, w_gate, w_up, w_down, gf)


def kernel(x_prompt, x_sample, norm1_g, w_in, w_fourier, w_pool, pool_scale, w_out, norm2_g,
           w_gate, w_up, w_down, normf_g):
    assert norm1_g.shape[0] == 1, "single-layer block"
    g1 = norm1_g[0][None]
    g2 = norm2_g[0][None]
    gf = normf_g[None]
    w_in_a = w_in[0, :, :A_WIDTH].astype(BF16)
    w_in_b = w_in[0, :, A_WIDTH:].astype(BF16)
    w_o_a = w_out[0, :A_WIDTH].astype(BF16)
    w2, w_po = _prep(w_fourier[0], w_pool[0], pool_scale[0][None], w_out[0])

    ya, w_g, w_u, w_d = _k12(x_prompt, g1, w_in_a, w2, cast=(w_gate[0], w_up[0], w_down[0]))
    y_prompt = _k3(x_prompt, ya, g1, w_in_b, w_o_a, w_po, g2, w_g, w_u, w_d, gf)
    y_prompt, x_sample = jax.lax.optimization_barrier((y_prompt, x_sample))
    (ya,) = _k12(x_sample, g1, w_in_a, w2)
    return y_prompt, _k3(x_sample, ya, g1, w_in_b, w_o_a, w_po, g2, w_g, w_u, w_d, gf)
```

```python
import functools

import jax
import jax.numpy as jnp
import numpy as np
from jax.experimental import pallas as pl
from jax.experimental.pallas import tpu as pltpu

F32 = jnp.float32
BF16 = jnp.bfloat16

D_MODEL = 1024
N_HEADS = 4
HEAD_DIM = 128
A_WIDTH = N_HEADS * HEAD_DIM
B_WIDTH = N_HEADS * HEAD_DIM
POOL_WINDOWS = (2, 4, 8, 16)
D_FF = 2816
EPS = 1e-6

HALO = 16
TOKEN_TILE = 1024
FF_CHUNK = 256
VPU_SLICE = 64
PROJ_SPLIT = 2
DFT_SPLIT = 128
S2_BLOCK = 8
MAX_S2_STEP = 2
K1_PAIRS = 8
MAX_PAIR_BLOCKS = 2
K12_FIXED_VMEM = 7 * 1024 * 1024
VMEM_LIMIT_V7X = 58 * 1024 * 1024


def _const_spec(shape):
    zeros = (0,) * len(shape)
    return pl.BlockSpec(shape, lambda *_: zeros, pipeline_mode=pl.Buffered(1))


def _rmsnorm(x, g):
    r = jax.lax.rsqrt(jnp.mean(x * x, axis=-1, keepdims=True) + EPS)
    return x * r * g


def _dot(a, b):
    return jnp.dot(a, b, preferred_element_type=F32)


def _split_bf16(a):
    hi = a.astype(BF16)
    lo = (a - hi.astype(F32)).astype(BF16)
    return hi, lo


def _dot_f32(a, b):
    ah, al = _split_bf16(a)
    bh, bl = _split_bf16(b)
    return _dot(ah, bh) + (_dot(ah, bl) + _dot(al, bh))


def _prep_kernel(cs_ref, wf_ref, wp_ref, ps_ref, wout_ref, win_ref, w2_ref, wpo_ref, wina_ref, winb_ref,
                 wouta_ref):
    wina_ref[...] = win_ref[:, :A_WIDTH].astype(BF16)
    winb_ref[...] = win_ref[:, A_WIDTH:].astype(BF16)
    wouta_ref[...] = wout_ref[:A_WIDTH, :].astype(BF16)
    for h in range(N_HEADS):
        w = wf_ref[h]
        w2_ref[h, :, :HEAD_DIM] = _dot_f32(cs_ref[0], w).astype(BF16)
        w2_ref[h, :, HEAD_DIM:] = (-_dot_f32(cs_ref[1], w)).astype(BF16)
    for g in range(N_HEADS):
        lanes = slice(g * HEAD_DIM, (g + 1) * HEAD_DIM)
        rows = slice(A_WIDTH + g * HEAD_DIM, A_WIDTH + (g + 1) * HEAD_DIM)
        wpo_ref[lanes, :] = _dot_f32(wp_ref[g] * ps_ref[:, lanes], wout_ref[rows, :]).astype(BF16)


def _prep(w_fourier, w_pool, pool_scale, w_out, w_in):
    n = np.arange(HEAD_DIM)
    ang = 2.0 * np.pi * np.outer(n, n) / HEAD_DIM
    cs = np.stack([np.cos(ang), np.sin(ang)]) / np.sqrt(HEAD_DIM)
    return pl.pallas_call(
        _prep_kernel,
        out_shape=(jax.ShapeDtypeStruct((N_HEADS, HEAD_DIM, 2 * HEAD_DIM), BF16),
                   jax.ShapeDtypeStruct((B_WIDTH, D_MODEL), BF16),
                   jax.ShapeDtypeStruct((D_MODEL, A_WIDTH), BF16),
                   jax.ShapeDtypeStruct((D_MODEL, B_WIDTH), BF16),
                   jax.ShapeDtypeStruct((A_WIDTH, D_MODEL), BF16)),
        name="prep_weights",
    )(jnp.asarray(cs, F32), w_fourier, w_pool, pool_scale, w_out, w_in)


def _stage1_group(x_ref, g1_ref, win_ref, w2_ref, f1_ref, gs_ref, ab_ref, sb, jg, n1):
    rows = n1 * S2_BLOCK
    half = n1 // 2
    x = x_ref[0, :, sb * S2_BLOCK:(sb + 1) * S2_BLOCK, :].reshape(rows, D_MODEL)
    h = _rmsnorm(x, g1_ref[...]).astype(BF16)
    za = _dot(h, win_ref[...]).astype(BF16)
    for hd in range(N_HEADS):
        ab = _dot(za[:, hd * HEAD_DIM:(hd + 1) * HEAD_DIM], w2_ref[hd])
        buf = ab_ref.at[hd % 2]
        buf[0] = ab[:, :HEAD_DIM]
        buf[1] = ab[:, HEAD_DIM:]
        xx = jnp.concatenate(
            [jnp.concatenate([buf.at[ri][pl.ds(j, n1, stride=S2_BLOCK), :]
                              for j in range(S2_BLOCK)], axis=1) for ri in range(2)],
            axis=0).astype(BF16)
        g = _dot(f1_ref[...], xx)
        gp = pltpu.bitcast(g.astype(BF16), jnp.uint32)
        for ri in range(2):
            out = gs_ref.at[hd, ri, jg].reshape(half * S2_BLOCK, HEAD_DIM)
            for j in range(S2_BLOCK):
                out[pl.ds(j, half, stride=S2_BLOCK), :] = (
                    gp[ri * half:(ri + 1) * half, j * HEAD_DIM:(j + 1) * HEAD_DIM])


def _stage2_pairs(gs_ref, t0_ref, tw_ref, f3_ref, y_ref, jb, n2, pair_blocks):
    pairs = pair_blocks * K1_PAIRS
    tr = t0_ref[0, 0]
    ts = t0_ref[0, 1]
    for p in range(pairs):
        if p:
            tr, ts = tr * tw_ref[0] - ts * tw_ref[1], tr * tw_ref[1] + ts * tw_ref[0]
        pr = []
        pi = []
        for hd in range(N_HEADS):
            wr = gs_ref[hd, 0, :, jb * pairs + p].reshape(n2, HEAD_DIM)
            wi = gs_ref[hd, 1, :, jb * pairs + p].reshape(n2, HEAD_DIM)
            gr = pltpu.bitcast(wr, BF16).astype(F32)
            gi = pltpu.bitcast(wi, BF16).astype(F32)
            pr.append(gr * tr + gi * ts)
            pi.append(gi * tr - gr * ts)
        x = jnp.concatenate([jnp.concatenate(pr, axis=1), jnp.concatenate(pi, axis=1)],
                            axis=0).astype(BF16)
        r = _dot(f3_ref[...], x)
        ru = pltpu.bitcast(r.astype(BF16), jnp.uint32)
        for hd in range(N_HEADS):
            out = y_ref.at[0, hd, p // K1_PAIRS].reshape(n2 * K1_PAIRS, HEAD_DIM)
            out[pl.ds(p % K1_PAIRS, n2, stride=K1_PAIRS), :] = ru[:, hd * HEAD_DIM:(hd + 1) * HEAD_DIM]


def _k12_kernel(x_ref, g1_ref, win_ref, w2_ref, f1_ref, t0_ref, tw_ref, f3_ref, *refs,
                n1, n2, s2_step, pair_blocks, n_cast):
    cast_in, y_ref, cast_out = refs[:n_cast], refs[n_cast], refs[n_cast + 1:2 * n_cast + 1]
    ab_ref, gs_ref = refs[2 * n_cast + 1:]
    j = pl.program_id(1)
    n_s1 = n2 // (s2_step * S2_BLOCK)

    @pl.when(j < n_s1)
    def _():
        for sb in range(s2_step):
            _stage1_group(x_ref, g1_ref, win_ref, w2_ref, f1_ref, gs_ref, ab_ref, sb, j * s2_step + sb, n1)
        for src, dst in zip(cast_in, cast_out):
            dst[...] = src[...].astype(BF16)

    @pl.when(j >= n_s1)
    def _():
        _stage2_pairs(gs_ref, t0_ref, tw_ref, f3_ref, y_ref, j - n_s1, n2, pair_blocks)


def _dft_tables(s, n1, pair_blocks):
    n2 = s // n1
    k1 = np.arange(n1)
    a1 = 2.0 * np.pi * np.outer(k1, k1) / n1
    c1, s1 = np.cos(a1), np.sin(a1)
    f1 = np.block([[c1, s1], [-s1, c1]]) / np.sqrt(n1)
    at = 2.0 * np.pi * np.outer(k1, np.arange(n2)) / s
    k2 = np.arange(n2)
    a2 = 2.0 * np.pi * np.outer(k2, k2) / n2
    f3 = np.concatenate([np.cos(a2), np.sin(a2)], axis=1) / np.sqrt(n2)
    f3p = np.einsum("krs,ef->kersf", f3.reshape(n2, 2, n2), np.eye(2)).reshape(2 * n2, 4 * n2)
    lanes = np.ones((1, 1, HEAD_DIM))
    atp = at.reshape(n1 // 2, 2, n2).transpose(0, 2, 1).reshape(n1 // 2, 2 * n2)
    at0 = atp[::pair_blocks * K1_PAIRS]
    t0 = np.stack([np.cos(at0), np.sin(at0)], axis=1)[:, :, :, None] * lanes
    a_next = np.repeat(2.0 * np.pi * 2.0 * np.arange(n2) / s, 2)
    tw = np.stack([np.cos(a_next), np.sin(a_next)])[:, :, None] * lanes
    return (jnp.asarray(f1, F32).astype(BF16), jnp.asarray(t0, F32), jnp.asarray(tw, F32),
            jnp.asarray(f3p, F32).astype(BF16))


def _k12_plan(s, cast):
    n2 = s // DFT_SPLIT
    spectrum = s * A_WIDTH * 4
    x_group = 2 * DFT_SPLIT * S2_BLOCK * D_MODEL * 4
    y_block = 2 * N_HEADS * n2 * K1_PAIRS * HEAD_DIM * 4
    for s2_step in range(MAX_S2_STEP, 0, -1):
        n_s1 = n2 // (s2_step * S2_BLOCK)
        cast_bytes = sum(2 * (4 + 2) * w.size // n_s1 for w in cast)
        for pair_blocks in range(MAX_PAIR_BLOCKS, 0, -1):
            need = spectrum + K12_FIXED_VMEM + s2_step * x_group + pair_blocks * y_block + cast_bytes
            if need <= VMEM_LIMIT_V7X:
                return s2_step, pair_blocks
    raise ValueError(f"k12 does not fit VMEM for sequence length {s}")


def _k12(x, g1, w_in_a, w2, cast=()):
    b, s, _ = x.shape
    n1 = DFT_SPLIT
    n2 = s // n1
    half = n1 // 2
    nj = n2 // S2_BLOCK
    s2_step, pair_blocks = _k12_plan(s, cast)
    f1, t0, tw, f3 = _dft_tables(s, n1, pair_blocks)
    n_s1 = n2 // (s2_step * S2_BLOCK)
    n_s2 = half // (pair_blocks * K1_PAIRS)
    cast_rows = [w.shape[0] // n_s1 for w in cast]
    assert all(r * n_s1 == w.shape[0] and r % 16 == 0 for r, w in zip(cast_rows, cast)), cast_rows

    def cast_spec(r, w):
        return pl.BlockSpec((r, w.shape[1]), lambda bi, j: (jnp.minimum(j, n_s1 - 1), 0))

    y_spec = pl.BlockSpec((1, N_HEADS, pair_blocks, n2, K1_PAIRS, HEAD_DIM),
                          lambda bi, j: (bi, 0, jnp.maximum(j - n_s1, 0), 0, 0, 0))
    y_shape = jax.ShapeDtypeStruct((b, N_HEADS, half // K1_PAIRS, n2, K1_PAIRS, HEAD_DIM), jnp.uint32)
    return pl.pallas_call(
        functools.partial(_k12_kernel, n1=n1, n2=n2, s2_step=s2_step, pair_blocks=pair_blocks,
                          n_cast=len(cast)),
        grid=(b, n_s1 + n_s2),
        in_specs=[
            pl.BlockSpec((1, n1, s2_step * S2_BLOCK, D_MODEL),
                         lambda bi, j: (bi, 0, jnp.minimum(j, n_s1 - 1), 0)),
            _const_spec((1, D_MODEL)),
            _const_spec((D_MODEL, A_WIDTH)),
            _const_spec((N_HEADS, HEAD_DIM, 2 * HEAD_DIM)),
            _const_spec((2 * n1, 2 * n1)),
            pl.BlockSpec((1, 2, 2 * n2, HEAD_DIM), lambda bi, j: (jnp.maximum(j - n_s1, 0), 0, 0, 0)),
            _const_spec((2, 2 * n2, HEAD_DIM)),
            _const_spec((2 * n2, 4 * n2)),
        ] + [cast_spec(r, w) for r, w in zip(cast_rows, cast)],
        out_specs=[y_spec] + [cast_spec(r, w) for r, w in zip(cast_rows, cast)],
        out_shape=[y_shape] + [jax.ShapeDtypeStruct(w.shape, BF16) for w in cast],
        scratch_shapes=[
            pltpu.VMEM((2, 2, n1 * S2_BLOCK, HEAD_DIM), F32),
            pltpu.VMEM((N_HEADS, 2, nj, half, S2_BLOCK, HEAD_DIM), jnp.uint32),
        ],
        compiler_params=pltpu.CompilerParams(
            dimension_semantics=("parallel", "arbitrary"), vmem_limit_bytes=VMEM_LIMIT_V7X),
        name="k12_in_proj_dft",
    )(x.reshape(b, n1, n2, D_MODEL), g1, w_in_a, w2, f1, t0, tw, f3, *cast)


def _k3_kernel(x_ref, xp_ref, xn_ref, ya_ref, g1_ref, winb_ref, wouta_ref, wpo_ref, g2_ref,
               wg_ref, wu_ref, wd_ref, gf_ref, o_ref, hs_ref, zs_ref, act_ref, d_ref, h2_ref, *, seq_len):
    tm = x_ref.shape[1]
    th = tm // 2
    i = pl.program_id(1)
    last = pl.num_programs(1) - 1
    x = x_ref[0]
    g1 = g1_ref[...]
    g2 = g2_ref[...]
    gf = gf_ref[...]
    half_a = slice(0, th)
    half_b = slice(th, tm)

    ya = jnp.concatenate(
        [pltpu.bitcast(jnp.swapaxes(ya_ref[0, hd], 0, 1).reshape(tm // 2, HEAD_DIM), BF16)
         for hd in range(N_HEADS)], axis=-1)
    x1pb = x[half_b] + _dot(ya[half_b], wouta_ref[...])

    hs_ref[:HALO] = _rmsnorm(xp_ref[0], g1).astype(BF16)
    hs_ref[HALO:HALO + tm] = _rmsnorm(x, g1).astype(BF16)
    hs_ref[HALO + tm:] = _rmsnorm(xn_ref[0], g1).astype(BF16)
    zb = _dot(hs_ref[...], winb_ref[...])
    row = jax.lax.broadcasted_iota(jnp.int32, (tm + 2 * HALO, 1), 0)
    inside = ((row >= HALO) | (i > 0)) & ((row < HALO + tm) | (i < last))
    zb = jnp.where(inside, zb, 0.0)
    for gi in range(N_HEADS):
        zs_ref[gi] = zb[:, gi * HEAD_DIM:(gi + 1) * HEAD_DIM]

    def pool_rows(r0):
        n = VPU_SLICE
        t = i * tm + r0 + jax.lax.broadcasted_iota(jnp.int32, (n, 1), 0)
        d = []
        for gi, k in enumerate(POOL_WINDOWS):
            sums = zs_ref[gi, pl.ds(HALO + r0 - k // 2, n), :]
            for j in range(1, k):
                sums = sums + zs_ref[gi, pl.ds(HALO + r0 - k // 2 + j, n), :]
            cnt = jnp.minimum(t + k // 2, seq_len) - jnp.maximum(t - k // 2, 0)
            d.append((sums / cnt.astype(F32) - zs_ref[gi, pl.ds(HALO + r0, n), :]).astype(BF16))
        d_ref[pl.ds(r0, n), :] = jnp.concatenate(d, axis=-1)

    def ffn_chunk(rows, c):
        cols = slice(c * FF_CHUNK, (c + 1) * FF_CHUNK)
        h2 = h2_ref[rows, :]
        gate = _dot(h2, wg_ref[:, cols])
        up = _dot(h2, wu_ref[:, cols])
        act_ref[rows, cols] = (gate * jax.nn.sigmoid(gate) * up).astype(BF16)

    n_chunks = D_FF // FF_CHUNK
    n_slices = th // VPU_SLICE

    x1pa = []
    for q in range(PROJ_SPLIT):
        rows = slice(q * th // PROJ_SPLIT, (q + 1) * th // PROJ_SPLIT)
        x1pa.append(x[rows] + _dot(ya[rows], wouta_ref[...]))
        for s in range(q * n_slices // PROJ_SPLIT, (q + 1) * n_slices // PROJ_SPLIT):
            pool_rows(s * VPU_SLICE)
    x1a = jnp.concatenate(x1pa, axis=0) + _dot(d_ref[half_a, :], wpo_ref[...])
    h2_ref[half_a, :] = _rmsnorm(x1a, g2).astype(BF16)

    box = {}

    def proj_b():
        box["x1b"] = x1pb + _dot(d_ref[half_b, :], wpo_ref[...])

    def norm2_b(s):
        r = slice(s * VPU_SLICE, (s + 1) * VPU_SLICE)
        h2_ref[pl.ds(th + s * VPU_SLICE, VPU_SLICE), :] = _rmsnorm(box["x1b"][r], g2).astype(BF16)

    tasks = [functools.partial(pool_rows, th + s * VPU_SLICE) for s in range(n_slices)]
    tasks += [proj_b] + [functools.partial(norm2_b, s) for s in range(n_slices)]
    per_chunk = -(-len(tasks) // n_chunks)
    for c in range(n_chunks):
        ffn_chunk(half_a, c)
        for task in tasks[c * per_chunk:(c + 1) * per_chunk]:
            task()
    x2a = x1a + _dot(act_ref[half_a, :], wd_ref[...])

    def final_a(s):
        r = slice(s * VPU_SLICE, (s + 1) * VPU_SLICE)
        o_ref[0, r, :] = _rmsnorm(x2a[r], gf)

    tasks = [functools.partial(final_a, s) for s in range(n_slices)]
    per_chunk = -(-len(tasks) // n_chunks)
    for c in range(n_chunks):
        ffn_chunk(half_b, c)
        for task in tasks[c * per_chunk:(c + 1) * per_chunk]:
            task()
    x2b = box["x1b"] + _dot(act_ref[half_b, :], wd_ref[...])
    o_ref[0, half_b, :] = _rmsnorm(x2b, gf)


def _k3(x, ya, g1, w_in_b, w_out_a, w_po, g2, w_gate, w_up, w_down, gf):
    b, s, _ = x.shape
    tm = TOKEN_TILE
    hb = tm // HALO
    n_hb = s // HALO
    return pl.pallas_call(
        functools.partial(_k3_kernel, seq_len=s),
        grid=(b, s // tm),
        in_specs=[
            pl.BlockSpec((1, tm, D_MODEL), lambda bi, i: (bi, i, 0)),
            pl.BlockSpec((1, HALO, D_MODEL), lambda bi, i: (bi, jnp.maximum(i * hb - 1, 0), 0)),
            pl.BlockSpec((1, HALO, D_MODEL), lambda bi, i: (bi, jnp.minimum((i + 1) * hb, n_hb - 1), 0)),
            pl.BlockSpec((1, N_HEADS, DFT_SPLIT // (2 * K1_PAIRS), tm // DFT_SPLIT, K1_PAIRS, HEAD_DIM),
                         lambda bi, i: (bi, 0, 0, i, 0, 0)),
            _const_spec((1, D_MODEL)),
            _const_spec((D_MODEL, B_WIDTH)),
            _const_spec((A_WIDTH, D_MODEL)),
            _const_spec((B_WIDTH, D_MODEL)),
            _const_spec((1, D_MODEL)),
            _const_spec((D_MODEL, D_FF)),
            _const_spec((D_MODEL, D_FF)),
            _const_spec((D_FF, D_MODEL)),
            _const_spec((1, D_MODEL)),
        ],
        out_specs=pl.BlockSpec((1, tm, D_MODEL), lambda bi, i: (bi, i, 0)),
        out_shape=jax.ShapeDtypeStruct((b, s, D_MODEL), F32),
        scratch_shapes=[
            pltpu.VMEM((tm + 2 * HALO, D_MODEL), BF16),
            pltpu.VMEM((N_HEADS, tm + 2 * HALO, HEAD_DIM), F32),
            pltpu.VMEM((tm, D_FF), BF16),
            pltpu.VMEM((tm, B_WIDTH), BF16),
            pltpu.VMEM((tm, D_MODEL), BF16),
        ],
        compiler_params=pltpu.CompilerParams(
            dimension_semantics=("parallel", "parallel"), vmem_limit_bytes=VMEM_LIMIT_V7X),
        name="k3_pool_out_ffn",
    )(x, x, x, ya, g1, w_in_b, w_out_a, w_po, g2, w_gate, w_up, w_down, gf)


def kernel(x_prompt, x_sample, norm1_g, w_in, w_fourier, w_pool, pool_scale, w_out, norm2_g,
           w_gate, w_up, w_down, normf_g):
    assert norm1_g.shape[0] == 1, "single-layer block"
    g1 = norm1_g[0][None]
    g2 = norm2_g[0][None]
    gf = normf_g[None]
    w2, w_po, w_in_a, w_in_b, w_o_a = _prep(w_fourier[0], w_pool[0], pool_scale[0][None], w_out[0], w_in[0])

    ya, w_g, w_u, w_d = _k12(x_prompt, g1, w_in_a, w2, cast=(w_gate[0], w_up[0], w_down[0]))
    y_prompt = _k3(x_prompt, ya, g1, w_in_b, w_o_a, w_po, g2, w_g, w_u, w_d, gf)
    y_prompt, x_sample = jax.lax.optimization_barrier((y_prompt, x_sample))
    (ya,) = _k12(x_sample, g1, w_in_a, w2)
    return y_prompt, _k3(x_sample, ya, g1, w_in_b, w_o_a, w_po, g2, w_g, w_u, w_d, gf)
```

```python
import functools

import jax
import jax.numpy as jnp
import numpy as np
from jax.experimental import pallas as pl
from jax.experimental.pallas import tpu as pltpu

F32 = jnp.float32
BF16 = jnp.bfloat16

D_MODEL = 1024
N_HEADS = 4
HEAD_DIM = 128
A_WIDTH = N_HEADS * HEAD_DIM
B_WIDTH = N_HEADS * HEAD_DIM
POOL_WINDOWS = (2, 4, 8, 16)
D_FF = 2816
EPS = 1e-6

HALO = 16
TOKEN_TILE = 1024
FF_CHUNK = 256
VPU_SLICE = 64
PROJ_SPLIT = 2
DFT_SPLIT = 128
S2_BLOCK = 8
MAX_S2_STEP = 2
K1_PAIRS = 8
MAX_PAIR_BLOCKS = 4
K12_FIXED_VMEM = 7 * 1024 * 1024
VMEM_LIMIT_V7X = 58 * 1024 * 1024


def _const_spec(shape):
    zeros = (0,) * len(shape)
    return pl.BlockSpec(shape, lambda *_: zeros, pipeline_mode=pl.Buffered(1))


def _rmsnorm(x, g):
    r = jax.lax.rsqrt(jnp.mean(x * x, axis=-1, keepdims=True) + EPS)
    return x * r * g


def _dot(a, b):
    return jnp.dot(a, b, preferred_element_type=F32)


def _split_bf16(a):
    hi = a.astype(BF16)
    lo = (a - hi.astype(F32)).astype(BF16)
    return hi, lo


def _dot_f32(a, b):
    ah, al = _split_bf16(a)
    bh, bl = _split_bf16(b)
    return _dot(ah, bh) + (_dot(ah, bl) + _dot(al, bh))


def _prep_kernel(cs_ref, wf_ref, wp_ref, ps_ref, wout_ref, win_ref, w2_ref, wpo_ref, wina_ref, winb_ref,
                 wouta_ref):
    wina_ref[...] = win_ref[:, :A_WIDTH].astype(BF16)
    winb_ref[...] = win_ref[:, A_WIDTH:].astype(BF16)
    wouta_ref[...] = wout_ref[:A_WIDTH, :].astype(BF16)
    for h in range(N_HEADS):
        w = wf_ref[h]
        w2_ref[h, :, :HEAD_DIM] = _dot_f32(cs_ref[0], w).astype(BF16)
        w2_ref[h, :, HEAD_DIM:] = (-_dot_f32(cs_ref[1], w)).astype(BF16)
    for g in range(N_HEADS):
        lanes = slice(g * HEAD_DIM, (g + 1) * HEAD_DIM)
        rows = slice(A_WIDTH + g * HEAD_DIM, A_WIDTH + (g + 1) * HEAD_DIM)
        wpo_ref[lanes, :] = _dot_f32(wp_ref[g] * ps_ref[:, lanes], wout_ref[rows, :]).astype(BF16)


def _prep(w_fourier, w_pool, pool_scale, w_out, w_in):
    n = np.arange(HEAD_DIM)
    ang = 2.0 * np.pi * np.outer(n, n) / HEAD_DIM
    cs = np.stack([np.cos(ang), np.sin(ang)]) / np.sqrt(HEAD_DIM)
    return pl.pallas_call(
        _prep_kernel,
        out_shape=(jax.ShapeDtypeStruct((N_HEADS, HEAD_DIM, 2 * HEAD_DIM), BF16),
                   jax.ShapeDtypeStruct((B_WIDTH, D_MODEL), BF16),
                   jax.ShapeDtypeStruct((D_MODEL, A_WIDTH), BF16),
                   jax.ShapeDtypeStruct((D_MODEL, B_WIDTH), BF16),
                   jax.ShapeDtypeStruct((A_WIDTH, D_MODEL), BF16)),
        name="prep_weights",
    )(jnp.asarray(cs, F32), w_fourier, w_pool, pool_scale, w_out, w_in)


def _stage1_group(x_ref, g1_ref, win_ref, w2_ref, f1_ref, gs_ref, ab_ref, sb, jg, n1):
    rows = n1 * S2_BLOCK
    half = n1 // 2
    x = x_ref[0, :, sb * S2_BLOCK:(sb + 1) * S2_BLOCK, :].reshape(rows, D_MODEL)
    h = _rmsnorm(x, g1_ref[...]).astype(BF16)
    za = _dot(h, win_ref[...]).astype(BF16)
    for hd in range(N_HEADS):
        ab = _dot(za[:, hd * HEAD_DIM:(hd + 1) * HEAD_DIM], w2_ref[hd])
        buf = ab_ref.at[hd % 2]
        buf[0] = ab[:, :HEAD_DIM]
        buf[1] = ab[:, HEAD_DIM:]
        xx = jnp.concatenate(
            [jnp.concatenate([buf.at[ri][pl.ds(j, n1, stride=S2_BLOCK), :]
                              for j in range(S2_BLOCK)], axis=1) for ri in range(2)],
            axis=0).astype(BF16)
        g = _dot(f1_ref[...], xx)
        gp = pltpu.bitcast(g.astype(BF16), jnp.uint32)
        for ri in range(2):
            out = gs_ref.at[hd, ri, jg].reshape(half * S2_BLOCK, HEAD_DIM)
            for j in range(S2_BLOCK):
                out[pl.ds(j, half, stride=S2_BLOCK), :] = (
                    gp[ri * half:(ri + 1) * half, j * HEAD_DIM:(j + 1) * HEAD_DIM])


def _stage2_pairs(gs_ref, t0_ref, tw_ref, f3_ref, y_ref, jb, n2, pair_blocks):
    pairs = pair_blocks * K1_PAIRS
    tr = t0_ref[0, 0]
    ts = t0_ref[0, 1]
    for p in range(pairs):
        if p:
            tr, ts = tr * tw_ref[0] - ts * tw_ref[1], tr * tw_ref[1] + ts * tw_ref[0]
        pr = []
        pi = []
        for hd in range(N_HEADS):
            wr = gs_ref[hd, 0, :, jb * pairs + p].reshape(n2, HEAD_DIM)
            wi = gs_ref[hd, 1, :, jb * pairs + p].reshape(n2, HEAD_DIM)
            gr = pltpu.bitcast(wr, BF16).astype(F32)
            gi = pltpu.bitcast(wi, BF16).astype(F32)
            pr.append(gr * tr + gi * ts)
            pi.append(gi * tr - gr * ts)
        x = jnp.concatenate([jnp.concatenate(pr, axis=1), jnp.concatenate(pi, axis=1)],
                            axis=0).astype(BF16)
        r = _dot(f3_ref[...], x)
        ru = pltpu.bitcast(r.astype(BF16), jnp.uint32)
        for hd in range(N_HEADS):
            out = y_ref.at[0, hd, p // K1_PAIRS].reshape(n2 * K1_PAIRS, HEAD_DIM)
            out[pl.ds(p % K1_PAIRS, n2, stride=K1_PAIRS), :] = ru[:, hd * HEAD_DIM:(hd + 1) * HEAD_DIM]


def _k12_kernel(x_ref, g1_ref, win_ref, w2_ref, f1_ref, t0_ref, tw_ref, f3_ref, *refs,
                n1, n2, s2_step, pair_blocks, n_cast):
    cast_in, y_ref, cast_out = refs[:n_cast], refs[n_cast], refs[n_cast + 1:2 * n_cast + 1]
    ab_ref, gs_ref = refs[2 * n_cast + 1:]
    j = pl.program_id(1)
    n_s1 = n2 // (s2_step * S2_BLOCK)

    @pl.when(j < n_s1)
    def _():
        for sb in range(s2_step):
            _stage1_group(x_ref, g1_ref, win_ref, w2_ref, f1_ref, gs_ref, ab_ref, sb, j * s2_step + sb, n1)
        for src, dst in zip(cast_in, cast_out):
            dst[...] = src[...].astype(BF16)

    @pl.when(j >= n_s1)
    def _():
        _stage2_pairs(gs_ref, t0_ref, tw_ref, f3_ref, y_ref, j - n_s1, n2, pair_blocks)


def _dft_tables(s, n1, pair_blocks):
    n2 = s // n1
    k1 = np.arange(n1)
    a1 = 2.0 * np.pi * np.outer(k1, k1) / n1
    c1, s1 = np.cos(a1), np.sin(a1)
    f1 = np.block([[c1, s1], [-s1, c1]]) / np.sqrt(n1)
    at = 2.0 * np.pi * np.outer(k1, np.arange(n2)) / s
    k2 = np.arange(n2)
    a2 = 2.0 * np.pi * np.outer(k2, k2) / n2
    f3 = np.concatenate([np.cos(a2), np.sin(a2)], axis=1) / np.sqrt(n2)
    f3p = np.einsum("krs,ef->kersf", f3.reshape(n2, 2, n2), np.eye(2)).reshape(2 * n2, 4 * n2)
    lanes = np.ones((1, 1, HEAD_DIM))
    atp = at.reshape(n1 // 2, 2, n2).transpose(0, 2, 1).reshape(n1 // 2, 2 * n2)
    at0 = atp[::pair_blocks * K1_PAIRS]
    t0 = np.stack([np.cos(at0), np.sin(at0)], axis=1)[:, :, :, None] * lanes
    a_next = np.repeat(2.0 * np.pi * 2.0 * np.arange(n2) / s, 2)
    tw = np.stack([np.cos(a_next), np.sin(a_next)])[:, :, None] * lanes
    return (jnp.asarray(f1, F32).astype(BF16), jnp.asarray(t0, F32), jnp.asarray(tw, F32),
            jnp.asarray(f3p, F32).astype(BF16))


def _k12_plan(s, cast):
    n2 = s // DFT_SPLIT
    spectrum = s * A_WIDTH * 4
    x_group = 2 * DFT_SPLIT * S2_BLOCK * D_MODEL * 4
    y_block = 2 * N_HEADS * n2 * K1_PAIRS * HEAD_DIM * 4
    for s2_step in range(MAX_S2_STEP, 0, -1):
        n_s1 = n2 // (s2_step * S2_BLOCK)
        cast_bytes = sum(2 * (4 + 2) * w.size // n_s1 for w in cast)
        for pair_blocks in range(MAX_PAIR_BLOCKS, 0, -1):
            need = spectrum + K12_FIXED_VMEM + s2_step * x_group + pair_blocks * y_block + cast_bytes
            if need <= VMEM_LIMIT_V7X:
                return s2_step, pair_blocks
    raise ValueError(f"k12 does not fit VMEM for sequence length {s}")


def _k12(x, g1, w_in_a, w2, cast=()):
    b, s, _ = x.shape
    n1 = DFT_SPLIT
    n2 = s // n1
    half = n1 // 2
    nj = n2 // S2_BLOCK
    s2_step, pair_blocks = _k12_plan(s, cast)
    f1, t0, tw, f3 = _dft_tables(s, n1, pair_blocks)
    n_s1 = n2 // (s2_step * S2_BLOCK)
    n_s2 = half // (pair_blocks * K1_PAIRS)
    cast_rows = [w.shape[0] // n_s1 for w in cast]
    assert all(r * n_s1 == w.shape[0] and r % 16 == 0 for r, w in zip(cast_rows, cast)), cast_rows

    def cast_spec(r, w):
        return pl.BlockSpec((r, w.shape[1]), lambda bi, j: (jnp.minimum(j, n_s1 - 1), 0))

    y_spec = pl.BlockSpec((1, N_HEADS, pair_blocks, n2, K1_PAIRS, HEAD_DIM),
                          lambda bi, j: (bi, 0, jnp.maximum(j - n_s1, 0), 0, 0, 0))
    y_shape = jax.ShapeDtypeStruct((b, N_HEADS, half // K1_PAIRS, n2, K1_PAIRS, HEAD_DIM), jnp.uint32)
    return pl.pallas_call(
        functools.partial(_k12_kernel, n1=n1, n2=n2, s2_step=s2_step, pair_blocks=pair_blocks,
                          n_cast=len(cast)),
        grid=(b, n_s1 + n_s2),
        in_specs=[
            pl.BlockSpec((1, n1, s2_step * S2_BLOCK, D_MODEL),
                         lambda bi, j: (bi, 0, jnp.minimum(j, n_s1 - 1), 0)),
            _const_spec((1, D_MODEL)),
            _const_spec((D_MODEL, A_WIDTH)),
            _const_spec((N_HEADS, HEAD_DIM, 2 * HEAD_DIM)),
            _const_spec((2 * n1, 2 * n1)),
            pl.BlockSpec((1, 2, 2 * n2, HEAD_DIM), lambda bi, j: (jnp.maximum(j - n_s1, 0), 0, 0, 0)),
            _const_spec((2, 2 * n2, HEAD_DIM)),
            _const_spec((2 * n2, 4 * n2)),
        ] + [cast_spec(r, w) for r, w in zip(cast_rows, cast)],
        out_specs=[y_spec] + [cast_spec(r, w) for r, w in zip(cast_rows, cast)],
        out_shape=[y_shape] + [jax.ShapeDtypeStruct(w.shape, BF16) for w in cast],
        scratch_shapes=[
            pltpu.VMEM((2, 2, n1 * S2_BLOCK, HEAD_DIM), F32),
            pltpu.VMEM((N_HEADS, 2, nj, half, S2_BLOCK, HEAD_DIM), jnp.uint32),
        ],
        compiler_params=pltpu.CompilerParams(
            dimension_semantics=("parallel", "arbitrary"), vmem_limit_bytes=VMEM_LIMIT_V7X),
        name="k12_in_proj_dft",
    )(x.reshape(b, n1, n2, D_MODEL), g1, w_in_a, w2, f1, t0, tw, f3, *cast)


def _k3_kernel(x_ref, xp_ref, xn_ref, ya_ref, g1_ref, winb_ref, wouta_ref, wpo_ref, g2_ref,
               wg_ref, wu_ref, wd_ref, gf_ref, o_ref, hs_ref, zs_ref, act_ref, d_ref, h2_ref, *, seq_len):
    tm = x_ref.shape[1]
    th = tm // 2
    i = pl.program_id(1)
    last = pl.num_programs(1) - 1
    x = x_ref[0]
    g1 = g1_ref[...]
    g2 = g2_ref[...]
    gf = gf_ref[...]
    half_a = slice(0, th)
    half_b = slice(th, tm)

    ya = jnp.concatenate(
        [pltpu.bitcast(jnp.swapaxes(ya_ref[0, hd], 0, 1).reshape(tm // 2, HEAD_DIM), BF16)
         for hd in range(N_HEADS)], axis=-1)
    x1pb = x[half_b] + _dot(ya[half_b], wouta_ref[...])

    hs_ref[:HALO] = _rmsnorm(xp_ref[0], g1).astype(BF16)
    hs_ref[HALO:HALO + tm] = _rmsnorm(x, g1).astype(BF16)
    hs_ref[HALO + tm:] = _rmsnorm(xn_ref[0], g1).astype(BF16)
    zb = _dot(hs_ref[...], winb_ref[...])
    row = jax.lax.broadcasted_iota(jnp.int32, (tm + 2 * HALO, 1), 0)
    inside = ((row >= HALO) | (i > 0)) & ((row < HALO + tm) | (i < last))
    zb = jnp.where(inside, zb, 0.0)
    for gi in range(N_HEADS):
        zs_ref[gi] = zb[:, gi * HEAD_DIM:(gi + 1) * HEAD_DIM]

    def pool_rows(r0):
        n = VPU_SLICE
        t = i * tm + r0 + jax.lax.broadcasted_iota(jnp.int32, (n, 1), 0)
        d = []
        for gi, k in enumerate(POOL_WINDOWS):
            sums = zs_ref[gi, pl.ds(HALO + r0 - k // 2, n), :]
            for j in range(1, k):
                sums = sums + zs_ref[gi, pl.ds(HALO + r0 - k // 2 + j, n), :]
            cnt = jnp.minimum(t + k // 2, seq_len) - jnp.maximum(t - k // 2, 0)
            d.append((sums / cnt.astype(F32) - zs_ref[gi, pl.ds(HALO + r0, n), :]).astype(BF16))
        d_ref[pl.ds(r0, n), :] = jnp.concatenate(d, axis=-1)

    def ffn_chunk(rows, c):
        cols = slice(c * FF_CHUNK, (c + 1) * FF_CHUNK)
        h2 = h2_ref[rows, :]
        gate = _dot(h2, wg_ref[:, cols])
        up = _dot(h2, wu_ref[:, cols])
        act_ref[rows, cols] = (gate * jax.nn.sigmoid(gate) * up).astype(BF16)

    n_chunks = D_FF // FF_CHUNK
    n_slices = th // VPU_SLICE

    x1pa = []
    for q in range(PROJ_SPLIT):
        rows = slice(q * th // PROJ_SPLIT, (q + 1) * th // PROJ_SPLIT)
        x1pa.append(x[rows] + _dot(ya[rows], wouta_ref[...]))
        for s in range(q * n_slices // PROJ_SPLIT, (q + 1) * n_slices // PROJ_SPLIT):
            pool_rows(s * VPU_SLICE)
    x1a = jnp.concatenate(x1pa, axis=0) + _dot(d_ref[half_a, :], wpo_ref[...])
    h2_ref[half_a, :] = _rmsnorm(x1a, g2).astype(BF16)

    box = {}

    def proj_b():
        box["x1b"] = x1pb + _dot(d_ref[half_b, :], wpo_ref[...])

    def norm2_b(s):
        r = slice(s * VPU_SLICE, (s + 1) * VPU_SLICE)
        h2_ref[pl.ds(th + s * VPU_SLICE, VPU_SLICE), :] = _rmsnorm(box["x1b"][r], g2).astype(BF16)

    tasks = [functools.partial(pool_rows, th + s * VPU_SLICE) for s in range(n_slices)]
    tasks += [proj_b] + [functools.partial(norm2_b, s) for s in range(n_slices)]
    per_chunk = -(-len(tasks) // n_chunks)
    for c in range(n_chunks):
        ffn_chunk(half_a, c)
        for task in tasks[c * per_chunk:(c + 1) * per_chunk]:
            task()
    x2a = x1a + _dot(act_ref[half_a, :], wd_ref[...])

    def final_a(s):
        r = slice(s * VPU_SLICE, (s + 1) * VPU_SLICE)
        o_ref[0, r, :] = _rmsnorm(x2a[r], gf)

    tasks = [functools.partial(final_a, s) for s in range(n_slices)]
    per_chunk = -(-len(tasks) // n_chunks)
    for c in range(n_chunks):
        ffn_chunk(half_b, c)
        for task in tasks[c * per_chunk:(c + 1) * per_chunk]:
            task()
    x2b = box["x1b"] + _dot(act_ref[half_b, :], wd_ref[...])
    o_ref[0, half_b, :] = _rmsnorm(x2b, gf)


def _k3(x, ya, g1, w_in_b, w_out_a, w_po, g2, w_gate, w_up, w_down, gf):
    b, s, _ = x.shape
    tm = TOKEN_TILE
    hb = tm // HALO
    n_hb = s // HALO
    return pl.pallas_call(
        functools.partial(_k3_kernel, seq_len=s),
        grid=(b, s // tm),
        in_specs=[
            pl.BlockSpec((1, tm, D_MODEL), lambda bi, i: (bi, i, 0)),
            pl.BlockSpec((1, HALO, D_MODEL), lambda bi, i: (bi, jnp.maximum(i * hb - 1, 0), 0)),
            pl.BlockSpec((1, HALO, D_MODEL), lambda bi, i: (bi, jnp.minimum((i + 1) * hb, n_hb - 1), 0)),
            pl.BlockSpec((1, N_HEADS, DFT_SPLIT // (2 * K1_PAIRS), tm // DFT_SPLIT, K1_PAIRS, HEAD_DIM),
                         lambda bi, i: (bi, 0, 0, i, 0, 0)),
            _const_spec((1, D_MODEL)),
            _const_spec((D_MODEL, B_WIDTH)),
            _const_spec((A_WIDTH, D_MODEL)),
            _const_spec((B_WIDTH, D_MODEL)),
            _const_spec((1, D_MODEL)),
            _const_spec((D_MODEL, D_FF)),
            _const_spec((D_MODEL, D_FF)),
            _const_spec((D_FF, D_MODEL)),
            _const_spec((1, D_MODEL)),
        ],
        out_specs=pl.BlockSpec((1, tm, D_MODEL), lambda bi, i: (bi, i, 0)),
        out_shape=jax.ShapeDtypeStruct((b, s, D_MODEL), F32),
        scratch_shapes=[
            pltpu.VMEM((tm + 2 * HALO, D_MODEL), BF16),
            pltpu.VMEM((N_HEADS, tm + 2 * HALO, HEAD_DIM), F32),
            pltpu.VMEM((tm, D_FF), BF16),
            pltpu.VMEM((tm, B_WIDTH), BF16),
            pltpu.VMEM((tm, D_MODEL), BF16),
        ],
        compiler_params=pltpu.CompilerParams(
            dimension_semantics=("parallel", "parallel"), vmem_limit_bytes=VMEM_LIMIT_V7X),
        name="k3_pool_out_ffn",
    )(x, x, x, ya, g1, w_in_b, w_out_a, w_po, g2, w_gate, w_up, w_down, gf)


def kernel(x_prompt, x_sample, norm1_g, w_in, w_fourier, w_pool, pool_scale, w_out, norm2_g,
           w_gate, w_up, w_down, normf_g):
    assert norm1_g.shape[0] == 1, "single-layer block"
    g1 = norm1_g[0][None]
    g2 = norm2_g[0][None]
    gf = normf_g[None]
    w2, w_po, w_in_a, w_in_b, w_o_a = _prep(w_fourier[0], w_pool[0], pool_scale[0][None], w_out[0], w_in[0])

    ya, w_g, w_u, w_d = _k12(x_prompt, g1, w_in_a, w2, cast=(w_gate[0], w_up[0], w_down[0]))
    y_prompt = _k3(x_prompt, ya, g1, w_in_b, w_o_a, w_po, g2, w_g, w_u, w_d, gf)
    y_prompt, x_sample = jax.lax.optimization_barrier((y_prompt, x_sample))
    (ya,) = _k12(x_sample, g1, w_in_a, w2)
    return y_prompt, _k3(x_sample, ya, g1, w_in_b, w_o_a, w_po, g2, w_g, w_u, w_d, gf)
```

```python
import functools

import jax
import jax.numpy as jnp
import numpy as np
from jax.experimental import pallas as pl
from jax.experimental.pallas import tpu as pltpu

F32 = jnp.float32
BF16 = jnp.bfloat16

D_MODEL = 1024
N_HEADS = 4
HEAD_DIM = 128
A_WIDTH = N_HEADS * HEAD_DIM
B_WIDTH = N_HEADS * HEAD_DIM
POOL_WINDOWS = (2, 4, 8, 16)
D_FF = 2816
EPS = 1e-6

HALO = 16
TOKEN_TILE = 1024
FF_CHUNK = 256
VPU_SLICE = 64
PROJ_SPLIT = 2
DFT_SPLIT = 128
S2_BLOCK = 8
MAX_S2_STEP = 2
K1_PAIRS = 8
MAX_PAIR_BLOCKS = 4
K12_FIXED_VMEM = 6 * 1024 * 1024
VMEM_LIMIT_V7X = 61 * 1024 * 1024


def _const_spec(shape):
    zeros = (0,) * len(shape)
    return pl.BlockSpec(shape, lambda *_: zeros, pipeline_mode=pl.Buffered(1))


def _rmsnorm(x, g):
    r = jax.lax.rsqrt(jnp.mean(x * x, axis=-1, keepdims=True) + EPS)
    return x * r * g


def _dot(a, b):
    return jnp.dot(a, b, preferred_element_type=F32)


def _split_bf16(a):
    hi = a.astype(BF16)
    lo = (a - hi.astype(F32)).astype(BF16)
    return hi, lo


def _dot_f32(a, b):
    ah, al = _split_bf16(a)
    bh, bl = _split_bf16(b)
    return _dot(ah, bh) + (_dot(ah, bl) + _dot(al, bh))


def _prep_kernel(cs_ref, wf_ref, wp_ref, ps_ref, wout_ref, win_ref, w2_ref, wpo_ref, wina_ref, winb_ref,
                 wouta_ref):
    wina_ref[...] = win_ref[:, :A_WIDTH].astype(BF16)
    winb_ref[...] = win_ref[:, A_WIDTH:].astype(BF16)
    wouta_ref[...] = wout_ref[:A_WIDTH, :].astype(BF16)
    for h in range(N_HEADS):
        w = wf_ref[h]
        w2_ref[h, :, :HEAD_DIM] = _dot_f32(cs_ref[0], w).astype(BF16)
        w2_ref[h, :, HEAD_DIM:] = (-_dot_f32(cs_ref[1], w)).astype(BF16)
    for g in range(N_HEADS):
        lanes = slice(g * HEAD_DIM, (g + 1) * HEAD_DIM)
        rows = slice(A_WIDTH + g * HEAD_DIM, A_WIDTH + (g + 1) * HEAD_DIM)
        wpo_ref[lanes, :] = _dot_f32(wp_ref[g] * ps_ref[:, lanes], wout_ref[rows, :]).astype(BF16)


def _prep(w_fourier, w_pool, pool_scale, w_out, w_in):
    n = np.arange(HEAD_DIM)
    ang = 2.0 * np.pi * np.outer(n, n) / HEAD_DIM
    cs = np.stack([np.cos(ang), np.sin(ang)]) / np.sqrt(HEAD_DIM)
    return pl.pallas_call(
        _prep_kernel,
        out_shape=(jax.ShapeDtypeStruct((N_HEADS, HEAD_DIM, 2 * HEAD_DIM), BF16),
                   jax.ShapeDtypeStruct((B_WIDTH, D_MODEL), BF16),
                   jax.ShapeDtypeStruct((D_MODEL, A_WIDTH), BF16),
                   jax.ShapeDtypeStruct((D_MODEL, B_WIDTH), BF16),
                   jax.ShapeDtypeStruct((A_WIDTH, D_MODEL), BF16)),
        name="prep_weights",
    )(jnp.asarray(cs, F32), w_fourier, w_pool, pool_scale, w_out, w_in)


def _stage1_group(x_ref, g1_ref, win_ref, w2_ref, f1_ref, gs_ref, ab_ref, sb, jg, n1):
    rows = n1 * S2_BLOCK
    half = n1 // 2
    x = x_ref[0, :, sb * S2_BLOCK:(sb + 1) * S2_BLOCK, :].reshape(rows, D_MODEL)
    h = _rmsnorm(x, g1_ref[...]).astype(BF16)
    za = _dot(h, win_ref[...]).astype(BF16)
    for hd in range(N_HEADS):
        ab = _dot(za[:, hd * HEAD_DIM:(hd + 1) * HEAD_DIM], w2_ref[hd])
        buf = ab_ref.at[hd % 2]
        buf[0] = ab[:, :HEAD_DIM]
        buf[1] = ab[:, HEAD_DIM:]
        xx = jnp.concatenate(
            [jnp.concatenate([buf.at[ri][pl.ds(j, n1, stride=S2_BLOCK), :]
                              for j in range(S2_BLOCK)], axis=1) for ri in range(2)],
            axis=0).astype(BF16)
        g = _dot(f1_ref[...], xx)
        gp = pltpu.bitcast(g.astype(BF16), jnp.uint32)
        for ri in range(2):
            out = gs_ref.at[hd, ri, jg].reshape(half * S2_BLOCK, HEAD_DIM)
            for j in range(S2_BLOCK):
                out[pl.ds(j, half, stride=S2_BLOCK), :] = (
                    gp[ri * half:(ri + 1) * half, j * HEAD_DIM:(j + 1) * HEAD_DIM])


def _stage2_pairs(gs_ref, t0_ref, tw_ref, f3_ref, y_ref, jb, n2, pair_blocks):
    pairs = pair_blocks * K1_PAIRS
    tr = t0_ref[0, 0]
    ts = t0_ref[0, 1]
    for p in range(pairs):
        if p:
            tr, ts = tr * tw_ref[0] - ts * tw_ref[1], tr * tw_ref[1] + ts * tw_ref[0]
        pr = []
        pi = []
        for hd in range(N_HEADS):
            wr = gs_ref[hd, 0, :, jb * pairs + p].reshape(n2, HEAD_DIM)
            wi = gs_ref[hd, 1, :, jb * pairs + p].reshape(n2, HEAD_DIM)
            gr = pltpu.bitcast(wr, BF16).astype(F32)
            gi = pltpu.bitcast(wi, BF16).astype(F32)
            pr.append(gr * tr + gi * ts)
            pi.append(gi * tr - gr * ts)
        x = jnp.concatenate([jnp.concatenate(pr, axis=1), jnp.concatenate(pi, axis=1)],
                            axis=0).astype(BF16)
        r = _dot(f3_ref[...], x)
        ru = pltpu.bitcast(r.astype(BF16), jnp.uint32)
        for hd in range(N_HEADS):
            out = y_ref.at[0, hd, p // K1_PAIRS].reshape(n2 * K1_PAIRS, HEAD_DIM)
            out[pl.ds(p % K1_PAIRS, n2, stride=K1_PAIRS), :] = ru[:, hd * HEAD_DIM:(hd + 1) * HEAD_DIM]


def _k12_kernel(x_ref, g1_ref, win_ref, w2_ref, f1_ref, t0_ref, tw_ref, f3_ref, *refs,
                n1, n2, s2_step, pair_blocks, n_cast):
    cast_in, y_ref, cast_out = refs[:n_cast], refs[n_cast], refs[n_cast + 1:2 * n_cast + 1]
    ab_ref, gs_ref = refs[2 * n_cast + 1:]
    j = pl.program_id(1)
    n_s1 = n2 // (s2_step * S2_BLOCK)

    @pl.when(j < n_s1)
    def _():
        for sb in range(s2_step):
            _stage1_group(x_ref, g1_ref, win_ref, w2_ref, f1_ref, gs_ref, ab_ref, sb, j * s2_step + sb, n1)
        for src, dst in zip(cast_in, cast_out):
            dst[...] = src[...].astype(BF16)

    @pl.when(j >= n_s1)
    def _():
        _stage2_pairs(gs_ref, t0_ref, tw_ref, f3_ref, y_ref, j - n_s1, n2, pair_blocks)


def _dft_tables(s, n1, pair_blocks):
    n2 = s // n1
    k1 = np.arange(n1)
    a1 = 2.0 * np.pi * np.outer(k1, k1) / n1
    c1, s1 = np.cos(a1), np.sin(a1)
    f1 = np.block([[c1, s1], [-s1, c1]]) / np.sqrt(n1)
    at = 2.0 * np.pi * np.outer(k1, np.arange(n2)) / s
    k2 = np.arange(n2)
    a2 = 2.0 * np.pi * np.outer(k2, k2) / n2
    f3 = np.concatenate([np.cos(a2), np.sin(a2)], axis=1) / np.sqrt(n2)
    f3p = np.einsum("krs,ef->kersf", f3.reshape(n2, 2, n2), np.eye(2)).reshape(2 * n2, 4 * n2)
    lanes = np.ones((1, 1, HEAD_DIM))
    atp = at.reshape(n1 // 2, 2, n2).transpose(0, 2, 1).reshape(n1 // 2, 2 * n2)
    at0 = atp[::pair_blocks * K1_PAIRS]
    t0 = np.stack([np.cos(at0), np.sin(at0)], axis=1)[:, :, :, None] * lanes
    a_next = np.repeat(2.0 * np.pi * 2.0 * np.arange(n2) / s, 2)
    tw = np.stack([np.cos(a_next), np.sin(a_next)])[:, :, None] * lanes
    return (jnp.asarray(f1, F32).astype(BF16), jnp.asarray(t0, F32), jnp.asarray(tw, F32),
            jnp.asarray(f3p, F32).astype(BF16))


def _k12_plan(s, cast):
    n2 = s // DFT_SPLIT
    spectrum = s * A_WIDTH * 4
    x_group = 2 * DFT_SPLIT * S2_BLOCK * D_MODEL * 4
    y_block = 2 * N_HEADS * n2 * K1_PAIRS * HEAD_DIM * 4
    for s2_step in range(MAX_S2_STEP, 0, -1):
        n_s1 = n2 // (s2_step * S2_BLOCK)
        cast_bytes = sum(2 * (4 + 2) * w.size // n_s1 for w in cast)
        for pair_blocks in range(MAX_PAIR_BLOCKS, 0, -1):
            need = spectrum + K12_FIXED_VMEM + s2_step * x_group + pair_blocks * y_block + cast_bytes
            if need <= VMEM_LIMIT_V7X:
                return s2_step, pair_blocks
    raise ValueError(f"k12 does not fit VMEM for sequence length {s}")


def _k12(x, g1, w_in_a, w2, cast=()):
    b, s, _ = x.shape
    n1 = DFT_SPLIT
    n2 = s // n1
    half = n1 // 2
    nj = n2 // S2_BLOCK
    s2_step, pair_blocks = _k12_plan(s, cast)
    f1, t0, tw, f3 = _dft_tables(s, n1, pair_blocks)
    n_s1 = n2 // (s2_step * S2_BLOCK)
    n_s2 = half // (pair_blocks * K1_PAIRS)
    cast_rows = [w.shape[0] // n_s1 for w in cast]
    assert all(r * n_s1 == w.shape[0] and r % 16 == 0 for r, w in zip(cast_rows, cast)), cast_rows

    def cast_spec(r, w):
        return pl.BlockSpec((r, w.shape[1]), lambda bi, j: (jnp.minimum(j, n_s1 - 1), 0))

    y_spec = pl.BlockSpec((1, N_HEADS, pair_blocks, n2, K1_PAIRS, HEAD_DIM),
                          lambda bi, j: (bi, 0, jnp.maximum(j - n_s1, 0), 0, 0, 0))
    y_shape = jax.ShapeDtypeStruct((b, N_HEADS, half // K1_PAIRS, n2, K1_PAIRS, HEAD_DIM), jnp.uint32)
    return pl.pallas_call(
        functools.partial(_k12_kernel, n1=n1, n2=n2, s2_step=s2_step, pair_blocks=pair_blocks,
                          n_cast=len(cast)),
        grid=(b, n_s1 + n_s2),
        in_specs=[
            pl.BlockSpec((1, n1, s2_step * S2_BLOCK, D_MODEL),
                         lambda bi, j: (bi, 0, jnp.minimum(j, n_s1 - 1), 0)),
            _const_spec((1, D_MODEL)),
            _const_spec((D_MODEL, A_WIDTH)),
            _const_spec((N_HEADS, HEAD_DIM, 2 * HEAD_DIM)),
            _const_spec((2 * n1, 2 * n1)),
            pl.BlockSpec((1, 2, 2 * n2, HEAD_DIM), lambda bi, j: (jnp.maximum(j - n_s1, 0), 0, 0, 0)),
            _const_spec((2, 2 * n2, HEAD_DIM)),
            _const_spec((2 * n2, 4 * n2)),
        ] + [cast_spec(r, w) for r, w in zip(cast_rows, cast)],
        out_specs=[y_spec] + [cast_spec(r, w) for r, w in zip(cast_rows, cast)],
        out_shape=[y_shape] + [jax.ShapeDtypeStruct(w.shape, BF16) for w in cast],
        scratch_shapes=[
            pltpu.VMEM((2, 2, n1 * S2_BLOCK, HEAD_DIM), F32),
            pltpu.VMEM((N_HEADS, 2, nj, half, S2_BLOCK, HEAD_DIM), jnp.uint32),
        ],
        compiler_params=pltpu.CompilerParams(
            dimension_semantics=("parallel", "arbitrary"), vmem_limit_bytes=VMEM_LIMIT_V7X),
        name="k12_in_proj_dft",
    )(x.reshape(b, n1, n2, D_MODEL), g1, w_in_a, w2, f1, t0, tw, f3, *cast)


def _k3_kernel(x_ref, xp_ref, xn_ref, ya_ref, g1_ref, winb_ref, wouta_ref, wpo_ref, g2_ref,
               wg_ref, wu_ref, wd_ref, gf_ref, o_ref, hs_ref, zs_ref, act_ref, d_ref, h2_ref, *, seq_len):
    tm = x_ref.shape[1]
    th = tm // 2
    i = pl.program_id(1)
    last = pl.num_programs(1) - 1
    x = x_ref[0]
    g1 = g1_ref[...]
    g2 = g2_ref[...]
    gf = gf_ref[...]
    half_a = slice(0, th)
    half_b = slice(th, tm)

    ya = jnp.concatenate(
        [pltpu.bitcast(jnp.swapaxes(ya_ref[0, hd], 0, 1).reshape(tm // 2, HEAD_DIM), BF16)
         for hd in range(N_HEADS)], axis=-1)
    x1pb = x[half_b] + _dot(ya[half_b], wouta_ref[...])

    hs_ref[:HALO] = _rmsnorm(xp_ref[0], g1).astype(BF16)
    hs_ref[HALO:HALO + tm] = _rmsnorm(x, g1).astype(BF16)
    hs_ref[HALO + tm:] = _rmsnorm(xn_ref[0], g1).astype(BF16)
    zb = _dot(hs_ref[...], winb_ref[...])
    row = jax.lax.broadcasted_iota(jnp.int32, (tm + 2 * HALO, 1), 0)
    inside = ((row >= HALO) | (i > 0)) & ((row < HALO + tm) | (i < last))
    zb = jnp.where(inside, zb, 0.0)
    for gi in range(N_HEADS):
        zs_ref[gi] = zb[:, gi * HEAD_DIM:(gi + 1) * HEAD_DIM]

    def pool_rows(r0):
        n = VPU_SLICE
        t = i * tm + r0 + jax.lax.broadcasted_iota(jnp.int32, (n, 1), 0)
        d = []
        for gi, k in enumerate(POOL_WINDOWS):
            sums = zs_ref[gi, pl.ds(HALO + r0 - k // 2, n), :]
            for j in range(1, k):
                sums = sums + zs_ref[gi, pl.ds(HALO + r0 - k // 2 + j, n), :]
            cnt = jnp.minimum(t + k // 2, seq_len) - jnp.maximum(t - k // 2, 0)
            d.append((sums / cnt.astype(F32) - zs_ref[gi, pl.ds(HALO + r0, n), :]).astype(BF16))
        d_ref[pl.ds(r0, n), :] = jnp.concatenate(d, axis=-1)

    def ffn_chunk(rows, c):
        cols = slice(c * FF_CHUNK, (c + 1) * FF_CHUNK)
        h2 = h2_ref[rows, :]
        gate = _dot(h2, wg_ref[:, cols])
        up = _dot(h2, wu_ref[:, cols])
        act_ref[rows, cols] = (gate * jax.nn.sigmoid(gate) * up).astype(BF16)

    n_chunks = D_FF // FF_CHUNK
    n_slices = th // VPU_SLICE

    x1pa = []
    for q in range(PROJ_SPLIT):
        rows = slice(q * th // PROJ_SPLIT, (q + 1) * th // PROJ_SPLIT)
        x1pa.append(x[rows] + _dot(ya[rows], wouta_ref[...]))
        for s in range(q * n_slices // PROJ_SPLIT, (q + 1) * n_slices // PROJ_SPLIT):
            pool_rows(s * VPU_SLICE)
    x1a = jnp.concatenate(x1pa, axis=0) + _dot(d_ref[half_a, :], wpo_ref[...])
    h2_ref[half_a, :] = _rmsnorm(x1a, g2).astype(BF16)

    box = {}

    def proj_b():
        box["x1b"] = x1pb + _dot(d_ref[half_b, :], wpo_ref[...])

    def norm2_b(s):
        r = slice(s * VPU_SLICE, (s + 1) * VPU_SLICE)
        h2_ref[pl.ds(th + s * VPU_SLICE, VPU_SLICE), :] = _rmsnorm(box["x1b"][r], g2).astype(BF16)

    tasks = [functools.partial(pool_rows, th + s * VPU_SLICE) for s in range(n_slices)]
    tasks += [proj_b] + [functools.partial(norm2_b, s) for s in range(n_slices)]
    per_chunk = -(-len(tasks) // n_chunks)
    for c in range(n_chunks):
        ffn_chunk(half_a, c)
        for task in tasks[c * per_chunk:(c + 1) * per_chunk]:
            task()
    x2a = x1a + _dot(act_ref[half_a, :], wd_ref[...])

    def final_a(s):
        r = slice(s * VPU_SLICE, (s + 1) * VPU_SLICE)
        o_ref[0, r, :] = _rmsnorm(x2a[r], gf)

    tasks = [functools.partial(final_a, s) for s in range(n_slices)]
    per_chunk = -(-len(tasks) // n_chunks)
    for c in range(n_chunks):
        ffn_chunk(half_b, c)
        for task in tasks[c * per_chunk:(c + 1) * per_chunk]:
            task()
    x2b = box["x1b"] + _dot(act_ref[half_b, :], wd_ref[...])
    o_ref[0, half_b, :] = _rmsnorm(x2b, gf)


def _k3(x, ya, g1, w_in_b, w_out_a, w_po, g2, w_gate, w_up, w_down, gf):
    b, s, _ = x.shape
    tm = TOKEN_TILE
    hb = tm // HALO
    n_hb = s // HALO
    return pl.pallas_call(
        functools.partial(_k3_kernel, seq_len=s),
        grid=(b, s // tm),
        in_specs=[
            pl.BlockSpec((1, tm, D_MODEL), lambda bi, i: (bi, i, 0)),
            pl.BlockSpec((1, HALO, D_MODEL), lambda bi, i: (bi, jnp.maximum(i * hb - 1, 0), 0)),
            pl.BlockSpec((1, HALO, D_MODEL), lambda bi, i: (bi, jnp.minimum((i + 1) * hb, n_hb - 1), 0)),
            pl.BlockSpec((1, N_HEADS, DFT_SPLIT // (2 * K1_PAIRS), tm // DFT_SPLIT, K1_PAIRS, HEAD_DIM),
                         lambda bi, i: (bi, 0, 0, i, 0, 0)),
            _const_spec((1, D_MODEL)),
            _const_spec((D_MODEL, B_WIDTH)),
            _const_spec((A_WIDTH, D_MODEL)),
            _const_spec((B_WIDTH, D_MODEL)),
            _const_spec((1, D_MODEL)),
            _const_spec((D_MODEL, D_FF)),
            _const_spec((D_MODEL, D_FF)),
            _const_spec((D_FF, D_MODEL)),
            _const_spec((1, D_MODEL)),
        ],
        out_specs=pl.BlockSpec((1, tm, D_MODEL), lambda bi, i: (bi, i, 0)),
        out_shape=jax.ShapeDtypeStruct((b, s, D_MODEL), F32),
        scratch_shapes=[
            pltpu.VMEM((tm + 2 * HALO, D_MODEL), BF16),
            pltpu.VMEM((N_HEADS, tm + 2 * HALO, HEAD_DIM), F32),
            pltpu.VMEM((tm, D_FF), BF16),
            pltpu.VMEM((tm, B_WIDTH), BF16),
            pltpu.VMEM((tm, D_MODEL), BF16),
        ],
        compiler_params=pltpu.CompilerParams(
            dimension_semantics=("parallel", "parallel"), vmem_limit_bytes=VMEM_LIMIT_V7X),
        name="k3_pool_out_ffn",
    )(x, x, x, ya, g1, w_in_b, w_out_a, w_po, g2, w_gate, w_up, w_down, gf)


def kernel(x_prompt, x_sample, norm1_g, w_in, w_fourier, w_pool, pool_scale, w_out, norm2_g,
           w_gate, w_up, w_down, normf_g):
    assert norm1_g.shape[0] == 1, "single-layer block"
    g1 = norm1_g[0][None]
    g2 = norm2_g[0][None]
    gf = normf_g[None]
    w2, w_po, w_in_a, w_in_b, w_o_a = _prep(w_fourier[0], w_pool[0], pool_scale[0][None], w_out[0], w_in[0])

    ya, w_g, w_u, w_d = _k12(x_prompt, g1, w_in_a, w2, cast=(w_gate[0], w_up[0], w_down[0]))
    y_prompt = _k3(x_prompt, ya, g1, w_in_b, w_o_a, w_po, g2, w_g, w_u, w_d, gf)
    y_prompt, x_sample = jax.lax.optimization_barrier((y_prompt, x_sample))
    (ya,) = _k12(x_sample, g1, w_in_a, w2)
    return y_prompt, _k3(x_sample, ya, g1, w_in_b, w_o_a, w_po, g2, w_g, w_u, w_d, gf)
```

```python
import functools

import jax
import jax.numpy as jnp
import numpy as np
from jax.experimental import pallas as pl
from jax.experimental.pallas import tpu as pltpu

F32 = jnp.float32
BF16 = jnp.bfloat16

D_MODEL = 1024
N_HEADS = 4
HEAD_DIM = 128
A_WIDTH = N_HEADS * HEAD_DIM
B_WIDTH = N_HEADS * HEAD_DIM
POOL_WINDOWS = (2, 4, 8, 16)
D_FF = 2816
EPS = 1e-6

HALO = 16
TOKEN_TILE = 1024
FF_CHUNK = 256
VPU_SLICE = 64
PROJ_SPLIT = 2
DFT_SPLIT = 128
S2_BLOCK = 8
MAX_S2_STEP = 2
K1_PAIRS = 8
MAX_PAIR_BLOCKS = 4
K12_FIXED_VMEM = 7 * 1024 * 1024
VMEM_LIMIT_V7X = 58 * 1024 * 1024
K3_VMEM_LIMIT_V7X = 55 * 1024 * 1024


def _const_spec(shape):
    zeros = (0,) * len(shape)
    return pl.BlockSpec(shape, lambda *_: zeros, pipeline_mode=pl.Buffered(1))


def _rmsnorm(x, g):
    r = jax.lax.rsqrt(jnp.mean(x * x, axis=-1, keepdims=True) + EPS)
    return x * r * g


def _dot(a, b):
    return jnp.dot(a, b, preferred_element_type=F32)


def _split_bf16(a):
    hi = a.astype(BF16)
    lo = (a - hi.astype(F32)).astype(BF16)
    return hi, lo


def _dot_f32(a, b):
    ah, al = _split_bf16(a)
    bh, bl = _split_bf16(b)
    return _dot(ah, bh) + (_dot(ah, bl) + _dot(al, bh))


def _prep_kernel(cs_ref, wf_ref, wp_ref, ps_ref, wout_ref, win_ref, w2_ref, wpo_ref, wina_ref, winb_ref,
                 wouta_ref):
    wina_ref[...] = win_ref[:, :A_WIDTH].astype(BF16)
    winb_ref[...] = win_ref[:, A_WIDTH:].astype(BF16)
    wouta_ref[...] = wout_ref[:A_WIDTH, :].astype(BF16)
    for h in range(N_HEADS):
        w = wf_ref[h]
        w2_ref[h, :, :HEAD_DIM] = _dot_f32(cs_ref[0], w).astype(BF16)
        w2_ref[h, :, HEAD_DIM:] = (-_dot_f32(cs_ref[1], w)).astype(BF16)
    for g in range(N_HEADS):
        lanes = slice(g * HEAD_DIM, (g + 1) * HEAD_DIM)
        rows = slice(A_WIDTH + g * HEAD_DIM, A_WIDTH + (g + 1) * HEAD_DIM)
        wpo_ref[lanes, :] = _dot_f32(wp_ref[g] * ps_ref[:, lanes], wout_ref[rows, :]).astype(BF16)


def _prep(w_fourier, w_pool, pool_scale, w_out, w_in):
    n = np.arange(HEAD_DIM)
    ang = 2.0 * np.pi * np.outer(n, n) / HEAD_DIM
    cs = np.stack([np.cos(ang), np.sin(ang)]) / np.sqrt(HEAD_DIM)
    return pl.pallas_call(
        _prep_kernel,
        out_shape=(jax.ShapeDtypeStruct((N_HEADS, HEAD_DIM, 2 * HEAD_DIM), BF16),
                   jax.ShapeDtypeStruct((B_WIDTH, D_MODEL), BF16),
                   jax.ShapeDtypeStruct((D_MODEL, A_WIDTH), BF16),
                   jax.ShapeDtypeStruct((D_MODEL, B_WIDTH), BF16),
                   jax.ShapeDtypeStruct((A_WIDTH, D_MODEL), BF16)),
        name="prep_weights",
    )(jnp.asarray(cs, F32), w_fourier, w_pool, pool_scale, w_out, w_in)


def _stage1_group(x_ref, g1_ref, win_ref, w2_ref, f1_ref, gs_ref, ab_ref, sb, jg, n1):
    rows = n1 * S2_BLOCK
    half = n1 // 2
    x = x_ref[0, :, sb * S2_BLOCK:(sb + 1) * S2_BLOCK, :].reshape(rows, D_MODEL)
    h = _rmsnorm(x, g1_ref[...]).astype(BF16)
    za = _dot(h, win_ref[...]).astype(BF16)
    for hd in range(N_HEADS):
        ab = _dot(za[:, hd * HEAD_DIM:(hd + 1) * HEAD_DIM], w2_ref[hd])
        buf = ab_ref.at[hd % 2]
        buf[0] = ab[:, :HEAD_DIM]
        buf[1] = ab[:, HEAD_DIM:]
        xx = jnp.concatenate(
            [jnp.concatenate([buf.at[ri][pl.ds(j, n1, stride=S2_BLOCK), :]
                              for j in range(S2_BLOCK)], axis=1) for ri in range(2)],
            axis=0).astype(BF16)
        g = _dot(f1_ref[...], xx)
        gp = pltpu.bitcast(g.astype(BF16), jnp.uint32)
        for ri in range(2):
            out = gs_ref.at[hd, ri, jg].reshape(half * S2_BLOCK, HEAD_DIM)
            for j in range(S2_BLOCK):
                out[pl.ds(j, half, stride=S2_BLOCK), :] = (
                    gp[ri * half:(ri + 1) * half, j * HEAD_DIM:(j + 1) * HEAD_DIM])


def _stage2_pairs(gs_ref, t0_ref, tw_ref, f3_ref, y_ref, jb, n2, pair_blocks):
    pairs = pair_blocks * K1_PAIRS
    tr = t0_ref[0, 0]
    ts = t0_ref[0, 1]
    for p in range(pairs):
        if p:
            tr, ts = tr * tw_ref[0] - ts * tw_ref[1], tr * tw_ref[1] + ts * tw_ref[0]
        pr = []
        pi = []
        for hd in range(N_HEADS):
            wr = gs_ref[hd, 0, :, jb * pairs + p].reshape(n2, HEAD_DIM)
            wi = gs_ref[hd, 1, :, jb * pairs + p].reshape(n2, HEAD_DIM)
            gr = pltpu.bitcast(wr, BF16).astype(F32)
            gi = pltpu.bitcast(wi, BF16).astype(F32)
            pr.append(gr * tr + gi * ts)
            pi.append(gi * tr - gr * ts)
        x = jnp.concatenate([jnp.concatenate(pr, axis=1), jnp.concatenate(pi, axis=1)],
                            axis=0).astype(BF16)
        r = _dot(f3_ref[...], x)
        ru = pltpu.bitcast(r.astype(BF16), jnp.uint32)
        for hd in range(N_HEADS):
            out = y_ref.at[0, hd, p // K1_PAIRS].reshape(n2 * K1_PAIRS, HEAD_DIM)
            out[pl.ds(p % K1_PAIRS, n2, stride=K1_PAIRS), :] = ru[:, hd * HEAD_DIM:(hd + 1) * HEAD_DIM]


def _k12_kernel(x_ref, g1_ref, win_ref, w2_ref, f1_ref, t0_ref, tw_ref, f3_ref, *refs,
                n1, n2, s2_step, pair_blocks, n_cast):
    cast_in, y_ref, cast_out = refs[:n_cast], refs[n_cast], refs[n_cast + 1:2 * n_cast + 1]
    ab_ref, gs_ref = refs[2 * n_cast + 1:]
    j = pl.program_id(1)
    n_s1 = n2 // (s2_step * S2_BLOCK)

    @pl.when(j < n_s1)
    def _():
        for sb in range(s2_step):
            _stage1_group(x_ref, g1_ref, win_ref, w2_ref, f1_ref, gs_ref, ab_ref, sb, j * s2_step + sb, n1)
        for src, dst in zip(cast_in, cast_out):
            dst[...] = src[...].astype(BF16)

    @pl.when(j >= n_s1)
    def _():
        _stage2_pairs(gs_ref, t0_ref, tw_ref, f3_ref, y_ref, j - n_s1, n2, pair_blocks)


def _dft_tables(s, n1, pair_blocks):
    n2 = s // n1
    k1 = np.arange(n1)
    a1 = 2.0 * np.pi * np.outer(k1, k1) / n1
    c1, s1 = np.cos(a1), np.sin(a1)
    f1 = np.block([[c1, s1], [-s1, c1]]) / np.sqrt(n1)
    at = 2.0 * np.pi * np.outer(k1, np.arange(n2)) / s
    k2 = np.arange(n2)
    a2 = 2.0 * np.pi * np.outer(k2, k2) / n2
    f3 = np.concatenate([np.cos(a2), np.sin(a2)], axis=1) / np.sqrt(n2)
    f3p = np.einsum("krs,ef->kersf", f3.reshape(n2, 2, n2), np.eye(2)).reshape(2 * n2, 4 * n2)
    lanes = np.ones((1, 1, HEAD_DIM))
    atp = at.reshape(n1 // 2, 2, n2).transpose(0, 2, 1).reshape(n1 // 2, 2 * n2)
    at0 = atp[::pair_blocks * K1_PAIRS]
    t0 = np.stack([np.cos(at0), np.sin(at0)], axis=1)[:, :, :, None] * lanes
    a_next = np.repeat(2.0 * np.pi * 2.0 * np.arange(n2) / s, 2)
    tw = np.stack([np.cos(a_next), np.sin(a_next)])[:, :, None] * lanes
    return (jnp.asarray(f1, F32).astype(BF16), jnp.asarray(t0, F32), jnp.asarray(tw, F32),
            jnp.asarray(f3p, F32).astype(BF16))


def _k12_plan(s, cast):
    n2 = s // DFT_SPLIT
    spectrum = s * A_WIDTH * 4
    x_group = 2 * DFT_SPLIT * S2_BLOCK * D_MODEL * 4
    y_block = 2 * N_HEADS * n2 * K1_PAIRS * HEAD_DIM * 4
    for s2_step in range(MAX_S2_STEP, 0, -1):
        n_s1 = n2 // (s2_step * S2_BLOCK)
        cast_bytes = sum(2 * (4 + 2) * w.size // n_s1 for w in cast)
        for pair_blocks in range(MAX_PAIR_BLOCKS, 0, -1):
            need = spectrum + K12_FIXED_VMEM + s2_step * x_group + pair_blocks * y_block + cast_bytes
            if need <= VMEM_LIMIT_V7X:
                return s2_step, pair_blocks
    raise ValueError(f"k12 does not fit VMEM for sequence length {s}")


def _k12(x, g1, w_in_a, w2, cast=()):
    b, s, _ = x.shape
    n1 = DFT_SPLIT
    n2 = s // n1
    half = n1 // 2
    nj = n2 // S2_BLOCK
    s2_step, pair_blocks = _k12_plan(s, cast)
    f1, t0, tw, f3 = _dft_tables(s, n1, pair_blocks)
    n_s1 = n2 // (s2_step * S2_BLOCK)
    n_s2 = half // (pair_blocks * K1_PAIRS)
    cast_rows = [w.shape[0] // n_s1 for w in cast]
    assert all(r * n_s1 == w.shape[0] and r % 16 == 0 for r, w in zip(cast_rows, cast)), cast_rows

    def cast_spec(r, w):
        return pl.BlockSpec((r, w.shape[1]), lambda bi, j: (jnp.minimum(j, n_s1 - 1), 0))

    y_spec = pl.BlockSpec((1, N_HEADS, pair_blocks, n2, K1_PAIRS, HEAD_DIM),
                          lambda bi, j: (bi, 0, jnp.maximum(j - n_s1, 0), 0, 0, 0))
    y_shape = jax.ShapeDtypeStruct((b, N_HEADS, half // K1_PAIRS, n2, K1_PAIRS, HEAD_DIM), jnp.uint32)
    return pl.pallas_call(
        functools.partial(_k12_kernel, n1=n1, n2=n2, s2_step=s2_step, pair_blocks=pair_blocks,
                          n_cast=len(cast)),
        grid=(b, n_s1 + n_s2),
        in_specs=[
            pl.BlockSpec((1, n1, s2_step * S2_BLOCK, D_MODEL),
                         lambda bi, j: (bi, 0, jnp.minimum(j, n_s1 - 1), 0)),
            _const_spec((1, D_MODEL)),
            _const_spec((D_MODEL, A_WIDTH)),
            _const_spec((N_HEADS, HEAD_DIM, 2 * HEAD_DIM)),
            _const_spec((2 * n1, 2 * n1)),
            pl.BlockSpec((1, 2, 2 * n2, HEAD_DIM), lambda bi, j: (jnp.maximum(j - n_s1, 0), 0, 0, 0)),
            _const_spec((2, 2 * n2, HEAD_DIM)),
            _const_spec((2 * n2, 4 * n2)),
        ] + [cast_spec(r, w) for r, w in zip(cast_rows, cast)],
        out_specs=[y_spec] + [cast_spec(r, w) for r, w in zip(cast_rows, cast)],
        out_shape=[y_shape] + [jax.ShapeDtypeStruct(w.shape, BF16) for w in cast],
        scratch_shapes=[
            pltpu.VMEM((2, 2, n1 * S2_BLOCK, HEAD_DIM), F32),
            pltpu.VMEM((N_HEADS, 2, nj, half, S2_BLOCK, HEAD_DIM), jnp.uint32),
        ],
        compiler_params=pltpu.CompilerParams(
            dimension_semantics=("parallel", "arbitrary"), vmem_limit_bytes=VMEM_LIMIT_V7X),
        name="k12_in_proj_dft",
    )(x.reshape(b, n1, n2, D_MODEL), g1, w_in_a, w2, f1, t0, tw, f3, *cast)


def _k3_kernel(x_ref, xp_ref, xn_ref, ya_ref, g1_ref, winb_ref, wouta_ref, wpo_ref, g2_ref,
               wg_ref, wu_ref, wd_ref, gf_ref, o_ref, hs_ref, zs_ref, act_ref, d_ref, h2_ref, *, seq_len):
    tm = x_ref.shape[1]
    th = tm // 2
    i = pl.program_id(1)
    last = pl.num_programs(1) - 1
    x = x_ref[0]
    g1 = g1_ref[...]
    g2 = g2_ref[...]
    gf = gf_ref[...]
    half_a = slice(0, th)
    half_b = slice(th, tm)

    ya = jnp.concatenate(
        [pltpu.bitcast(jnp.swapaxes(ya_ref[0, hd], 0, 1).reshape(tm // 2, HEAD_DIM), BF16)
         for hd in range(N_HEADS)], axis=-1)
    x1pb = x[half_b] + _dot(ya[half_b], wouta_ref[...])

    hs_ref[:HALO] = _rmsnorm(xp_ref[0], g1).astype(BF16)
    hs_ref[HALO:HALO + tm] = _rmsnorm(x, g1).astype(BF16)
    hs_ref[HALO + tm:] = _rmsnorm(xn_ref[0], g1).astype(BF16)
    zb = _dot(hs_ref[...], winb_ref[...])
    row = jax.lax.broadcasted_iota(jnp.int32, (tm + 2 * HALO, 1), 0)
    inside = ((row >= HALO) | (i > 0)) & ((row < HALO + tm) | (i < last))
    zb = jnp.where(inside, zb, 0.0)
    for gi in range(N_HEADS):
        zs_ref[gi] = zb[:, gi * HEAD_DIM:(gi + 1) * HEAD_DIM]

    def pool_rows(r0):
        n = VPU_SLICE
        t = i * tm + r0 + jax.lax.broadcasted_iota(jnp.int32, (n, 1), 0)
        d = []
        for gi, k in enumerate(POOL_WINDOWS):
            sums = zs_ref[gi, pl.ds(HALO + r0 - k // 2, n), :]
            for j in range(1, k):
                sums = sums + zs_ref[gi, pl.ds(HALO + r0 - k // 2 + j, n), :]
            cnt = jnp.minimum(t + k // 2, seq_len) - jnp.maximum(t - k // 2, 0)
            d.append((sums / cnt.astype(F32) - zs_ref[gi, pl.ds(HALO + r0, n), :]).astype(BF16))
        d_ref[pl.ds(r0, n), :] = jnp.concatenate(d, axis=-1)

    def ffn_chunk(rows, c):
        cols = slice(c * FF_CHUNK, (c + 1) * FF_CHUNK)
        h2 = h2_ref[rows, :]
        gate = _dot(h2, wg_ref[:, cols])
        up = _dot(h2, wu_ref[:, cols])
        act_ref[rows, cols] = (gate * jax.nn.sigmoid(gate) * up).astype(BF16)

    n_chunks = D_FF // FF_CHUNK
    n_slices = th // VPU_SLICE

    x1pa = []
    for q in range(PROJ_SPLIT):
        rows = slice(q * th // PROJ_SPLIT, (q + 1) * th // PROJ_SPLIT)
        x1pa.append(x[rows] + _dot(ya[rows], wouta_ref[...]))
        for s in range(q * n_slices // PROJ_SPLIT, (q + 1) * n_slices // PROJ_SPLIT):
            pool_rows(s * VPU_SLICE)
    x1a = jnp.concatenate(x1pa, axis=0) + _dot(d_ref[half_a, :], wpo_ref[...])
    h2_ref[half_a, :] = _rmsnorm(x1a, g2).astype(BF16)

    box = {}

    def proj_b():
        box["x1b"] = x1pb + _dot(d_ref[half_b, :], wpo_ref[...])

    def norm2_b(s):
        r = slice(s * VPU_SLICE, (s + 1) * VPU_SLICE)
        h2_ref[pl.ds(th + s * VPU_SLICE, VPU_SLICE), :] = _rmsnorm(box["x1b"][r], g2).astype(BF16)

    tasks = [functools.partial(pool_rows, th + s * VPU_SLICE) for s in range(n_slices)]
    tasks += [proj_b] + [functools.partial(norm2_b, s) for s in range(n_slices)]
    per_chunk = -(-len(tasks) // n_chunks)
    for c in range(n_chunks):
        ffn_chunk(half_a, c)
        for task in tasks[c * per_chunk:(c + 1) * per_chunk]:
            task()
    x2a = x1a + _dot(act_ref[half_a, :], wd_ref[...])

    def final_a(s):
        r = slice(s * VPU_SLICE, (s + 1) * VPU_SLICE)
        o_ref[0, r, :] = _rmsnorm(x2a[r], gf)

    tasks = [functools.partial(final_a, s) for s in range(n_slices)]
    per_chunk = -(-len(tasks) // n_chunks)
    for c in range(n_chunks):
        ffn_chunk(half_b, c)
        for task in tasks[c * per_chunk:(c + 1) * per_chunk]:
            task()
    x2b = box["x1b"] + _dot(act_ref[half_b, :], wd_ref[...])
    o_ref[0, half_b, :] = _rmsnorm(x2b, gf)


def _k3(x, ya, g1, w_in_b, w_out_a, w_po, g2, w_gate, w_up, w_down, gf):
    b, s, _ = x.shape
    tm = TOKEN_TILE
    hb = tm // HALO
    n_hb = s // HALO
    return pl.pallas_call(
        functools.partial(_k3_kernel, seq_len=s),
        grid=(b, s // tm),
        in_specs=[
            pl.BlockSpec((1, tm, D_MODEL), lambda bi, i: (bi, i, 0)),
            pl.BlockSpec((1, HALO, D_MODEL), lambda bi, i: (bi, jnp.maximum(i * hb - 1, 0), 0)),
            pl.BlockSpec((1, HALO, D_MODEL), lambda bi, i: (bi, jnp.minimum((i + 1) * hb, n_hb - 1), 0)),
            pl.BlockSpec((1, N_HEADS, DFT_SPLIT // (2 * K1_PAIRS), tm // DFT_SPLIT, K1_PAIRS, HEAD_DIM),
                         lambda bi, i: (bi, 0, 0, i, 0, 0)),
            _const_spec((1, D_MODEL)),
            _const_spec((D_MODEL, B_WIDTH)),
            _const_spec((A_WIDTH, D_MODEL)),
            _const_spec((B_WIDTH, D_MODEL)),
            _const_spec((1, D_MODEL)),
            _const_spec((D_MODEL, D_FF)),
            _const_spec((D_MODEL, D_FF)),
            _const_spec((D_FF, D_MODEL)),
            _const_spec((1, D_MODEL)),
        ],
        out_specs=pl.BlockSpec((1, tm, D_MODEL), lambda bi, i: (bi, i, 0)),
        out_shape=jax.ShapeDtypeStruct((b, s, D_MODEL), F32),
        scratch_shapes=[
            pltpu.VMEM((tm + 2 * HALO, D_MODEL), BF16),
            pltpu.VMEM((N_HEADS, tm + 2 * HALO, HEAD_DIM), F32),
            pltpu.VMEM((tm, D_FF), BF16),
            pltpu.VMEM((tm, B_WIDTH), BF16),
            pltpu.VMEM((tm, D_MODEL), BF16),
        ],
        compiler_params=pltpu.CompilerParams(
            dimension_semantics=("parallel", "parallel"), vmem_limit_bytes=K3_VMEM_LIMIT_V7X),
        name="k3_pool_out_ffn",
    )(x, x, x, ya, g1, w_in_b, w_out_a, w_po, g2, w_gate, w_up, w_down, gf)


def kernel(x_prompt, x_sample, norm1_g, w_in, w_fourier, w_pool, pool_scale, w_out, norm2_g,
           w_gate, w_up, w_down, normf_g):
    assert norm1_g.shape[0] == 1, "single-layer block"
    g1 = norm1_g[0][None]
    g2 = norm2_g[0][None]
    gf = normf_g[None]
    w2, w_po, w_in_a, w_in_b, w_o_a = _prep(w_fourier[0], w_pool[0], pool_scale[0][None], w_out[0], w_in[0])

    ya, w_g, w_u, w_d = _k12(x_prompt, g1, w_in_a, w2, cast=(w_gate[0], w_up[0], w_down[0]))
    y_prompt = _k3(x_prompt, ya, g1, w_in_b, w_o_a, w_po, g2, w_g, w_u, w_d, gf)
    y_prompt, x_sample = jax.lax.optimization_barrier((y_prompt, x_sample))
    (ya,) = _k12(x_sample, g1, w_in_a, w2)
    return y_prompt, _k3(x_sample, ya, g1, w_in_b, w_o_a, w_po, g2, w_g, w_u, w_d, gf)
```

```python
import functools

import jax
import jax.numpy as jnp
import numpy as np
from jax.experimental import pallas as pl
from jax.experimental.pallas import tpu as pltpu

F32 = jnp.float32
BF16 = jnp.bfloat16

D_MODEL = 1024
N_HEADS = 4
HEAD_DIM = 128
A_WIDTH = N_HEADS * HEAD_DIM
B_WIDTH = N_HEADS * HEAD_DIM
POOL_WINDOWS = (2, 4, 8, 16)
D_FF = 2816
EPS = 1e-6

HALO = 16
TOKEN_TILE = 1024
FF_CHUNK = 256
VPU_SLICE = 64
PROJ_SPLIT = 2
DFT_SPLIT = 128
S2_BLOCK = 8
MAX_S2_STEP = 2
K1_PAIRS = 8
MAX_PAIR_BLOCKS = 4
K12_FIXED_VMEM = 7 * 1024 * 1024
VMEM_LIMIT_V7X = 58 * 1024 * 1024


def _const_spec(shape):
    zeros = (0,) * len(shape)
    return pl.BlockSpec(shape, lambda *_: zeros, pipeline_mode=pl.Buffered(1))


def _rmsnorm(x, g):
    r = jax.lax.rsqrt(jnp.mean(x * x, axis=-1, keepdims=True) + EPS)
    return x * r * g


def _dot(a, b):
    return jnp.dot(a, b, preferred_element_type=F32)


def _split_bf16(a):
    hi = a.astype(BF16)
    lo = (a - hi.astype(F32)).astype(BF16)
    return hi, lo


def _dot_f32(a, b):
    ah, al = _split_bf16(a)
    bh, bl = _split_bf16(b)
    return _dot(ah, bh) + (_dot(ah, bl) + _dot(al, bh))


def _prep_kernel(cs_ref, wf_ref, wp_ref, ps_ref, wout_ref, win_ref, w2_ref, wpo_ref, wina_ref, winb_ref,
                 wouta_ref):
    wina_ref[...] = win_ref[:, :A_WIDTH].astype(BF16)
    winb_ref[...] = win_ref[:, A_WIDTH:].astype(BF16)
    wouta_ref[...] = wout_ref[:A_WIDTH, :].astype(BF16)
    for h in range(N_HEADS):
        w = wf_ref[h]
        w2_ref[h, :, :HEAD_DIM] = _dot_f32(cs_ref[0], w).astype(BF16)
        w2_ref[h, :, HEAD_DIM:] = (-_dot_f32(cs_ref[1], w)).astype(BF16)
    for g in range(N_HEADS):
        lanes = slice(g * HEAD_DIM, (g + 1) * HEAD_DIM)
        rows = slice(A_WIDTH + g * HEAD_DIM, A_WIDTH + (g + 1) * HEAD_DIM)
        wpo_ref[lanes, :] = _dot_f32(wp_ref[g] * ps_ref[:, lanes], wout_ref[rows, :]).astype(BF16)


def _prep(w_fourier, w_pool, pool_scale, w_out, w_in):
    n = np.arange(HEAD_DIM)
    ang = 2.0 * np.pi * np.outer(n, n) / HEAD_DIM
    cs = np.stack([np.cos(ang), np.sin(ang)]) / np.sqrt(HEAD_DIM)
    return pl.pallas_call(
        _prep_kernel,
        out_shape=(jax.ShapeDtypeStruct((N_HEADS, HEAD_DIM, 2 * HEAD_DIM), BF16),
                   jax.ShapeDtypeStruct((B_WIDTH, D_MODEL), BF16),
                   jax.ShapeDtypeStruct((D_MODEL, A_WIDTH), BF16),
                   jax.ShapeDtypeStruct((D_MODEL, B_WIDTH), BF16),
                   jax.ShapeDtypeStruct((A_WIDTH, D_MODEL), BF16)),
        name="prep_weights",
    )(jnp.asarray(cs, F32), w_fourier, w_pool, pool_scale, w_out, w_in)


def _stage1_group(x_ref, g1_ref, win_ref, w2_ref, f1_ref, gs_ref, ab_ref, sb, jg, n1):
    rows = n1 * S2_BLOCK
    half = n1 // 2
    x = x_ref[0, :, sb * S2_BLOCK:(sb + 1) * S2_BLOCK, :].reshape(rows, D_MODEL)
    h = _rmsnorm(x, g1_ref[...]).astype(BF16)
    za = _dot(h, win_ref[...]).astype(BF16)
    for hd in range(N_HEADS):
        ab = _dot(za[:, hd * HEAD_DIM:(hd + 1) * HEAD_DIM], w2_ref[hd])
        buf = ab_ref.at[hd % 2]
        buf[0] = ab[:, :HEAD_DIM]
        buf[1] = ab[:, HEAD_DIM:]
        xx = jnp.concatenate(
            [jnp.concatenate([buf.at[ri][pl.ds(j, n1, stride=S2_BLOCK), :]
                              for j in range(S2_BLOCK)], axis=1) for ri in range(2)],
            axis=0).astype(BF16)
        g = _dot(f1_ref[...], xx)
        gp = pltpu.bitcast(g.astype(BF16), jnp.uint32)
        for ri in range(2):
            out = gs_ref.at[hd, ri, jg].reshape(half * S2_BLOCK, HEAD_DIM)
            for j in range(S2_BLOCK):
                out[pl.ds(j, half, stride=S2_BLOCK), :] = (
                    gp[ri * half:(ri + 1) * half, j * HEAD_DIM:(j + 1) * HEAD_DIM])


def _stage2_pairs(gs_ref, t0_ref, tw_ref, f3_ref, y_ref, jb, n2, pair_blocks):
    pairs = pair_blocks * K1_PAIRS
    tr = t0_ref[0, 0]
    ts = t0_ref[0, 1]
    for p in range(pairs):
        if p:
            tr, ts = tr * tw_ref[0] - ts * tw_ref[1], tr * tw_ref[1] + ts * tw_ref[0]
        pr = []
        pi = []
        for hd in range(N_HEADS):
            wr = gs_ref[hd, 0, :, jb * pairs + p].reshape(n2, HEAD_DIM)
            wi = gs_ref[hd, 1, :, jb * pairs + p].reshape(n2, HEAD_DIM)
            gr = pltpu.bitcast(wr, BF16).astype(F32)
            gi = pltpu.bitcast(wi, BF16).astype(F32)
            pr.append(gr * tr + gi * ts)
            pi.append(gi * tr - gr * ts)
        x = jnp.concatenate([jnp.concatenate(pr, axis=1), jnp.concatenate(pi, axis=1)],
                            axis=0).astype(BF16)
        r = _dot(f3_ref[...], x)
        ru = pltpu.bitcast(r.astype(BF16), jnp.uint32)
        for hd in range(N_HEADS):
            out = y_ref.at[0, hd, p // K1_PAIRS].reshape(n2 * K1_PAIRS, HEAD_DIM)
            out[pl.ds(p % K1_PAIRS, n2, stride=K1_PAIRS), :] = ru[:, hd * HEAD_DIM:(hd + 1) * HEAD_DIM]


def _k12_kernel(x_ref, g1_ref, win_ref, w2_ref, f1_ref, t0_ref, tw_ref, f3_ref, *refs,
                n1, n2, s2_step, pair_blocks, n_cast):
    cast_in, y_ref, cast_out = refs[:n_cast], refs[n_cast], refs[n_cast + 1:2 * n_cast + 1]
    ab_ref, gs_ref = refs[2 * n_cast + 1:]
    j = pl.program_id(1)
    n_s1 = n2 // (s2_step * S2_BLOCK)

    @pl.when(j < n_s1)
    def _():
        for sb in range(s2_step):
            _stage1_group(x_ref, g1_ref, win_ref, w2_ref, f1_ref, gs_ref, ab_ref, sb, j * s2_step + sb, n1)
        for src, dst in zip(cast_in, cast_out):
            dst[...] = src[...].astype(BF16)

    @pl.when(j >= n_s1)
    def _():
        _stage2_pairs(gs_ref, t0_ref, tw_ref, f3_ref, y_ref, j - n_s1, n2, pair_blocks)


def _dft_tables(s, n1, pair_blocks):
    n2 = s // n1
    k1 = np.arange(n1)
    a1 = 2.0 * np.pi * np.outer(k1, k1) / n1
    c1, s1 = np.cos(a1), np.sin(a1)
    f1 = np.block([[c1, s1], [-s1, c1]]) / np.sqrt(n1)
    at = 2.0 * np.pi * np.outer(k1, np.arange(n2)) / s
    k2 = np.arange(n2)
    a2 = 2.0 * np.pi * np.outer(k2, k2) / n2
    f3 = np.concatenate([np.cos(a2), np.sin(a2)], axis=1) / np.sqrt(n2)
    f3p = np.einsum("krs,ef->kersf", f3.reshape(n2, 2, n2), np.eye(2)).reshape(2 * n2, 4 * n2)
    lanes = np.ones((1, 1, HEAD_DIM))
    atp = at.reshape(n1 // 2, 2, n2).transpose(0, 2, 1).reshape(n1 // 2, 2 * n2)
    at0 = atp[::pair_blocks * K1_PAIRS]
    t0 = np.stack([np.cos(at0), np.sin(at0)], axis=1)[:, :, :, None] * lanes
    a_next = np.repeat(2.0 * np.pi * 2.0 * np.arange(n2) / s, 2)
    tw = np.stack([np.cos(a_next), np.sin(a_next)])[:, :, None] * lanes
    return (jnp.asarray(f1, F32).astype(BF16), jnp.asarray(t0, F32), jnp.asarray(tw, F32),
            jnp.asarray(f3p, F32).astype(BF16))


def _k12_plan(s, cast):
    n2 = s // DFT_SPLIT
    spectrum = s * A_WIDTH * 4
    x_group = 2 * DFT_SPLIT * S2_BLOCK * D_MODEL * 4
    y_block = 2 * N_HEADS * n2 * K1_PAIRS * HEAD_DIM * 4
    for s2_step in range(MAX_S2_STEP, 0, -1):
        n_s1 = n2 // (s2_step * S2_BLOCK)
        cast_bytes = sum(2 * (4 + 2) * w.size // n_s1 for w in cast)
        for pair_blocks in range(MAX_PAIR_BLOCKS, 0, -1):
            need = spectrum + K12_FIXED_VMEM + s2_step * x_group + pair_blocks * y_block + cast_bytes
            if need <= VMEM_LIMIT_V7X:
                return s2_step, pair_blocks
    raise ValueError(f"k12 does not fit VMEM for sequence length {s}")


def _k12(x, g1, w_in_a, w2, cast=()):
    b, s, _ = x.shape
    n1 = DFT_SPLIT
    n2 = s // n1
    half = n1 // 2
    nj = n2 // S2_BLOCK
    s2_step, pair_blocks = _k12_plan(s, cast)
    f1, t0, tw, f3 = _dft_tables(s, n1, pair_blocks)
    n_s1 = n2 // (s2_step * S2_BLOCK)
    n_s2 = half // (pair_blocks * K1_PAIRS)
    cast_rows = [w.shape[0] // n_s1 for w in cast]
    assert all(r * n_s1 == w.shape[0] and r % 16 == 0 for r, w in zip(cast_rows, cast)), cast_rows

    def cast_spec(r, w):
        return pl.BlockSpec((r, w.shape[1]), lambda bi, j: (jnp.minimum(j, n_s1 - 1), 0))

    y_spec = pl.BlockSpec((1, N_HEADS, pair_blocks, n2, K1_PAIRS, HEAD_DIM),
                          lambda bi, j: (bi, 0, jnp.maximum(j - n_s1, 0), 0, 0, 0))
    y_shape = jax.ShapeDtypeStruct((b, N_HEADS, half // K1_PAIRS, n2, K1_PAIRS, HEAD_DIM), jnp.uint32)
    return pl.pallas_call(
        functools.partial(_k12_kernel, n1=n1, n2=n2, s2_step=s2_step, pair_blocks=pair_blocks,
                          n_cast=len(cast)),
        grid=(b, n_s1 + n_s2),
        in_specs=[
            pl.BlockSpec((1, n1, s2_step * S2_BLOCK, D_MODEL),
                         lambda bi, j: (bi, 0, jnp.minimum(j, n_s1 - 1), 0)),
            _const_spec((1, D_MODEL)),
            _const_spec((D_MODEL, A_WIDTH)),
            _const_spec((N_HEADS, HEAD_DIM, 2 * HEAD_DIM)),
            _const_spec((2 * n1, 2 * n1)),
            pl.BlockSpec((1, 2, 2 * n2, HEAD_DIM), lambda bi, j: (jnp.maximum(j - n_s1, 0), 0, 0, 0)),
            _const_spec((2, 2 * n2, HEAD_DIM)),
            _const_spec((2 * n2, 4 * n2)),
        ] + [cast_spec(r, w) for r, w in zip(cast_rows, cast)],
        out_specs=[y_spec] + [cast_spec(r, w) for r, w in zip(cast_rows, cast)],
        out_shape=[y_shape] + [jax.ShapeDtypeStruct(w.shape, BF16) for w in cast],
        scratch_shapes=[
            pltpu.VMEM((2, 2, n1 * S2_BLOCK, HEAD_DIM), F32),
            pltpu.VMEM((N_HEADS, 2, nj, half, S2_BLOCK, HEAD_DIM), jnp.uint32),
        ],
        compiler_params=pltpu.CompilerParams(
            dimension_semantics=("parallel", "arbitrary"), vmem_limit_bytes=VMEM_LIMIT_V7X),
        name="k12_in_proj_dft",
    )(x.reshape(b, n1, n2, D_MODEL), g1, w_in_a, w2, f1, t0, tw, f3, *cast)


def _k3_kernel(x_ref, xp_ref, xn_ref, ya_ref, g1_ref, winb_ref, wouta_ref, wpo_ref, g2_ref,
               wg_hbm, wu_hbm, wd_hbm, gf_ref, o_ref, hs_ref, zs_ref, act_ref, d_ref, h2_ref,
               wg_ref, wu_ref, wd_ref, w_sem, *, seq_len):
    tm = x_ref.shape[1]
    th = tm // 2
    i = pl.program_id(1)
    last = pl.num_programs(1) - 1

    first_step = (pl.program_id(0) == 0) & (i == 0)
    w_copies = [pltpu.make_async_copy(src, dst, w_sem.at[n])
                for n, (src, dst) in enumerate(((wg_hbm, wg_ref), (wu_hbm, wu_ref), (wd_hbm, wd_ref)))]

    @pl.when(first_step)
    def _():
        for copy in w_copies:
            copy.start()
    x = x_ref[0]
    g1 = g1_ref[...]
    g2 = g2_ref[...]
    gf = gf_ref[...]
    half_a = slice(0, th)
    half_b = slice(th, tm)

    ya = jnp.concatenate(
        [pltpu.bitcast(jnp.swapaxes(ya_ref[0, hd], 0, 1).reshape(tm // 2, HEAD_DIM), BF16)
         for hd in range(N_HEADS)], axis=-1)
    x1pb = x[half_b] + _dot(ya[half_b], wouta_ref[...])

    hs_ref[:HALO] = _rmsnorm(xp_ref[0], g1).astype(BF16)
    hs_ref[HALO:HALO + tm] = _rmsnorm(x, g1).astype(BF16)
    hs_ref[HALO + tm:] = _rmsnorm(xn_ref[0], g1).astype(BF16)
    zb = _dot(hs_ref[...], winb_ref[...])
    row = jax.lax.broadcasted_iota(jnp.int32, (tm + 2 * HALO, 1), 0)
    inside = ((row >= HALO) | (i > 0)) & ((row < HALO + tm) | (i < last))
    zb = jnp.where(inside, zb, 0.0)
    for gi in range(N_HEADS):
        zs_ref[gi] = zb[:, gi * HEAD_DIM:(gi + 1) * HEAD_DIM]

    def pool_rows(r0):
        n = VPU_SLICE
        t = i * tm + r0 + jax.lax.broadcasted_iota(jnp.int32, (n, 1), 0)
        d = []
        for gi, k in enumerate(POOL_WINDOWS):
            sums = zs_ref[gi, pl.ds(HALO + r0 - k // 2, n), :]
            for j in range(1, k):
                sums = sums + zs_ref[gi, pl.ds(HALO + r0 - k // 2 + j, n), :]
            cnt = jnp.minimum(t + k // 2, seq_len) - jnp.maximum(t - k // 2, 0)
            d.append((sums / cnt.astype(F32) - zs_ref[gi, pl.ds(HALO + r0, n), :]).astype(BF16))
        d_ref[pl.ds(r0, n), :] = jnp.concatenate(d, axis=-1)

    def ffn_chunk(rows, c):
        cols = slice(c * FF_CHUNK, (c + 1) * FF_CHUNK)
        h2 = h2_ref[rows, :]
        gate = _dot(h2, wg_ref[:, cols])
        up = _dot(h2, wu_ref[:, cols])
        act_ref[rows, cols] = (gate * jax.nn.sigmoid(gate) * up).astype(BF16)

    n_chunks = D_FF // FF_CHUNK
    n_slices = th // VPU_SLICE

    x1pa = []
    for q in range(PROJ_SPLIT):
        rows = slice(q * th // PROJ_SPLIT, (q + 1) * th // PROJ_SPLIT)
        x1pa.append(x[rows] + _dot(ya[rows], wouta_ref[...]))
        for s in range(q * n_slices // PROJ_SPLIT, (q + 1) * n_slices // PROJ_SPLIT):
            pool_rows(s * VPU_SLICE)
    x1a = jnp.concatenate(x1pa, axis=0) + _dot(d_ref[half_a, :], wpo_ref[...])
    h2_ref[half_a, :] = _rmsnorm(x1a, g2).astype(BF16)

    @pl.when(first_step)
    def _():
        for copy in w_copies:
            copy.wait()

    box = {}

    def proj_b():
        box["x1b"] = x1pb + _dot(d_ref[half_b, :], wpo_ref[...])

    def norm2_b(s):
        r = slice(s * VPU_SLICE, (s + 1) * VPU_SLICE)
        h2_ref[pl.ds(th + s * VPU_SLICE, VPU_SLICE), :] = _rmsnorm(box["x1b"][r], g2).astype(BF16)

    tasks = [functools.partial(pool_rows, th + s * VPU_SLICE) for s in range(n_slices)]
    tasks += [proj_b] + [functools.partial(norm2_b, s) for s in range(n_slices)]
    per_chunk = -(-len(tasks) // n_chunks)
    for c in range(n_chunks):
        ffn_chunk(half_a, c)
        for task in tasks[c * per_chunk:(c + 1) * per_chunk]:
            task()
    x2a = x1a + _dot(act_ref[half_a, :], wd_ref[...])

    def final_a(s):
        r = slice(s * VPU_SLICE, (s + 1) * VPU_SLICE)
        o_ref[0, r, :] = _rmsnorm(x2a[r], gf)

    tasks = [functools.partial(final_a, s) for s in range(n_slices)]
    per_chunk = -(-len(tasks) // n_chunks)
    for c in range(n_chunks):
        ffn_chunk(half_b, c)
        for task in tasks[c * per_chunk:(c + 1) * per_chunk]:
            task()
    x2b = box["x1b"] + _dot(act_ref[half_b, :], wd_ref[...])
    o_ref[0, half_b, :] = _rmsnorm(x2b, gf)


def _k3(x, ya, g1, w_in_b, w_out_a, w_po, g2, w_gate, w_up, w_down, gf):
    b, s, _ = x.shape
    tm = TOKEN_TILE
    hb = tm // HALO
    n_hb = s // HALO
    return pl.pallas_call(
        functools.partial(_k3_kernel, seq_len=s),
        grid=(b, s // tm),
        in_specs=[
            pl.BlockSpec((1, tm, D_MODEL), lambda bi, i: (bi, i, 0)),
            pl.BlockSpec((1, HALO, D_MODEL), lambda bi, i: (bi, jnp.maximum(i * hb - 1, 0), 0)),
            pl.BlockSpec((1, HALO, D_MODEL), lambda bi, i: (bi, jnp.minimum((i + 1) * hb, n_hb - 1), 0)),
            pl.BlockSpec((1, N_HEADS, DFT_SPLIT // (2 * K1_PAIRS), tm // DFT_SPLIT, K1_PAIRS, HEAD_DIM),
                         lambda bi, i: (bi, 0, 0, i, 0, 0)),
            _const_spec((1, D_MODEL)),
            _const_spec((D_MODEL, B_WIDTH)),
            _const_spec((A_WIDTH, D_MODEL)),
            _const_spec((B_WIDTH, D_MODEL)),
            _const_spec((1, D_MODEL)),
            pl.BlockSpec(memory_space=pl.ANY),
            pl.BlockSpec(memory_space=pl.ANY),
            pl.BlockSpec(memory_space=pl.ANY),
            _const_spec((1, D_MODEL)),
        ],
        out_specs=pl.BlockSpec((1, tm, D_MODEL), lambda bi, i: (bi, i, 0)),
        out_shape=jax.ShapeDtypeStruct((b, s, D_MODEL), F32),
        scratch_shapes=[
            pltpu.VMEM((tm + 2 * HALO, D_MODEL), BF16),
            pltpu.VMEM((N_HEADS, tm + 2 * HALO, HEAD_DIM), F32),
            pltpu.VMEM((tm, D_FF), BF16),
            pltpu.VMEM((tm, B_WIDTH), BF16),
            pltpu.VMEM((tm, D_MODEL), BF16),
            pltpu.VMEM((D_MODEL, D_FF), BF16),
            pltpu.VMEM((D_MODEL, D_FF), BF16),
            pltpu.VMEM((D_FF, D_MODEL), BF16),
            pltpu.SemaphoreType.DMA((3,)),
        ],
        compiler_params=pltpu.CompilerParams(
            dimension_semantics=("arbitrary", "arbitrary"), vmem_limit_bytes=VMEM_LIMIT_V7X),
        name="k3_pool_out_ffn",
    )(x, x, x, ya, g1, w_in_b, w_out_a, w_po, g2, w_gate, w_up, w_down, gf)


def kernel(x_prompt, x_sample, norm1_g, w_in, w_fourier, w_pool, pool_scale, w_out, norm2_g,
           w_gate, w_up, w_down, normf_g):
    assert norm1_g.shape[0] == 1, "single-layer block"
    g1 = norm1_g[0][None]
    g2 = norm2_g[0][None]
    gf = normf_g[None]
    w2, w_po, w_in_a, w_in_b, w_o_a = _prep(w_fourier[0], w_pool[0], pool_scale[0][None], w_out[0], w_in[0])

    ya, w_g, w_u, w_d = _k12(x_prompt, g1, w_in_a, w2, cast=(w_gate[0], w_up[0], w_down[0]))
    y_prompt = _k3(x_prompt, ya, g1, w_in_b, w_o_a, w_po, g2, w_g, w_u, w_d, gf)
    y_prompt, x_sample = jax.lax.optimization_barrier((y_prompt, x_sample))
    (ya,) = _k12(x_sample, g1, w_in_a, w2)
    return y_prompt, _k3(x_sample, ya, g1, w_in_b, w_o_a, w_po, g2, w_g, w_u, w_d, gf)
```
